```python
import math
import jax, jax.numpy as jnp
from jax import lax
import numpy as np

D_MODEL = 1024
BATCH = 4
SEQ = 4096
DEPTH = 2

HEAD_DIM = 64
A_HEADS = 6
A_WIDTH = A_HEADS * HEAD_DIM
GMLP_CHUNK = 128
POOL_WINDOWS = (2, 4, 8, 16)
B_GROUPS = len(POOL_WINDOWS)
B_GROUP_DIM = 64
B_WIDTH = B_GROUPS * B_GROUP_DIM
DILATED_CONFIGS = ((128, 1), (512, 4), (2048, 16))
C_HEADS_PER_GROUP = 2
C_GROUPS = len(DILATED_CONFIGS)
C_HEADS = C_GROUPS * C_HEADS_PER_GROUP
C_WIDTH = C_HEADS * HEAD_DIM
C_OUT_WIDTH = C_HEADS_PER_GROUP * HEAD_DIM
MIX_WIDTH = A_WIDTH + B_WIDTH + C_WIDTH
IN_WIDTH = 2 * A_WIDTH + B_WIDTH + 3 * C_WIDTH
OUT_WIDTH = A_WIDTH + B_WIDTH + C_OUT_WIDTH
N_BUCKETS = 32
MAX_DISTANCE = 1024
MEM_LEN = 256
X_HEADS = 4
X_HEAD_DIM = D_MODEL // X_HEADS
D_FF = 128 * ((8 * D_MODEL // 3 + 127) // 128)
CONV_WIDTH = 3
EPS = 1e-6
NEG_INF = -1e30

kernel_name = "hybrid_gmlp_pool_dilated_encoder"


def _rmsnorm(x, g):
    x32 = x.astype(jnp.float32)
    y = x32 * lax.rsqrt(jnp.mean(x32 * x32, axis=-1, keepdims=True) + EPS)
    return (y * g.astype(jnp.float32)).astype(x.dtype)


def _t5_bucket(rel):
    nb = N_BUCKETS // 2
    ret = (rel > 0).astype(np.int32) * nb
    n = np.abs(rel)
    max_exact = nb // 2
    large = max_exact + (np.log(np.maximum(n, 1) / max_exact)
                         / math.log(MAX_DISTANCE / max_exact) * (nb - max_exact)).astype(np.int32)
    large = np.minimum(large, nb - 1)
    return ret + np.where(n < max_exact, n, large)


def _spatial_gating(z_uv, v_gain, w_s, b_s):
    B, S, _ = z_uv.shape
    z = jax.nn.gelu(z_uv)
    u, v = jnp.split(z, 2, axis=-1)
    v = _rmsnorm(v.reshape(B, S, A_HEADS, HEAD_DIM), v_gain)
    vc = v.reshape(B, S // GMLP_CHUNK, GMLP_CHUNK, A_HEADS, HEAD_DIM)
    s = jnp.einsum('hpq,bnqhe->bnphe', w_s, vc) + b_s.T[None, None, :, :, None]
    return (u.reshape(B, S, A_HEADS, HEAD_DIM) * s.reshape(B, S, A_HEADS, HEAD_DIM)).reshape(B, S, A_WIDTH)


def _multiscale_pool(z, w_pool, b_pool, pool_scale):
    B, S, _ = z.shape
    cs = jnp.pad(jnp.cumsum(z.astype(jnp.float32), axis=1), ((0, 0), (1, 0), (0, 0)))
    pos = np.arange(S)
    outs = []
    for g, w in enumerate(POOL_WINDOWS):
        lo = np.clip(pos - w // 2, 0, S)
        hi = np.clip(pos + w // 2, 0, S)
        sl = slice(g * B_GROUP_DIM, (g + 1) * B_GROUP_DIM)
        cnt = (hi - lo).astype(np.float32)[None, :, None]
        outs.append((cs[:, hi, sl] - cs[:, lo, sl]) / cnt)
    pooled = jnp.concatenate(outs, axis=-1).astype(z.dtype) - z
    pooled = pooled.reshape(B, S, B_GROUPS, B_GROUP_DIM)
    y = jnp.einsum('bsge,gef->bsgf', pooled, w_pool) + b_pool
    return y.reshape(B, S, B_WIDTH) * pool_scale


def _dilated_window_attention(q, k, v, rel_table_g, dilation, radius):
    B, S, H, E = q.shape
    L = S // dilation
    C = radius
    n_blk = -(-L // C)
    Lp = n_blk * C

    def to_sub(t):
        t = t.reshape(B, L, dilation, H, E).transpose(0, 2, 1, 3, 4)
        return jnp.pad(t, ((0, 0), (0, 0), (0, Lp - L), (0, 0), (0, 0)))

    def band(t):
        t = jnp.pad(to_sub(t), ((0, 0), (0, 0), (C, C), (0, 0), (0, 0))).reshape(B, dilation, n_blk + 2, C, H, E)
        return jnp.concatenate([t[:, :, :-2], t[:, :, 1:-1], t[:, :, 2:]], axis=3)

    qs = to_sub(q).reshape(B, dilation, n_blk, C, H, E)
    kb, vb = band(k), band(v)
    delta = (np.arange(3 * C)[None, :] - C) - np.arange(C)[:, None]
    bias = rel_table_g[_t5_bucket(delta * dilation)].transpose(2, 0, 1)
    key_sub = np.arange(n_blk)[:, None] * C + np.arange(3 * C)[None, :] - C
    valid = ((np.abs(delta) <= radius)[None]
             & ((key_sub >= 0) & (key_sub < L))[:, None, :])
    logits = jnp.einsum('bdnqhe,bdnkhe->bdnhqk', qs, kb).astype(jnp.float32) * (E ** -0.5)
    logits = logits + bias[None, None, None].astype(jnp.float32)
    logits = jnp.where(valid[None, None, :, None], logits, NEG_INF)
    lse = jax.nn.logsumexp(logits, axis=-1)
    p = jnp.exp(logits - lse[..., None]).astype(v.dtype)
    out = jnp.einsum('bdnhqk,bdnkhe->bdnqhe', p, vb).reshape(B, dilation, Lp, H, E)[:, :, :L]
    out = out.transpose(0, 2, 1, 3, 4).reshape(B, S, H, E)
    lse = lse.transpose(0, 1, 2, 4, 3).reshape(B, dilation, Lp, H)[:, :, :L]
    lse = lse.transpose(0, 2, 1, 3).reshape(B, S, H)
    return out, lse


def _hybrid_mixer(h, rel_table, w_in, b_in, v_gain, w_s, b_s, w_pool, b_pool, pool_scale, w_out, b_out):
    B, S, _ = h.shape
    z = h @ w_in + b_in
    za = z[..., :2 * A_WIDTH]
    zb = z[..., 2 * A_WIDTH:2 * A_WIDTH + B_WIDTH]
    zc = z[..., 2 * A_WIDTH + B_WIDTH:]
    ya = _spatial_gating(za, v_gain, w_s, b_s)
    yb = _multiscale_pool(zb, w_pool, b_pool, pool_scale)
    q, k, v = [t.reshape(B, S, C_GROUPS, C_HEADS_PER_GROUP, HEAD_DIM) for t in jnp.split(zc, 3, axis=-1)]
    outs, lses = [], []
    for g, (window, dil) in enumerate(DILATED_CONFIGS):
        o, l = _dilated_window_attention(q[:, :, g], k[:, :, g], v[:, :, g],
                                         rel_table[:, g * C_HEADS_PER_GROUP:(g + 1) * C_HEADS_PER_GROUP],
                                         dil, window // (2 * dil))
        outs.append(o)
        lses.append(l)
    weights = jax.nn.softmax(jnp.stack(lses, axis=0), axis=0)
    yc = jnp.einsum('gbsh,gbshe->bshe', weights, jnp.stack(outs, axis=0).astype(jnp.float32))
    yc = yc.astype(h.dtype).reshape(B, S, C_OUT_WIDTH)
    y = jnp.concatenate([ya, yb, yc], axis=-1)
    return y @ w_out + b_out


def _memory_cross_attention(h, mem_n, w_q, w_kv, w_o, b_o):
    B, S, _ = h.shape
    M = mem_n.shape[1]
    q = (h @ w_q).reshape(B, S, X_HEADS, X_HEAD_DIM)
    k, v = [t.reshape(B, M, X_HEADS, X_HEAD_DIM) for t in jnp.split(mem_n @ w_kv, 2, axis=-1)]
    logits = jnp.einsum('bshe,bmhe->bhsm', q, k).astype(jnp.float32) * (X_HEAD_DIM ** -0.5)
    p = jax.nn.softmax(logits, axis=-1).astype(v.dtype)
    o = jnp.einsum('bhsm,bmhe->bshe', p, v).reshape(B, S, D_MODEL)
    return o @ w_o + b_o


def _conv_ffn(h, w_up, b_up, conv_w, conv_b, w_down, b_down):
    S = h.shape[1]
    u = h @ w_up + b_up
    up = jnp.pad(u, ((0, 0), (1, 1), (0, 0)))
    u = conv_w[0] * up[:, :S] + conv_w[1] * up[:, 1:S + 1] + conv_w[2] * up[:, 2:] + conv_b
    gate, val = jnp.split(u, 2, axis=-1)
    return (jax.nn.silu(gate) * val) @ w_down + b_down


def setup_inputs(seed: int = 0) -> dict:
    key = jax.random.key(seed)
    ks = iter(jax.random.split(key, 40))
    f32 = jnp.float32

    def nrm(shape, scale):
        return jax.random.normal(next(ks), shape, f32) * scale

    L = DEPTH
    return {
        "x": nrm((BATCH, SEQ, D_MODEL), 1.0),
        "mem": nrm((BATCH, MEM_LEN, D_MODEL), 1.0),
        "rel_table": nrm((N_BUCKETS, C_HEADS), 0.5),
        "mem_norm_g": 1.0 + nrm((D_MODEL,), 0.02),
        "norm_mix_g": 1.0 + nrm((L, D_MODEL), 0.02),
        "w_in": nrm((L, D_MODEL, IN_WIDTH), D_MODEL ** -0.5),
        "b_in": nrm((L, IN_WIDTH), 0.02),
        "gmlp_v_g": 1.0 + nrm((L, A_HEADS, HEAD_DIM), 0.02),
        "gmlp_w_s": nrm((L, A_HEADS, GMLP_CHUNK, GMLP_CHUNK), GMLP_CHUNK ** -0.5),
        "gmlp_b_s": 1.0 + nrm((L, A_HEADS, GMLP_CHUNK), 0.02),
        "pool_w": nrm((L, B_GROUPS, B_GROUP_DIM, B_GROUP_DIM), B_GROUP_DIM ** -0.5),
        "pool_b": nrm((L, B_GROUPS, B_GROUP_DIM), 0.02),
        "pool_scale": 1.0 + nrm((L, B_WIDTH), 0.1),
        "w_out": nrm((L, OUT_WIDTH, D_MODEL), OUT_WIDTH ** -0.5),
        "b_out": nrm((L, D_MODEL), 0.02),
        "norm_mem_g": 1.0 + nrm((L, D_MODEL), 0.02),
        "xattn_w_q": nrm((L, D_MODEL, D_MODEL), D_MODEL ** -0.5),
        "xattn_w_kv": nrm((L, D_MODEL, 2 * D_MODEL), D_MODEL ** -0.5),
        "xattn_w_o": nrm((L, D_MODEL, D_MODEL), D_MODEL ** -0.5),
        "xattn_b_o": nrm((L, D_MODEL), 0.02),
        "norm_ffn_g": 1.0 + nrm((L, D_MODEL), 0.02),
        "ffn_w_up": nrm((L, D_MODEL, 2 * D_FF), D_MODEL ** -0.5),
        "ffn_b_up": nrm((L, 2 * D_FF), 0.02),
        "ffn_conv_w": jnp.array([0.25, 0.5, 0.25], f32)[None, :, None] + nrm((L, CONV_WIDTH, 2 * D_FF), 0.1),
        "ffn_conv_b": nrm((L, 2 * D_FF), 0.02),
        "ffn_w_down": nrm((L, D_FF, D_MODEL), D_FF ** -0.5),
        "ffn_b_down": nrm((L, D_MODEL), 0.02),
        "final_norm_g": 1.0 + nrm((D_MODEL,), 0.02),
    }


def reference(x, mem, rel_table, mem_norm_g, norm_mix_g, w_in, b_in, gmlp_v_g, gmlp_w_s, gmlp_b_s,
              pool_w, pool_b, pool_scale, w_out, b_out, norm_mem_g, xattn_w_q, xattn_w_kv, xattn_w_o,
              xattn_b_o, norm_ffn_g, ffn_w_up, ffn_b_up, ffn_conv_w, ffn_conv_b, ffn_w_down, ffn_b_down,
              final_norm_g):
    mem_n = _rmsnorm(mem, mem_norm_g)
    for l in range(DEPTH):
        h = _rmsnorm(x, norm_mix_g[l])
        x = x + _hybrid_mixer(h, rel_table, w_in[l], b_in[l], gmlp_v_g[l], gmlp_w_s[l], gmlp_b_s[l],
                              pool_w[l], pool_b[l], pool_scale[l], w_out[l], b_out[l])
        h = _rmsnorm(x, norm_mem_g[l])
        x = x + _memory_cross_attention(h, mem_n, xattn_w_q[l], xattn_w_kv[l], xattn_w_o[l], xattn_b_o[l])
        h = _rmsnorm(x, norm_ffn_g[l])
        x = x + _conv_ffn(h, ffn_w_up[l], ffn_b_up[l], ffn_conv_w[l], ffn_conv_b[l], ffn_w_down[l], ffn_b_down[l])
    return _rmsnorm(x, final_norm_g)
```

```python
import functools
import math

import numpy as np
import jax
import jax.numpy as jnp
from jax import lax
from jax.experimental import pallas as pl
from jax.experimental.pallas import tpu as pltpu

F32 = jnp.float32
BF16 = jnp.bfloat16

HEAD_DIM = 64
A_HEADS = 6
A_WIDTH = A_HEADS * HEAD_DIM
GMLP_CHUNK = 128
POOL_WINDOWS = (2, 4, 8, 16)
B_GROUP_DIM = 64
B_WIDTH = len(POOL_WINDOWS) * B_GROUP_DIM
DILATED_CONFIGS = ((128, 1), (512, 4), (2048, 16))
C_GROUPS = len(DILATED_CONFIGS)
C_GROUP_WIDTH = 2 * HEAD_DIM
C_WIDTH = C_GROUPS * C_GROUP_WIDTH
RADIUS = 64
N_BUCKETS = 32
MAX_DISTANCE = 1024
X_HEADS = 4
EPS = 1e-6
NEG_INF = -1e30

LANES = 128
POOL_HALO = 8
FFN_HALO = 8
VMEM_LIMIT = 56 * 1024 * 1024


def _cparams(n_axes):
    return pltpu.CompilerParams(dimension_semantics=("parallel",) * n_axes,
                                vmem_limit_bytes=VMEM_LIMIT)


def _rms(x, g):
    ms = jnp.mean(x * x, axis=-1, keepdims=True)
    return x * lax.rsqrt(ms + EPS) * g


def _gelu_tanh(x):
    return x * (0.5 * (1.0 + jnp.tanh(math.sqrt(2.0 / math.pi) * (x + 0.044715 * (x * x * x)))))


def _dot(a, b):
    return jnp.dot(a, b, preferred_element_type=F32)


def _dot_nt(a, b):
    return lax.dot_general(a, b, (((1,), (1,)), ((), ())), preferred_element_type=F32)


def _full(shape):
    return pl.BlockSpec(shape, lambda *_: (0,) * len(shape))


def _mix_in_kernel(x_ref, g_ref, w_ref, b_ref, vg_ref, ws_ref, bs_ref, ones_ref,
                   ya_ref, zb_ref, q_ref, k_ref, v_ref, *, tm):
    hb = _rms(x_ref[...], g_ref[...]).astype(BF16)

    def seg(a, b):
        return _dot(hb, w_ref[:, a:b]) + b_ref[:, a:b]

    u = _gelu_tanh(seg(0, A_WIDTH))
    v = _gelu_tanh(seg(A_WIDTH, 2 * A_WIDTH))
    vsq = v * v
    hi = vsq.astype(BF16)
    lo = (vsq - hi.astype(F32)).astype(BF16)
    ssum = _dot(hi, ones_ref[...]) + _dot(lo, ones_ref[...])
    vn = v * lax.rsqrt(ssum * (1.0 / HEAD_DIM) + EPS) * vg_ref[...]

    lane = lax.broadcasted_iota(jnp.int32, (GMLP_CHUNK, LANES), 1)
    first_head = lane < HEAD_DIM
    for c in range(tm // GMLP_CHUNK):
        rows = slice(c * GMLP_CHUNK, (c + 1) * GMLP_CHUNK)
        for j in range(A_WIDTH // LANES):
            cols = slice(j * LANES, (j + 1) * LANES)
            vp = vn[rows, cols]
            rhs = jnp.concatenate([jnp.where(first_head, vp, 0.0).astype(BF16),
                                   jnp.where(first_head, 0.0, vp).astype(BF16)], axis=0)
            s = _dot(ws_ref[j], rhs) + bs_ref[:, cols]
            ya_ref[rows, cols] = (u[rows, cols] * s).astype(BF16)

    off = 2 * A_WIDTH
    zb_ref[...] = seg(off, off + B_WIDTH)
    off += B_WIDTH
    qf = seg(off, off + C_WIDTH) * (HEAD_DIM ** -0.5)
    kf = seg(off + C_WIDTH, off + 2 * C_WIDTH)
    vf = seg(off + 2 * C_WIDTH, off + 3 * C_WIDTH)
    for g in range(C_GROUPS):
        cols = slice(g * C_GROUP_WIDTH, (g + 1) * C_GROUP_WIDTH)
        q_ref[g] = qf[:, cols].astype(BF16)
        k_ref[g] = kf[:, cols].astype(BF16)
        v_ref[g] = vf[:, cols].astype(BF16)


def _mix_in(x, g, w_in, b_in, v_gain, ws_cat, bs_full, ones_bd, *, tm):
    t, d = x.shape
    in_width = w_in.shape[1]
    qkv_shape = jax.ShapeDtypeStruct((C_GROUPS, t, C_GROUP_WIDTH), BF16)
    qkv_spec = pl.BlockSpec((C_GROUPS, tm, C_GROUP_WIDTH), lambda i: (0, i, 0))
    return pl.pallas_call(
        functools.partial(_mix_in_kernel, tm=tm),
        grid=(t // tm,),
        in_specs=[pl.BlockSpec((tm, d), lambda i: (i, 0)),
                  _full((1, d)), _full((d, in_width)), _full((1, in_width)), _full((1, A_WIDTH)),
                  _full(ws_cat.shape), _full(bs_full.shape), _full(ones_bd.shape)],
        out_specs=[pl.BlockSpec((tm, A_WIDTH), lambda i: (i, 0)),
                   pl.BlockSpec((tm, B_WIDTH), lambda i: (i, 0)),
                   qkv_spec, qkv_spec, qkv_spec],
        out_shape=[jax.ShapeDtypeStruct((t, A_WIDTH), BF16),
                   jax.ShapeDtypeStruct((t, B_WIDTH), F32),
                   qkv_shape, qkv_shape, qkv_shape],
        compiler_params=_cparams(1),
        name="mix_in",
    )(x, g, w_in, b_in, v_gain, ws_cat, bs_full, ones_bd)


def _attn_kernel(q_ref, kp_ref, k_ref, kn_ref, vp_ref, v_ref, vn_ref, bias_ref,
                 o_ref, lse_ref, kx_ref, vx_ref, *, lq, sub_len):
    i = pl.program_id(2)
    kx_ref[0:RADIUS] = kp_ref[...]
    kx_ref[RADIUS:RADIUS + lq] = k_ref[...]
    kx_ref[RADIUS + lq:] = kn_ref[...]
    vx_ref[0:RADIUS] = vp_ref[...]
    vx_ref[RADIUS:RADIUS + lq] = v_ref[...]
    vx_ref[RADIUS + lq:] = vn_ref[...]

    qc = 2 * RADIUS
    kc = qc + 2 * RADIUS
    lane = lax.broadcasted_iota(jnp.int32, (qc, LANES), 1)
    first_head = lane < HEAD_DIM
    col = lax.broadcasted_iota(jnp.int32, (qc, kc), 1)
    for c in range(lq // qc):
        rows = slice(c * qc, (c + 1) * qc)
        q = q_ref[rows, :]
        keys = kx_ref[c * qc:c * qc + kc, :]
        vals = vx_ref[c * qc:c * qc + kc, :]
        kidx = col + (i * lq + c * qc - RADIUS)
        valid = (kidx >= 0) & (kidx < sub_len)
        o_heads, lse_heads = [], []
        for h in range(2):
            qh = jnp.where(first_head if h == 0 else ~first_head, q, jnp.zeros_like(q))
            s = _dot_nt(qh, keys) + bias_ref[h]
            s = jnp.where(valid, s, NEG_INF)
            m = jnp.max(s, axis=-1, keepdims=True)
            p = jnp.exp(s - m)
            l = jnp.sum(p, axis=-1, keepdims=True)
            o_heads.append(_dot(p.astype(BF16), vals) / l)
            lse_heads.append(jnp.broadcast_to(m + jnp.log(l), (qc, LANES)))
        o_ref[rows, :] = jnp.where(first_head, o_heads[0], o_heads[1])
        lse_ref[rows, :] = jnp.where(first_head, lse_heads[0], lse_heads[1])


def _attn(q, k, v, bias, *, group, dilation, batch, seq, lq):
    sub_len = seq // dilation
    lq = min(lq, sub_len)
    view = (C_GROUPS, batch, sub_len, dilation * C_GROUP_WIDTH)
    q, k, v = q.reshape(view), k.reshape(view), v.reshape(view)
    halo_per_blk = lq // RADIUS
    n_halo = sub_len // RADIUS

    main = pl.BlockSpec((None, None, lq, C_GROUP_WIDTH), lambda b, r, i: (group, b, i, r))
    prev = pl.BlockSpec((None, None, RADIUS, C_GROUP_WIDTH),
                        lambda b, r, i: (group, b, jnp.maximum(i * halo_per_blk - 1, 0), r))
    nxt = pl.BlockSpec((None, None, RADIUS, C_GROUP_WIDTH),
                       lambda b, r, i: (group, b, jnp.minimum((i + 1) * halo_per_blk, n_halo - 1), r))
    out_spec = pl.BlockSpec((None, lq, C_GROUP_WIDTH), lambda b, r, i: (b, i, r))
    out_shape = jax.ShapeDtypeStruct((batch, sub_len, dilation * C_GROUP_WIDTH), F32)
    o, lse = pl.pallas_call(
        functools.partial(_attn_kernel, lq=lq, sub_len=sub_len),
        grid=(batch, dilation, sub_len // lq),
        in_specs=[main, prev, main, nxt, prev, main, nxt, _full(bias.shape)],
        out_specs=[out_spec, out_spec],
        out_shape=[out_shape, out_shape],
        scratch_shapes=[pltpu.VMEM((lq + 2 * RADIUS, C_GROUP_WIDTH), BF16),
                        pltpu.VMEM((lq + 2 * RADIUS, C_GROUP_WIDTH), BF16)],
        compiler_params=_cparams(3),
        name=f"dilated_attn_d{dilation}",
    )(q, k, k, k, v, v, v, bias)
    t = batch * seq
    return o.reshape(t, C_GROUP_WIDTH), lse.reshape(t, C_GROUP_WIDTH)


def _t5_bucket(rel):
    nb = N_BUCKETS // 2
    ret = (rel > 0).astype(np.int32) * nb
    n = np.abs(rel)
    max_exact = nb // 2
    large = max_exact + (np.log(np.maximum(n, 1) / max_exact)
                         / math.log(MAX_DISTANCE / max_exact) * (nb - max_exact)).astype(np.int32)
    large = np.minimum(large, nb - 1)
    return ret + np.where(n < max_exact, n, large)


def _attn_bias(rel_table, group, dilation):
    qc, kc = 2 * RADIUS, 4 * RADIUS
    delta = (np.arange(kc)[None, :] - RADIUS) - np.arange(qc)[:, None]
    bucket = _t5_bucket(delta * dilation)
    bias = rel_table[bucket][..., 2 * group:2 * group + 2]
    bias = jnp.where((np.abs(delta) <= RADIUS)[..., None], bias, NEG_INF)
    return jnp.transpose(bias, (2, 0, 1)).astype(F32)


def _mix_out_kernel(x_ref, ya_ref, zp_ref, zb_ref, zn_ref, o0_ref, o1_ref, o2_ref, l0_ref, l1_ref, l2_ref,
                    wp_ref, bp_ref, ps_ref, wo_ref, bo_ref, out_ref, zx_ref, *, tm, seq):
    tiles = seq // tm
    t = pl.program_id(0) % tiles
    hl = POOL_HALO
    zx_ref[0:hl] = jnp.where(t > 0, zp_ref[...], 0.0)
    zx_ref[hl:hl + tm] = zb_ref[...]
    zx_ref[hl + tm:] = jnp.where(t < tiles - 1, zn_ref[...], 0.0)

    pos = t * tm + lax.broadcasted_iota(jnp.int32, (tm, LANES), 0)
    lane = lax.broadcasted_iota(jnp.int32, (tm, LANES), 1)
    first_group = lane < B_GROUP_DIM
    pooled = []
    for cb in range(B_WIDTH // LANES):
        cols = slice(cb * LANES, (cb + 1) * LANES)
        half_a, half_b = POOL_WINDOWS[2 * cb] // 2, POOL_WINDOWS[2 * cb + 1] // 2

        def shifted(j):
            return zx_ref[hl + j:hl + j + tm, cols]

        acc = shifted(-1) + shifted(0)
        sums = {1: acc}
        for half in (2, 4, 8):
            if half > half_b:
                break
            for j in range(half // 2, half):
                acc = acc + shifted(-j - 1) + shifted(j)
            sums[half] = acc
        half = jnp.where(first_group, half_a, half_b)
        cnt = (jnp.minimum(pos + half, seq) - jnp.maximum(pos - half, 0)).astype(F32)
        wsum = jnp.where(first_group, sums[half_a], sums[half_b])
        pooled.append(wsum / cnt - zb_ref[:, cols])
    pooled = jnp.concatenate(pooled, axis=-1).astype(BF16)
    yb = (_dot(pooled, wp_ref[...]) + bp_ref[...]) * ps_ref[...]

    l0, l1, l2 = l0_ref[...], l1_ref[...], l2_ref[...]
    m = jnp.maximum(l0, jnp.maximum(l1, l2))
    e0, e1, e2 = jnp.exp(l0 - m), jnp.exp(l1 - m), jnp.exp(l2 - m)
    yc = (e0 * o0_ref[...] + e1 * o1_ref[...] + e2 * o2_ref[...]) / (e0 + e1 + e2)

    y = (_dot(ya_ref[...], wo_ref[0:A_WIDTH, :])
         + _dot(yb.astype(BF16), wo_ref[A_WIDTH:A_WIDTH + B_WIDTH, :])
         + _dot(yc.astype(BF16), wo_ref[A_WIDTH + B_WIDTH:, :]))
    out_ref[...] = x_ref[...] + y + bo_ref[...]


def _mix_out(x, ya, zb, o, lse, wp_bd, bp, ps, w_out, b_out, *, tm, seq):
    t, d = x.shape
    hl = POOL_HALO
    blk_per_tile = tm // hl
    n_hl = t // hl
    row = lambda w: pl.BlockSpec((tm, w), lambda i: (i, 0))
    cw = pl.BlockSpec((tm, C_GROUP_WIDTH), lambda i: (i, 0))
    return pl.pallas_call(
        functools.partial(_mix_out_kernel, tm=tm, seq=seq),
        grid=(t // tm,),
        in_specs=[row(d), row(A_WIDTH),
                  pl.BlockSpec((hl, B_WIDTH), lambda i: (jnp.maximum(i * blk_per_tile - 1, 0), 0)),
                  row(B_WIDTH),
                  pl.BlockSpec((hl, B_WIDTH), lambda i: (jnp.minimum((i + 1) * blk_per_tile, n_hl - 1), 0)),
                  cw, cw, cw, cw, cw, cw,
                  _full(wp_bd.shape), _full((1, B_WIDTH)), _full((1, B_WIDTH)),
                  _full(w_out.shape), _full((1, d))],
        out_specs=row(d),
        out_shape=jax.ShapeDtypeStruct((t, d), F32),
        scratch_shapes=[pltpu.VMEM((tm + 2 * hl, B_WIDTH), F32)],
        compiler_params=_cparams(1),
        name="mix_out",
    )(x, ya, zb, zb, zb, o[0], o[1], o[2], lse[0], lse[1], lse[2], wp_bd, bp, ps, w_out, b_out)


def _kv_kernel(mem_ref, g_ref, w_ref, kv_ref):
    mem_n = _rms(mem_ref[...], g_ref[...]).astype(BF16)
    kv_ref[...] = _dot(mem_n, w_ref[...]).astype(BF16)


def _kv_proj(mem, g, w_kv):
    depth, d, d2 = w_kv.shape
    rows = mem.shape[0]
    return pl.pallas_call(
        _kv_kernel,
        grid=(depth,),
        in_specs=[_full((rows, d)), _full((1, d)), pl.BlockSpec((None, d, d2), lambda l: (l, 0, 0))],
        out_specs=pl.BlockSpec((None, rows, d2), lambda l: (l, 0, 0)),
        out_shape=jax.ShapeDtypeStruct((depth, rows, d2), BF16),
        compiler_params=_cparams(1),
        name="mem_kv",
    )(mem, g, w_kv)


def _xattn_kernel(x_ref, g_ref, wq_ref, kv_ref, wo_ref, bo_ref, out_ref, o_ref, *, d):
    x = x_ref[...]
    hb = _rms(x, g_ref[...]).astype(BF16)
    hd = d // X_HEADS
    q = (_dot(hb, wq_ref[...]) * (hd ** -0.5)).astype(BF16)
    for h in range(X_HEADS):
        cols = slice(h * hd, (h + 1) * hd)
        s = _dot_nt(q[:, cols], kv_ref[:, cols])
        m = jnp.max(s, axis=-1, keepdims=True)
        p = jnp.exp(s - m)
        l = jnp.sum(p, axis=-1, keepdims=True)
        oh = _dot(p.astype(BF16), kv_ref[:, d + h * hd:d + (h + 1) * hd]) / l
        o_ref[:, cols] = oh.astype(BF16)
    out_ref[...] = x + _dot(o_ref[...], wo_ref[...]) + bo_ref[...]


def _xattn(x, g, w_q, kv, layer, w_o, b_o, *, tm, seq, mem_len):
    t, d = x.shape
    tiles = seq // tm
    kv = kv.reshape(kv.shape[0], t // seq, mem_len, 2 * d)
    row = pl.BlockSpec((tm, d), lambda i: (i, 0))
    return pl.pallas_call(
        functools.partial(_xattn_kernel, d=d),
        grid=(t // tm,),
        in_specs=[row, _full((1, d)), _full((d, d)),
                  pl.BlockSpec((None, None, mem_len, 2 * d), lambda i: (layer, i // tiles, 0, 0)),
                  _full((d, d)), _full((1, d))],
        out_specs=row,
        out_shape=jax.ShapeDtypeStruct((t, d), F32),
        scratch_shapes=[pltpu.VMEM((tm, d), BF16)],
        compiler_params=_cparams(1),
        name="mem_xattn",
    )(x, g, w_q, kv, w_o, b_o)


def _ffn_kernel(xp_ref, x_ref, xn_ref, g_ref, wu_ref, bu_ref, cw_ref, cb_ref, wd_ref, bd_ref, fg_ref,
                out_ref, ug_ref, uv_ref, acc_ref, *, tm, seq, d_ff, cn, final_norm):
    tiles = seq // tm
    t = pl.program_id(0) % tiles
    hl = FFN_HALO
    g = g_ref[...]
    x = x_ref[...]
    hb = jnp.concatenate([_rms(xp_ref[...], g), _rms(x, g), _rms(xn_ref[...], g)], axis=0).astype(BF16)

    def conv(u_ref, cols):
        @pl.when(t == 0)
        def _():
            u_ref[hl - 1:hl, :] = jnp.zeros((1, cn), F32)

        @pl.when(t == tiles - 1)
        def _():
            u_ref[hl + tm:hl + tm + 1, :] = jnp.zeros((1, cn), F32)

        return (cw_ref[0:1, cols] * u_ref[hl - 1:hl - 1 + tm, :]
                + cw_ref[1:2, cols] * u_ref[hl:hl + tm, :]
                + cw_ref[2:3, cols] * u_ref[hl + 1:hl + 1 + tm, :]
                + cb_ref[:, cols])

    for j in range(d_ff // cn):
        gcols = slice(j * cn, (j + 1) * cn)
        vcols = slice(d_ff + j * cn, d_ff + (j + 1) * cn)
        ug_ref[...] = _dot(hb, wu_ref[:, gcols]) + bu_ref[:, gcols]
        uv_ref[...] = _dot(hb, wu_ref[:, vcols]) + bu_ref[:, vcols]
        gate = conv(ug_ref, gcols)
        val = conv(uv_ref, vcols)
        act = (gate * jax.nn.sigmoid(gate) * val).astype(BF16)
        part = _dot(act, wd_ref[gcols, :])
        if j == 0:
            acc_ref[...] = part
        else:
            acc_ref[...] += part
    y = x + acc_ref[...] + bd_ref[...]
    if final_norm:
        y = _rms(y, fg_ref[...])
    out_ref[...] = y


def _ffn(x, g, w_up, b_up, conv_w, conv_b, w_down, b_down, final_g, *, tm, seq, cn, final_norm):
    t, d = x.shape
    d_ff = w_down.shape[0]
    hl = FFN_HALO
    blk_per_tile = tm // hl
    n_hl = t // hl
    row = pl.BlockSpec((tm, d), lambda i: (i, 0))
    return pl.pallas_call(
        functools.partial(_ffn_kernel, tm=tm, seq=seq, d_ff=d_ff, cn=cn, final_norm=final_norm),
        grid=(t // tm,),
        in_specs=[pl.BlockSpec((hl, d), lambda i: (jnp.maximum(i * blk_per_tile - 1, 0), 0)),
                  row,
                  pl.BlockSpec((hl, d), lambda i: (jnp.minimum((i + 1) * blk_per_tile, n_hl - 1), 0)),
                  _full((1, d)), _full(w_up.shape), _full((1, 2 * d_ff)), _full(conv_w.shape),
                  _full((1, 2 * d_ff)), _full(w_down.shape), _full((1, d)), _full((1, d))],
        out_specs=row,
        out_shape=jax.ShapeDtypeStruct((t, d), F32),
        scratch_shapes=[pltpu.VMEM((tm + 2 * hl, cn), F32), pltpu.VMEM((tm + 2 * hl, cn), F32),
                        pltpu.VMEM((tm, d), F32)],
        compiler_params=_cparams(1),
        name="conv_ffn",
    )(x, x, x, g, w_up, b_up, conv_w, conv_b, w_down, b_down, final_g)


def _block_diag(blocks):
    n, a, b = blocks.shape
    eye = jnp.eye(n, dtype=blocks.dtype)
    return (eye[:, None, :, None] * blocks[:, :, None, :]).reshape(n * a, n * b)


def kernel(x, mem, rel_table, mem_norm_g, norm_mix_g, w_in, b_in, gmlp_v_g, gmlp_w_s, gmlp_b_s, pool_w, pool_b, pool_scale, w_out, b_out, norm_mem_g, xattn_w_q, xattn_w_kv, xattn_w_o, xattn_b_o, norm_ffn_g, ffn_w_up, ffn_b_up, ffn_conv_w, ffn_conv_b, ffn_w_down, ffn_b_down, final_norm_g):
    batch, seq, d = x.shape
    mem_len = mem.shape[1]
    depth = w_in.shape[0]
    t = batch * seq
    assert all(w // (2 * dil) == RADIUS for w, dil in DILATED_CONFIGS)

    row1 = lambda a: a.reshape(1, -1)
    ones_bd = jnp.asarray(np.kron(np.eye(A_HEADS), np.ones((HEAD_DIM, HEAD_DIM))), BF16)
    biases = [_attn_bias(rel_table, g, dil) for g, (_, dil) in enumerate(DILATED_CONFIGS)]
    kv = _kv_proj(mem.reshape(batch * mem_len, d), row1(mem_norm_g), xattn_w_kv.astype(BF16))

    xf = x.reshape(t, d)
    for l in range(depth):
        ws_cat = (gmlp_w_s[l].reshape(A_HEADS // 2, 2, GMLP_CHUNK, GMLP_CHUNK)
                  .transpose(0, 2, 1, 3).reshape(A_HEADS // 2, GMLP_CHUNK, 2 * GMLP_CHUNK).astype(BF16))
        bs_full = jnp.repeat(gmlp_b_s[l].T, HEAD_DIM, axis=1)
        ya, zb, q, k, v = _mix_in(xf, row1(norm_mix_g[l]), w_in[l].astype(BF16), row1(b_in[l]),
                                  row1(gmlp_v_g[l]), ws_cat, bs_full, ones_bd, tm=256)
        o, lse = [], []
        for g, (_, dil) in enumerate(DILATED_CONFIGS):
            og, lg = _attn(q, k, v, biases[g], group=g, dilation=dil, batch=batch, seq=seq, lq=512)
            o.append(og)
            lse.append(lg)
        xf = _mix_out(xf, ya, zb, o, lse, _block_diag(pool_w[l]).astype(BF16), row1(pool_b[l]),
                      row1(pool_scale[l]), w_out[l].astype(BF16), row1(b_out[l]), tm=512, seq=seq)
        xf = _xattn(xf, row1(norm_mem_g[l]), xattn_w_q[l].astype(BF16), kv, l,
                    xattn_w_o[l].astype(BF16), row1(xattn_b_o[l]), tm=512, seq=seq, mem_len=mem_len)
        xf = _ffn(xf, row1(norm_ffn_g[l]), ffn_w_up[l].astype(BF16), row1(ffn_b_up[l]), ffn_conv_w[l],
                  row1(ffn_conv_b[l]), ffn_w_down[l].astype(BF16), row1(ffn_b_down[l]), row1(final_norm_g),
                  tm=256, seq=seq, cn=256, final_norm=(l == depth - 1))
    return xf.reshape(batch, seq, d)
```

```python
import functools
import math

import numpy as np
import jax
import jax.numpy as jnp
from jax import lax
from jax.experimental import pallas as pl
from jax.experimental.pallas import tpu as pltpu

F32 = jnp.float32
BF16 = jnp.bfloat16

HEAD_DIM = 64
A_HEADS = 6
A_WIDTH = A_HEADS * HEAD_DIM
GMLP_CHUNK = 128
POOL_WINDOWS = (2, 4, 8, 16)
B_GROUP_DIM = 64
B_WIDTH = len(POOL_WINDOWS) * B_GROUP_DIM
DILATED_CONFIGS = ((128, 1), (512, 4), (2048, 16))
C_GROUPS = len(DILATED_CONFIGS)
C_GROUP_WIDTH = 2 * HEAD_DIM
C_WIDTH = C_GROUPS * C_GROUP_WIDTH
RADIUS = 64
N_BUCKETS = 32
MAX_DISTANCE = 1024
X_HEADS = 4
EPS = 1e-6
NEG_INF = -1e30

LANES = 128
POOL_HALO = 8
FFN_HALO = 8
VMEM_LIMIT = 56 * 1024 * 1024


def _cparams(n_axes):
    return pltpu.CompilerParams(dimension_semantics=("parallel",) * n_axes,
                                vmem_limit_bytes=VMEM_LIMIT)


def _rms(x, g):
    ms = jnp.mean(x * x, axis=-1, keepdims=True)
    return x * lax.rsqrt(ms + EPS) * g


def _gelu_tanh(x):
    return x * (0.5 * (1.0 + jnp.tanh(math.sqrt(2.0 / math.pi) * (x + 0.044715 * (x * x * x)))))


def _dot(a, b):
    return jnp.dot(a, b, preferred_element_type=F32)


def _dot_nt(a, b):
    return lax.dot_general(a, b, (((1,), (1,)), ((), ())), preferred_element_type=F32)


def _full(shape):
    return pl.BlockSpec(shape, lambda *_: (0,) * len(shape))


def _mix_in_kernel(x_ref, g_ref, w_ref, b_ref, vg_ref, ws_ref, bs_ref, ones_ref,
                   ya_ref, zb_ref, g0_ref, g1_ref, g2_ref, qkv_ref, *, tm):
    hb = _rms(x_ref[...], g_ref[...]).astype(BF16)

    def seg(a, b):
        return _dot(hb, w_ref[:, a:b]) + b_ref[:, a:b]

    u = _gelu_tanh(seg(0, A_WIDTH))
    v = _gelu_tanh(seg(A_WIDTH, 2 * A_WIDTH))
    vsq = v * v
    hi = vsq.astype(BF16)
    lo = (vsq - hi.astype(F32)).astype(BF16)
    ssum = _dot(hi, ones_ref[...]) + _dot(lo, ones_ref[...])
    vn = v * lax.rsqrt(ssum * (1.0 / HEAD_DIM) + EPS) * vg_ref[...]

    lane = lax.broadcasted_iota(jnp.int32, (GMLP_CHUNK, LANES), 1)
    first_head = lane < HEAD_DIM
    for c in range(tm // GMLP_CHUNK):
        rows = slice(c * GMLP_CHUNK, (c + 1) * GMLP_CHUNK)
        for j in range(A_WIDTH // LANES):
            cols = slice(j * LANES, (j + 1) * LANES)
            vp = vn[rows, cols]
            rhs = jnp.concatenate([jnp.where(first_head, vp, 0.0).astype(BF16),
                                   jnp.where(first_head, 0.0, vp).astype(BF16)], axis=0)
            s = _dot(ws_ref[j], rhs) + bs_ref[:, cols]
            ya_ref[rows, cols] = (u[rows, cols] * s).astype(BF16)

    off = 2 * A_WIDTH
    zb_ref[...] = seg(off, off + B_WIDTH)
    off += B_WIDTH
    parts = (seg(off, off + C_WIDTH) * (HEAD_DIM ** -0.5),
             seg(off + C_WIDTH, off + 2 * C_WIDTH),
             seg(off + 2 * C_WIDTH, off + 3 * C_WIDTH))
    for g, (out_ref, (_, dil)) in enumerate(zip((g0_ref, g1_ref, g2_ref), DILATED_CONFIGS)):
        for part in range(3):
            tile = parts[part][:, g * C_GROUP_WIDTH:(g + 1) * C_GROUP_WIDTH]
            if dil == 1:
                out_ref[part] = tile.astype(BF16)
                continue
            qkv_ref[part, g] = tile
            for r in range(dil):
                piece = qkv_ref[part, g, pl.ds(r, tm // dil, stride=dil), :]
                out_ref[part, :, r * C_GROUP_WIDTH:(r + 1) * C_GROUP_WIDTH] = piece.astype(BF16)


def _mix_in(x, g, w_in, b_in, v_gain, ws_cat, bs_full, ones_bd, *, tm):
    t, d = x.shape
    in_width = w_in.shape[1]
    qkv_shapes = [jax.ShapeDtypeStruct((3, t // dil, dil * C_GROUP_WIDTH), BF16) for _, dil in DILATED_CONFIGS]
    qkv_specs = [pl.BlockSpec((3, tm // dil, dil * C_GROUP_WIDTH), lambda i: (0, i, 0))
                 for _, dil in DILATED_CONFIGS]
    return pl.pallas_call(
        functools.partial(_mix_in_kernel, tm=tm),
        grid=(t // tm,),
        in_specs=[pl.BlockSpec((tm, d), lambda i: (i, 0)),
                  _full((1, d)), _full((d, in_width)), _full((1, in_width)), _full((1, A_WIDTH)),
                  _full(ws_cat.shape), _full(bs_full.shape), _full(ones_bd.shape)],
        out_specs=[pl.BlockSpec((tm, A_WIDTH), lambda i: (i, 0)),
                   pl.BlockSpec((tm, B_WIDTH), lambda i: (i, 0))] + qkv_specs,
        out_shape=[jax.ShapeDtypeStruct((t, A_WIDTH), BF16),
                   jax.ShapeDtypeStruct((t, B_WIDTH), F32)] + qkv_shapes,
        scratch_shapes=[pltpu.VMEM((3, C_GROUPS, tm, C_GROUP_WIDTH), F32)],
        compiler_params=_cparams(1),
        name="mix_in",
    )(x, g, w_in, b_in, v_gain, ws_cat, bs_full, ones_bd)


def _attn_kernel(q_ref, kp_ref, k_ref, kn_ref, vp_ref, v_ref, vn_ref, bias_ref,
                 o_ref, lse_ref, kx_ref, vx_ref, *, lq, sub_len):
    i = pl.program_id(2)
    kx_ref[0:RADIUS] = kp_ref[...]
    kx_ref[RADIUS:RADIUS + lq] = k_ref[...]
    kx_ref[RADIUS + lq:] = kn_ref[...]
    vx_ref[0:RADIUS] = vp_ref[...]
    vx_ref[RADIUS:RADIUS + lq] = v_ref[...]
    vx_ref[RADIUS + lq:] = vn_ref[...]

    qc = 2 * RADIUS
    kc = qc + 2 * RADIUS
    lane = lax.broadcasted_iota(jnp.int32, (qc, LANES), 1)
    first_head = lane < HEAD_DIM
    col = lax.broadcasted_iota(jnp.int32, (qc, kc), 1)
    for c in range(lq // qc):
        rows = slice(c * qc, (c + 1) * qc)
        q = q_ref[rows, :]
        keys = kx_ref[c * qc:c * qc + kc, :]
        vals = vx_ref[c * qc:c * qc + kc, :]
        kidx = col + (i * lq + c * qc - RADIUS)
        valid = (kidx >= 0) & (kidx < sub_len)
        o_heads, lse_heads = [], []
        for h in range(2):
            qh = jnp.where(first_head if h == 0 else ~first_head, q, jnp.zeros_like(q))
            s = _dot_nt(qh, keys) + bias_ref[h]
            s = jnp.where(valid, s, NEG_INF)
            m = jnp.max(s, axis=-1, keepdims=True)
            p = jnp.exp(s - m)
            l = jnp.sum(p, axis=-1, keepdims=True)
            o_heads.append(_dot(p.astype(BF16), vals) / l)
            lse_heads.append(jnp.broadcast_to(m + jnp.log(l), (qc, LANES)))
        o_ref[rows, :] = jnp.where(first_head, o_heads[0], o_heads[1])
        lse_ref[rows, :] = jnp.where(first_head, lse_heads[0], lse_heads[1])


def _attn(qkv, bias, *, dilation, batch, seq, lq):
    sub_len = seq // dilation
    lq = min(lq, sub_len)
    qkv = qkv.reshape(3, batch, sub_len, dilation * C_GROUP_WIDTH)
    halo_per_blk = lq // RADIUS
    n_halo = sub_len // RADIUS

    def main(part):
        return pl.BlockSpec((None, None, lq, C_GROUP_WIDTH), lambda b, r, i: (part, b, i, r))

    def prev(part):
        return pl.BlockSpec((None, None, RADIUS, C_GROUP_WIDTH),
                            lambda b, r, i: (part, b, jnp.maximum(i * halo_per_blk - 1, 0), r))

    def nxt(part):
        return pl.BlockSpec((None, None, RADIUS, C_GROUP_WIDTH),
                            lambda b, r, i: (part, b, jnp.minimum((i + 1) * halo_per_blk, n_halo - 1), r))

    out_spec = pl.BlockSpec((None, lq, C_GROUP_WIDTH), lambda b, r, i: (b, i, r))
    out_shape = jax.ShapeDtypeStruct((batch, sub_len, dilation * C_GROUP_WIDTH), F32)
    o, lse = pl.pallas_call(
        functools.partial(_attn_kernel, lq=lq, sub_len=sub_len),
        grid=(batch, dilation, sub_len // lq),
        in_specs=[main(0), prev(1), main(1), nxt(1), prev(2), main(2), nxt(2), _full(bias.shape)],
        out_specs=[out_spec, out_spec],
        out_shape=[out_shape, out_shape],
        scratch_shapes=[pltpu.VMEM((lq + 2 * RADIUS, C_GROUP_WIDTH), BF16),
                        pltpu.VMEM((lq + 2 * RADIUS, C_GROUP_WIDTH), BF16)],
        compiler_params=_cparams(3),
        name=f"dilated_attn_d{dilation}",
    )(qkv, qkv, qkv, qkv, qkv, qkv, qkv, bias)
    rows = batch * sub_len
    return o.reshape(rows, dilation * C_GROUP_WIDTH), lse.reshape(rows, dilation * C_GROUP_WIDTH)


def _t5_bucket(rel):
    nb = N_BUCKETS // 2
    ret = (rel > 0).astype(np.int32) * nb
    n = np.abs(rel)
    max_exact = nb // 2
    large = max_exact + (np.log(np.maximum(n, 1) / max_exact)
                         / math.log(MAX_DISTANCE / max_exact) * (nb - max_exact)).astype(np.int32)
    large = np.minimum(large, nb - 1)
    return ret + np.where(n < max_exact, n, large)


def _attn_bias(rel_table, group, dilation):
    qc, kc = 2 * RADIUS, 4 * RADIUS
    n = qc + kc - 1
    delta = (np.arange(n) + qc - 1) % n - (qc + RADIUS - 1)
    onehot = (_t5_bucket(delta * dilation)[:, None] == np.arange(N_BUCKETS)[None, :]).astype(np.float32)
    per_delta = jnp.dot(onehot, rel_table[:, 2 * group:2 * group + 2], precision=lax.Precision.HIGHEST)
    per_delta = jnp.where((np.abs(delta) <= RADIUS)[:, None], per_delta, NEG_INF).T
    reps = -(-(qc * (n - 1)) // n)
    tiled = jnp.tile(per_delta, (1, reps))[:, :qc * (n - 1)].reshape(2, qc, n - 1)
    return tiled[:, :, :kc]


def _mix_out_kernel(x_ref, ya_ref, zp_ref, zb_ref, zn_ref, o0_ref, o1_ref, o2_ref, l0_ref, l1_ref, l2_ref,
                    wp_ref, bp_ref, ps_ref, wo_ref, bo_ref, out_ref, zx_ref, nat_ref, *, tm, seq):
    tiles = seq // tm
    t = pl.program_id(0) % tiles
    hl = POOL_HALO
    zx_ref[0:hl] = jnp.where(t > 0, zp_ref[...], 0.0)
    zx_ref[hl:hl + tm] = zb_ref[...]
    zx_ref[hl + tm:] = jnp.where(t < tiles - 1, zn_ref[...], 0.0)

    pos = t * tm + lax.broadcasted_iota(jnp.int32, (tm, LANES), 0)
    lane = lax.broadcasted_iota(jnp.int32, (tm, LANES), 1)
    first_group = lane < B_GROUP_DIM
    pooled = []
    for cb in range(B_WIDTH // LANES):
        cols = slice(cb * LANES, (cb + 1) * LANES)
        half_a, half_b = POOL_WINDOWS[2 * cb] // 2, POOL_WINDOWS[2 * cb + 1] // 2

        def shifted(j):
            return zx_ref[hl + j:hl + j + tm, cols]

        acc = shifted(-1) + shifted(0)
        sums = {1: acc}
        for half in (2, 4, 8):
            if half > half_b:
                break
            for j in range(half // 2, half):
                acc = acc + shifted(-j - 1) + shifted(j)
            sums[half] = acc
        half = jnp.where(first_group, half_a, half_b)
        cnt = (jnp.minimum(pos + half, seq) - jnp.maximum(pos - half, 0)).astype(F32)
        wsum = jnp.where(first_group, sums[half_a], sums[half_b])
        pooled.append(wsum / cnt - zb_ref[:, cols])
    pooled = jnp.concatenate(pooled, axis=-1).astype(BF16)
    yb = (_dot(pooled, wp_ref[...]) + bp_ref[...]) * ps_ref[...]

    def natural(src_ref, slot, dil):
        if dil == 1:
            return src_ref[...]
        for r in range(dil):
            nat_ref[slot, pl.ds(r, tm // dil, stride=dil), :] = src_ref[:, r * C_GROUP_WIDTH:(r + 1) * C_GROUP_WIDTH]
        return nat_ref[slot]

    dils = [dil for _, dil in DILATED_CONFIGS]
    o0, o1, o2 = [natural(r, s, dl) for s, (r, dl) in enumerate(zip((o0_ref, o1_ref, o2_ref), dils))]
    l0, l1, l2 = [natural(r, s + 3, dl) for s, (r, dl) in enumerate(zip((l0_ref, l1_ref, l2_ref), dils))]
    m = jnp.maximum(l0, jnp.maximum(l1, l2))
    e0, e1, e2 = jnp.exp(l0 - m), jnp.exp(l1 - m), jnp.exp(l2 - m)
    yc = (e0 * o0 + e1 * o1 + e2 * o2) / (e0 + e1 + e2)

    y = (_dot(ya_ref[...], wo_ref[0:A_WIDTH, :])
         + _dot(yb.astype(BF16), wo_ref[A_WIDTH:A_WIDTH + B_WIDTH, :])
         + _dot(yc.astype(BF16), wo_ref[A_WIDTH + B_WIDTH:, :]))
    out_ref[...] = x_ref[...] + y + bo_ref[...]


def _mix_out(x, ya, zb, o, lse, wp_bd, bp, ps, w_out, b_out, *, tm, seq):
    t, d = x.shape
    hl = POOL_HALO
    blk_per_tile = tm // hl
    n_hl = t // hl
    row = lambda w: pl.BlockSpec((tm, w), lambda i: (i, 0))
    grp = [pl.BlockSpec((tm // dil, dil * C_GROUP_WIDTH), lambda i: (i, 0)) for _, dil in DILATED_CONFIGS]
    return pl.pallas_call(
        functools.partial(_mix_out_kernel, tm=tm, seq=seq),
        grid=(t // tm,),
        in_specs=[row(d), row(A_WIDTH),
                  pl.BlockSpec((hl, B_WIDTH), lambda i: (jnp.maximum(i * blk_per_tile - 1, 0), 0)),
                  row(B_WIDTH),
                  pl.BlockSpec((hl, B_WIDTH), lambda i: (jnp.minimum((i + 1) * blk_per_tile, n_hl - 1), 0)),
                  *grp, *grp,
                  _full(wp_bd.shape), _full((1, B_WIDTH)), _full((1, B_WIDTH)),
                  _full(w_out.shape), _full((1, d))],
        out_specs=row(d),
        out_shape=jax.ShapeDtypeStruct((t, d), F32),
        scratch_shapes=[pltpu.VMEM((tm + 2 * hl, B_WIDTH), F32),
                        pltpu.VMEM((2 * C_GROUPS, tm, C_GROUP_WIDTH), F32)],
        compiler_params=_cparams(1),
        name="mix_out",
    )(x, ya, zb, zb, zb, o[0], o[1], o[2], lse[0], lse[1], lse[2], wp_bd, bp, ps, w_out, b_out)


def _kv_kernel(mem_ref, g_ref, w_ref, kv_ref):
    mem_n = _rms(mem_ref[...], g_ref[...]).astype(BF16)
    kv_ref[...] = _dot(mem_n, w_ref[...]).astype(BF16)


def _kv_proj(mem, g, w_kv):
    depth, d, d2 = w_kv.shape
    rows = mem.shape[0]
    return pl.pallas_call(
        _kv_kernel,
        grid=(depth,),
        in_specs=[_full((rows, d)), _full((1, d)), pl.BlockSpec((None, d, d2), lambda l: (l, 0, 0))],
        out_specs=pl.BlockSpec((None, rows, d2), lambda l: (l, 0, 0)),
        out_shape=jax.ShapeDtypeStruct((depth, rows, d2), BF16),
        compiler_params=_cparams(1),
        name="mem_kv",
    )(mem, g, w_kv)


def _xattn_kernel(x_ref, g_ref, wq_ref, kv_ref, wo_ref, bo_ref, out_ref, o_ref, *, d):
    x = x_ref[...]
    hb = _rms(x, g_ref[...]).astype(BF16)
    hd = d // X_HEADS
    q = (_dot(hb, wq_ref[...]) * (hd ** -0.5)).astype(BF16)
    for h in range(X_HEADS):
        cols = slice(h * hd, (h + 1) * hd)
        s = _dot_nt(q[:, cols], kv_ref[:, cols])
        m = jnp.max(s, axis=-1, keepdims=True)
        p = jnp.exp(s - m)
        l = jnp.sum(p, axis=-1, keepdims=True)
        oh = _dot(p.astype(BF16), kv_ref[:, d + h * hd:d + (h + 1) * hd]) / l
        o_ref[:, cols] = oh.astype(BF16)
    out_ref[...] = x + _dot(o_ref[...], wo_ref[...]) + bo_ref[...]


def _xattn(x, g, w_q, kv, layer, w_o, b_o, *, tm, seq, mem_len):
    t, d = x.shape
    tiles = seq // tm
    kv = kv.reshape(kv.shape[0], t // seq, mem_len, 2 * d)
    row = pl.BlockSpec((tm, d), lambda i: (i, 0))
    return pl.pallas_call(
        functools.partial(_xattn_kernel, d=d),
        grid=(t // tm,),
        in_specs=[row, _full((1, d)), _full((d, d)),
                  pl.BlockSpec((None, None, mem_len, 2 * d), lambda i: (layer, i // tiles, 0, 0)),
                  _full((d, d)), _full((1, d))],
        out_specs=row,
        out_shape=jax.ShapeDtypeStruct((t, d), F32),
        scratch_shapes=[pltpu.VMEM((tm, d), BF16)],
        compiler_params=_cparams(1),
        name="mem_xattn",
    )(x, g, w_q, kv, w_o, b_o)


def _ffn_kernel(xp_ref, x_ref, xn_ref, g_ref, wu_ref, bu_ref, cw_ref, cb_ref, wd_ref, bd_ref, fg_ref,
                out_ref, u0_ref, u1_ref, u2_ref, u3_ref, acc_ref, *, tm, seq, d_ff, cn, final_norm):
    u_bufs = ((u0_ref, u1_ref), (u2_ref, u3_ref))
    tiles = seq // tm
    t = pl.program_id(0) % tiles
    hl = FFN_HALO
    g = g_ref[...]
    x = x_ref[...]
    hb = jnp.concatenate([_rms(xp_ref[...], g), _rms(x, g), _rms(xn_ref[...], g)], axis=0).astype(BF16)

    keep_prev = t > 0
    keep_next = t < tiles - 1

    rows = tm + 2 * hl

    def conv(u_ref, ue, cols):
        bias = bu_ref[:, cols]
        c0, c1, c2 = cw_ref[0:1, cols], cw_ref[1:2, cols], cw_ref[2:3, cols]
        u_ref[0, 1:1 + rows, :] = ue
        u_ref[1, hl - 1:hl - 1 + rows, :] = ue
        u_ref[0, hl:hl + 1, :] = jnp.where(keep_prev, ue[hl - 1:hl, :], -bias)
        u_ref[1, 2 * hl + tm - 1:2 * hl + tm, :] = jnp.where(keep_next, ue[hl + tm:hl + tm + 1, :], -bias)
        return (c0 * u_ref[0, hl:hl + tm, :] + c1 * ue[hl:hl + tm, :] + c2 * u_ref[1, 2 * hl:2 * hl + tm, :]
                + (bias * (c0 + c1 + c2) + cb_ref[:, cols]))

    n_chunks = d_ff // cn

    def up(j):
        return (_dot(hb, wu_ref[:, j * cn:(j + 1) * cn]), _dot(hb, wu_ref[:, d_ff + j * cn:d_ff + (j + 1) * cn]))

    def down(j, act):
        part = _dot(act, wd_ref[j * cn:(j + 1) * cn, :])
        if j == 0:
            acc_ref[...] = part
        else:
            acc_ref[...] += part

    ue = up(0)
    act_prev = None
    for j in range(n_chunks):
        ue_next = up(j + 1) if j + 1 < n_chunks else None
        if act_prev is not None:
            down(j - 1, act_prev)
        gate = conv(u_bufs[j % 2][0], ue[0], slice(j * cn, (j + 1) * cn))
        val = conv(u_bufs[j % 2][1], ue[1], slice(d_ff + j * cn, d_ff + (j + 1) * cn))
        act_prev = (gate * jax.nn.sigmoid(gate) * val).astype(BF16)
        ue = ue_next
    down(n_chunks - 1, act_prev)
    y = x + acc_ref[...] + bd_ref[...]
    if final_norm:
        y = _rms(y, fg_ref[...])
    out_ref[...] = y


def _ffn(x, g, w_up, b_up, conv_w, conv_b, w_down, b_down, final_g, *, tm, seq, cn, final_norm):
    t, d = x.shape
    d_ff = w_down.shape[0]
    hl = FFN_HALO
    blk_per_tile = tm // hl
    n_hl = t // hl
    row = pl.BlockSpec((tm, d), lambda i: (i, 0))
    return pl.pallas_call(
        functools.partial(_ffn_kernel, tm=tm, seq=seq, d_ff=d_ff, cn=cn, final_norm=final_norm),
        grid=(t // tm,),
        in_specs=[pl.BlockSpec((hl, d), lambda i: (jnp.maximum(i * blk_per_tile - 1, 0), 0)),
                  row,
                  pl.BlockSpec((hl, d), lambda i: (jnp.minimum((i + 1) * blk_per_tile, n_hl - 1), 0)),
                  _full((1, d)), _full(w_up.shape), _full((1, 2 * d_ff)), _full(conv_w.shape),
                  _full((1, 2 * d_ff)), _full(w_down.shape), _full((1, d)), _full((1, d))],
        out_specs=row,
        out_shape=jax.ShapeDtypeStruct((t, d), F32),
        scratch_shapes=[pltpu.VMEM((2, tm + 3 * hl, cn), F32)] * 4 + [pltpu.VMEM((tm, d), F32)],
        compiler_params=_cparams(1),
        name="conv_ffn",
    )(x, x, x, g, w_up, b_up, conv_w, conv_b, w_down, b_down, final_g)


def _block_diag(blocks):
    n, a, b = blocks.shape
    eye = jnp.eye(n, dtype=blocks.dtype)
    return (eye[:, None, :, None] * blocks[:, :, None, :]).reshape(n * a, n * b)


def kernel(x, mem, rel_table, mem_norm_g, norm_mix_g, w_in, b_in, gmlp_v_g, gmlp_w_s, gmlp_b_s, pool_w, pool_b, pool_scale, w_out, b_out, norm_mem_g, xattn_w_q, xattn_w_kv, xattn_w_o, xattn_b_o, norm_ffn_g, ffn_w_up, ffn_b_up, ffn_conv_w, ffn_conv_b, ffn_w_down, ffn_b_down, final_norm_g):
    batch, seq, d = x.shape
    mem_len = mem.shape[1]
    depth = w_in.shape[0]
    t = batch * seq
    assert all(w // (2 * dil) == RADIUS for w, dil in DILATED_CONFIGS)

    row1 = lambda a: a.reshape(1, -1)
    ones_bd = jnp.asarray(np.kron(np.eye(A_HEADS), np.ones((HEAD_DIM, HEAD_DIM))), BF16)
    biases = [_attn_bias(rel_table, g, dil) for g, (_, dil) in enumerate(DILATED_CONFIGS)]
    kv = _kv_proj(mem.reshape(batch * mem_len, d), row1(mem_norm_g), xattn_w_kv.astype(BF16))

    xf = x.reshape(t, d)
    for l in range(depth):
        ws_cat = (gmlp_w_s[l].reshape(A_HEADS // 2, 2, GMLP_CHUNK, GMLP_CHUNK)
                  .transpose(0, 2, 1, 3).reshape(A_HEADS // 2, GMLP_CHUNK, 2 * GMLP_CHUNK).astype(BF16))
        bs_full = jnp.repeat(gmlp_b_s[l].T, HEAD_DIM, axis=1)
        ya, zb, *qkv = _mix_in(xf, row1(norm_mix_g[l]), w_in[l].astype(BF16), row1(b_in[l]),
                               row1(gmlp_v_g[l]), ws_cat, bs_full, ones_bd, tm=256)
        o, lse = [], []
        for g, (_, dil) in enumerate(DILATED_CONFIGS):
            og, lg = _attn(qkv[g], biases[g], dilation=dil, batch=batch, seq=seq, lq=512)
            o.append(og)
            lse.append(lg)
        xf = _mix_out(xf, ya, zb, o, lse, _block_diag(pool_w[l]).astype(BF16), row1(pool_b[l]),
                      row1(pool_scale[l]), w_out[l].astype(BF16), row1(b_out[l]), tm=512, seq=seq)
        xf = _xattn(xf, row1(norm_mem_g[l]), xattn_w_q[l].astype(BF16), kv, l,
                    xattn_w_o[l].astype(BF16), row1(xattn_b_o[l]), tm=512, seq=seq, mem_len=mem_len)
        xf = _ffn(xf, row1(norm_ffn_g[l]), ffn_w_up[l].astype(BF16), row1(ffn_b_up[l]), ffn_conv_w[l],
                  row1(ffn_conv_b[l]), ffn_w_down[l].astype(BF16), row1(ffn_b_down[l]), row1(final_norm_g),
                  tm=256, seq=seq, cn=256, final_norm=(l == depth - 1))
    return xf.reshape(batch, seq, d)
```

```python
import functools
import math

import numpy as np
import jax
import jax.numpy as jnp
from jax import lax
from jax.experimental import pallas as pl
from jax.experimental.pallas import tpu as pltpu

F32 = jnp.float32
BF16 = jnp.bfloat16

HEAD_DIM = 64
A_HEADS = 6
A_WIDTH = A_HEADS * HEAD_DIM
GMLP_CHUNK = 128
POOL_WINDOWS = (2, 4, 8, 16)
B_GROUP_DIM = 64
B_WIDTH = len(POOL_WINDOWS) * B_GROUP_DIM
DILATED_CONFIGS = ((128, 1), (512, 4), (2048, 16))
C_GROUPS = len(DILATED_CONFIGS)
C_GROUP_WIDTH = 2 * HEAD_DIM
C_WIDTH = C_GROUPS * C_GROUP_WIDTH
RADIUS = 64
N_BUCKETS = 32
MAX_DISTANCE = 1024
X_HEADS = 4
EPS = 1e-6
NEG_INF = -1e30

LANES = 128
POOL_HALO = 8
FFN_HALO = 8
VMEM_LIMIT = 56 * 1024 * 1024


def _cparams(n_axes):
    return pltpu.CompilerParams(dimension_semantics=("parallel",) * n_axes,
                                vmem_limit_bytes=VMEM_LIMIT)


def _rms(x, g):
    ms = jnp.mean(x * x, axis=-1, keepdims=True)
    return x * lax.rsqrt(ms + EPS) * g


def _gelu_tanh(x):
    return x * (0.5 * (1.0 + jnp.tanh(math.sqrt(2.0 / math.pi) * (x + 0.044715 * (x * x * x)))))


def _dot(a, b):
    return jnp.dot(a, b, preferred_element_type=F32)


def _dot_nt(a, b):
    return lax.dot_general(a, b, (((1,), (1,)), ((), ())), preferred_element_type=F32)


def _full(shape):
    return pl.BlockSpec(shape, lambda *_: (0,) * len(shape))


def _mix_in_kernel(x_ref, g_ref, w_ref, b_ref, vg_ref, ws_ref, bs_ref, ones_ref,
                   ya_ref, zb_ref, g0_ref, g1_ref, g2_ref, qkv_ref, *, tm):
    hb = _rms(x_ref[...], g_ref[...]).astype(BF16)

    def seg(a, b):
        return _dot(hb, w_ref[:, a:b]) + b_ref[:, a:b]

    uv = _gelu_tanh(seg(0, 2 * A_WIDTH))
    u, v = uv[:, :A_WIDTH], uv[:, A_WIDTH:]
    ssum = _dot((v * v).astype(BF16), ones_ref[...])
    vn = v * lax.rsqrt(ssum * (1.0 / HEAD_DIM) + EPS) * vg_ref[...]

    lane = lax.broadcasted_iota(jnp.int32, (GMLP_CHUNK, LANES), 1)
    first_head = lane < HEAD_DIM
    for c in range(tm // GMLP_CHUNK):
        rows = slice(c * GMLP_CHUNK, (c + 1) * GMLP_CHUNK)
        for j in range(A_WIDTH // LANES):
            cols = slice(j * LANES, (j + 1) * LANES)
            vp = vn[rows, cols]
            rhs = jnp.concatenate([jnp.where(first_head, vp, 0.0).astype(BF16),
                                   jnp.where(first_head, 0.0, vp).astype(BF16)], axis=0)
            s = _dot(ws_ref[j], rhs) + bs_ref[:, cols]
            ya_ref[rows, cols] = (u[rows, cols] * s).astype(BF16)

    off = 2 * A_WIDTH
    zb_ref[...] = seg(off, off + B_WIDTH)
    off += B_WIDTH
    qk = seg(off, off + 2 * C_WIDTH)
    parts = (qk[:, :C_WIDTH] * (HEAD_DIM ** -0.5), qk[:, C_WIDTH:], seg(off + 2 * C_WIDTH, off + 3 * C_WIDTH))
    for g, (out_ref, (_, dil)) in enumerate(zip((g0_ref, g1_ref, g2_ref), DILATED_CONFIGS)):
        for part in range(3):
            tile = parts[part][:, g * C_GROUP_WIDTH:(g + 1) * C_GROUP_WIDTH]
            if dil == 1:
                out_ref[part] = tile.astype(BF16)
                continue
            qkv_ref[part, g] = tile
            for r in range(dil):
                piece = qkv_ref[part, g, pl.ds(r, tm // dil, stride=dil), :]
                out_ref[part, :, r * C_GROUP_WIDTH:(r + 1) * C_GROUP_WIDTH] = piece.astype(BF16)


def _mix_in(x, g, w_in, b_in, v_gain, ws_cat, bs_full, ones_bd, *, tm):
    t, d = x.shape
    in_width = w_in.shape[1]
    qkv_shapes = [jax.ShapeDtypeStruct((3, t // dil, dil * C_GROUP_WIDTH), BF16) for _, dil in DILATED_CONFIGS]
    qkv_specs = [pl.BlockSpec((3, tm // dil, dil * C_GROUP_WIDTH), lambda i: (0, i, 0))
                 for _, dil in DILATED_CONFIGS]
    return pl.pallas_call(
        functools.partial(_mix_in_kernel, tm=tm),
        grid=(t // tm,),
        in_specs=[pl.BlockSpec((tm, d), lambda i: (i, 0)),
                  _full((1, d)), _full((d, in_width)), _full((1, in_width)), _full((1, A_WIDTH)),
                  _full(ws_cat.shape), _full(bs_full.shape), _full(ones_bd.shape)],
        out_specs=[pl.BlockSpec((tm, A_WIDTH), lambda i: (i, 0)),
                   pl.BlockSpec((tm, B_WIDTH), lambda i: (i, 0))] + qkv_specs,
        out_shape=[jax.ShapeDtypeStruct((t, A_WIDTH), BF16),
                   jax.ShapeDtypeStruct((t, B_WIDTH), F32)] + qkv_shapes,
        scratch_shapes=[pltpu.VMEM((3, C_GROUPS, tm, C_GROUP_WIDTH), F32)],
        compiler_params=_cparams(1),
        name="mix_in",
    )(x, g, w_in, b_in, v_gain, ws_cat, bs_full, ones_bd)


def _attn_kernel(q_ref, kp_ref, k_ref, kn_ref, vp_ref, v_ref, vn_ref, bias_ref,
                 o_ref, lse_ref, kx_ref, vx_ref, *, lq, n_blocks, n_sub):
    i = pl.program_id(2)
    kx_ref[0:RADIUS] = kp_ref[...]
    kx_ref[RADIUS:RADIUS + lq] = k_ref[...]
    kx_ref[RADIUS + lq:] = kn_ref[...]
    vx_ref[0:RADIUS] = vp_ref[...]
    vx_ref[RADIUS:RADIUS + lq] = v_ref[...]
    vx_ref[RADIUS + lq:] = vn_ref[...]

    qc = 2 * RADIUS
    kc = qc + 2 * RADIUS
    n_chunks = lq // qc
    first_head = lax.broadcasted_iota(jnp.int32, (qc, LANES), 1) < HEAD_DIM
    for c in range(n_chunks):
        rows = slice(c * qc, (c + 1) * qc)
        variant = 1
        if c == 0:
            variant = jnp.where(i == 0, 0, 1)
        if c == n_chunks - 1:
            variant = jnp.where(i == n_blocks - 1, 2, variant)
        for r in range(n_sub):
            lanes = slice(r * LANES, (r + 1) * LANES)
            q = q_ref[rows, lanes]
            keys = kx_ref[c * qc:c * qc + kc, lanes]
            vals = vx_ref[c * qc:c * qc + kc, lanes]
            o_heads, lse_heads = [], []
            for h in range(2):
                qh = jnp.where(first_head if h == 0 else ~first_head, q, jnp.zeros_like(q))
                s = _dot_nt(qh, keys) + bias_ref[variant, h]
                m = jnp.max(s, axis=-1, keepdims=True)
                p = jnp.exp(s - m)
                l = jnp.sum(p, axis=-1, keepdims=True)
                o_heads.append(_dot(p.astype(BF16), vals) / l)
                lse_heads.append(jnp.broadcast_to(m + jnp.log(l), (qc, LANES)))
            o_ref[rows, lanes] = jnp.where(first_head, o_heads[0], o_heads[1])
            lse_ref[rows, lanes] = jnp.where(first_head, lse_heads[0], lse_heads[1])


def _attn(qkv, bias, *, dilation, batch, seq, lq, n_sub):
    sub_len = seq // dilation
    lq = min(lq, sub_len)
    n_sub = min(n_sub, dilation)
    n_blocks = sub_len // lq
    assert sub_len >= 4 * RADIUS
    qkv = qkv.reshape(3, batch, sub_len, dilation * C_GROUP_WIDTH)
    halo_per_blk = lq // RADIUS
    n_halo = sub_len // RADIUS
    width = n_sub * C_GROUP_WIDTH

    def main(part):
        return pl.BlockSpec((None, None, lq, width), lambda b, r, i: (part, b, i, r))

    def prev(part):
        return pl.BlockSpec((None, None, RADIUS, width),
                            lambda b, r, i: (part, b, jnp.maximum(i * halo_per_blk - 1, 0), r))

    def nxt(part):
        return pl.BlockSpec((None, None, RADIUS, width),
                            lambda b, r, i: (part, b, jnp.minimum((i + 1) * halo_per_blk, n_halo - 1), r))

    out_spec = pl.BlockSpec((None, lq, width), lambda b, r, i: (b, i, r))
    out_shape = jax.ShapeDtypeStruct((batch, sub_len, dilation * C_GROUP_WIDTH), F32)
    o, lse = pl.pallas_call(
        functools.partial(_attn_kernel, lq=lq, n_blocks=n_blocks, n_sub=n_sub),
        grid=(batch, dilation // n_sub, n_blocks),
        in_specs=[main(0), prev(1), main(1), nxt(1), prev(2), main(2), nxt(2), _full(bias.shape)],
        out_specs=[out_spec, out_spec],
        out_shape=[out_shape, out_shape],
        scratch_shapes=[pltpu.VMEM((lq + 2 * RADIUS, width), BF16),
                        pltpu.VMEM((lq + 2 * RADIUS, width), BF16)],
        compiler_params=_cparams(3),
        name=f"dilated_attn_d{dilation}",
    )(qkv, qkv, qkv, qkv, qkv, qkv, qkv, bias)
    rows = batch * sub_len
    return o.reshape(rows, dilation * C_GROUP_WIDTH), lse.reshape(rows, dilation * C_GROUP_WIDTH)


def _t5_bucket(rel):
    nb = N_BUCKETS // 2
    ret = (rel > 0).astype(np.int32) * nb
    n = np.abs(rel)
    max_exact = nb // 2
    large = max_exact + (np.log(np.maximum(n, 1) / max_exact)
                         / math.log(MAX_DISTANCE / max_exact) * (nb - max_exact)).astype(np.int32)
    large = np.minimum(large, nb - 1)
    return ret + np.where(n < max_exact, n, large)


def _attn_bias(rel_table, group, dilation):
    qc, kc = 2 * RADIUS, 4 * RADIUS
    n = qc + kc - 1
    delta = (np.arange(n) + qc - 1) % n - (qc + RADIUS - 1)
    onehot = (_t5_bucket(delta * dilation)[:, None] == np.arange(N_BUCKETS)[None, :]).astype(np.float32)
    per_delta = jnp.dot(onehot, rel_table[:, 2 * group:2 * group + 2], precision=lax.Precision.HIGHEST)
    per_delta = jnp.where((np.abs(delta) <= RADIUS)[:, None], per_delta, NEG_INF).T
    reps = -(-(qc * (n - 1)) // n)
    band = jnp.tile(per_delta, (1, reps))[:, :qc * (n - 1)].reshape(2, qc, n - 1)[:, :, :kc]
    key = np.arange(kc)[None, None, :]
    return jnp.stack([jnp.where(key < RADIUS, NEG_INF, band), band, jnp.where(key >= kc - RADIUS, NEG_INF, band)])


def _mix_out_kernel(x_ref, ya_ref, zp_ref, zb_ref, zn_ref, o0_ref, o1_ref, o2_ref, l0_ref, l1_ref, l2_ref,
                    wp_ref, bp_ref, ps_ref, wo_ref, bo_ref, out_ref, zx_ref, nat_ref, *, tm, seq):
    tiles = seq // tm
    t = pl.program_id(0) % tiles
    hl = POOL_HALO
    zx_ref[0:hl] = jnp.where(t > 0, zp_ref[...], 0.0)
    zx_ref[hl:hl + tm] = zb_ref[...]
    zx_ref[hl + tm:] = jnp.where(t < tiles - 1, zn_ref[...], 0.0)

    pos = t * tm + lax.broadcasted_iota(jnp.int32, (tm, LANES), 0)
    lane = lax.broadcasted_iota(jnp.int32, (tm, LANES), 1)
    first_group = lane < B_GROUP_DIM
    pooled = []
    for cb in range(B_WIDTH // LANES):
        cols = slice(cb * LANES, (cb + 1) * LANES)
        half_a, half_b = POOL_WINDOWS[2 * cb] // 2, POOL_WINDOWS[2 * cb + 1] // 2

        def shifted(j):
            return zx_ref[hl + j:hl + j + tm, cols]

        acc = shifted(-1) + shifted(0)
        sums = {1: acc}
        for half in (2, 4, 8):
            if half > half_b:
                break
            for j in range(half // 2, half):
                acc = acc + shifted(-j - 1) + shifted(j)
            sums[half] = acc
        half = jnp.where(first_group, half_a, half_b)
        cnt = (jnp.minimum(pos + half, seq) - jnp.maximum(pos - half, 0)).astype(F32)
        wsum = jnp.where(first_group, sums[half_a], sums[half_b])
        pooled.append(wsum / cnt - zb_ref[:, cols])
    pooled = jnp.concatenate(pooled, axis=-1).astype(BF16)
    yb = (_dot(pooled, wp_ref[...]) + bp_ref[...]) * ps_ref[...]

    def natural(src_ref, slot, dil):
        if dil == 1:
            return src_ref[...]
        for r in range(dil):
            nat_ref[slot, pl.ds(r, tm // dil, stride=dil), :] = src_ref[:, r * C_GROUP_WIDTH:(r + 1) * C_GROUP_WIDTH]
        return nat_ref[slot]

    dils = [dil for _, dil in DILATED_CONFIGS]
    o0, o1, o2 = [natural(r, s, dl) for s, (r, dl) in enumerate(zip((o0_ref, o1_ref, o2_ref), dils))]
    l0, l1, l2 = [natural(r, s + 3, dl) for s, (r, dl) in enumerate(zip((l0_ref, l1_ref, l2_ref), dils))]
    m = jnp.maximum(l0, jnp.maximum(l1, l2))
    e0, e1, e2 = jnp.exp(l0 - m), jnp.exp(l1 - m), jnp.exp(l2 - m)
    yc = (e0 * o0 + e1 * o1 + e2 * o2) / (e0 + e1 + e2)

    y = (_dot(ya_ref[...], wo_ref[0:A_WIDTH, :])
         + _dot(yb.astype(BF16), wo_ref[A_WIDTH:A_WIDTH + B_WIDTH, :])
         + _dot(yc.astype(BF16), wo_ref[A_WIDTH + B_WIDTH:, :]))
    out_ref[...] = x_ref[...] + y + bo_ref[...]


def _mix_out(x, ya, zb, o, lse, wp_bd, bp, ps, w_out, b_out, *, tm, seq):
    t, d = x.shape
    hl = POOL_HALO
    blk_per_tile = tm // hl
    n_hl = t // hl
    row = lambda w: pl.BlockSpec((tm, w), lambda i: (i, 0))
    grp = [pl.BlockSpec((tm // dil, dil * C_GROUP_WIDTH), lambda i: (i, 0)) for _, dil in DILATED_CONFIGS]
    return pl.pallas_call(
        functools.partial(_mix_out_kernel, tm=tm, seq=seq),
        grid=(t // tm,),
        in_specs=[row(d), row(A_WIDTH),
                  pl.BlockSpec((hl, B_WIDTH), lambda i: (jnp.maximum(i * blk_per_tile - 1, 0), 0)),
                  row(B_WIDTH),
                  pl.BlockSpec((hl, B_WIDTH), lambda i: (jnp.minimum((i + 1) * blk_per_tile, n_hl - 1), 0)),
                  *grp, *grp,
                  _full(wp_bd.shape), _full((1, B_WIDTH)), _full((1, B_WIDTH)),
                  _full(w_out.shape), _full((1, d))],
        out_specs=row(d),
        out_shape=jax.ShapeDtypeStruct((t, d), F32),
        scratch_shapes=[pltpu.VMEM((tm + 2 * hl, B_WIDTH), F32),
                        pltpu.VMEM((2 * C_GROUPS, tm, C_GROUP_WIDTH), F32)],
        compiler_params=_cparams(1),
        name="mix_out",
    )(x, ya, zb, zb, zb, o[0], o[1], o[2], lse[0], lse[1], lse[2], wp_bd, bp, ps, w_out, b_out)


def _kv_kernel(mem_ref, g_ref, w_ref, kv_ref):
    mem_n = _rms(mem_ref[...], g_ref[...]).astype(BF16)
    kv_ref[...] = _dot(mem_n, w_ref[...]).astype(BF16)


def _kv_proj(mem, g, w_kv):
    depth, d, d2 = w_kv.shape
    rows = mem.shape[0]
    return pl.pallas_call(
        _kv_kernel,
        grid=(depth,),
        in_specs=[_full((rows, d)), _full((1, d)), pl.BlockSpec((None, d, d2), lambda l: (l, 0, 0))],
        out_specs=pl.BlockSpec((None, rows, d2), lambda l: (l, 0, 0)),
        out_shape=jax.ShapeDtypeStruct((depth, rows, d2), BF16),
        compiler_params=_cparams(1),
        name="mem_kv",
    )(mem, g, w_kv)


def _xattn_kernel(x_ref, g_ref, wq_ref, kv_ref, wo_ref, bo_ref, out_ref, o_ref, *, d):
    x = x_ref[...]
    hb = _rms(x, g_ref[...]).astype(BF16)
    hd = d // X_HEADS
    q = (_dot(hb, wq_ref[...]) * (hd ** -0.5)).astype(BF16)
    for h in range(X_HEADS):
        cols = slice(h * hd, (h + 1) * hd)
        s = _dot_nt(q[:, cols], kv_ref[:, cols])
        m = jnp.max(s, axis=-1, keepdims=True)
        p = jnp.exp(s - m)
        l = jnp.sum(p, axis=-1, keepdims=True)
        oh = _dot(p.astype(BF16), kv_ref[:, d + h * hd:d + (h + 1) * hd]) / l
        o_ref[:, cols] = oh.astype(BF16)
    out_ref[...] = x + _dot(o_ref[...], wo_ref[...]) + bo_ref[...]


def _xattn(x, g, w_q, kv, layer, w_o, b_o, *, tm, seq, mem_len):
    t, d = x.shape
    tiles = seq // tm
    kv = kv.reshape(kv.shape[0], t // seq, mem_len, 2 * d)
    row = pl.BlockSpec((tm, d), lambda i: (i, 0))
    return pl.pallas_call(
        functools.partial(_xattn_kernel, d=d),
        grid=(t // tm,),
        in_specs=[row, _full((1, d)), _full((d, d)),
                  pl.BlockSpec((None, None, mem_len, 2 * d), lambda i: (layer, i // tiles, 0, 0)),
                  _full((d, d)), _full((1, d))],
        out_specs=row,
        out_shape=jax.ShapeDtypeStruct((t, d), F32),
        scratch_shapes=[pltpu.VMEM((tm, d), BF16)],
        compiler_params=_cparams(1),
        name="mem_xattn",
    )(x, g, w_q, kv, w_o, b_o)


def _ffn_kernel(xp_ref, x_ref, xn_ref, g_ref, wu_ref, bu_ref, cw_ref, cb_ref, wd_ref, bd_ref, fg_ref,
                out_ref, u0_ref, u1_ref, u2_ref, u3_ref, acc_ref, *, tm, seq, d_ff, cn, final_norm):
    u_bufs = ((u0_ref, u1_ref), (u2_ref, u3_ref))
    tiles = seq // tm
    t = pl.program_id(0) % tiles
    hl = FFN_HALO
    g = g_ref[...]
    x = x_ref[...]
    hb = jnp.concatenate([_rms(xp_ref[...], g), _rms(x, g), _rms(xn_ref[...], g)], axis=0).astype(BF16)

    keep_prev = t > 0
    keep_next = t < tiles - 1

    rows = tm + 2 * hl

    def conv(u_ref, ue, cols):
        bias = bu_ref[:, cols]
        c0, c1, c2 = cw_ref[0:1, cols], cw_ref[1:2, cols], cw_ref[2:3, cols]
        u_ref[0, 1:1 + rows, :] = ue
        u_ref[1, hl - 1:hl - 1 + rows, :] = ue
        u_ref[0, hl:hl + 1, :] = jnp.where(keep_prev, ue[hl - 1:hl, :], -bias)
        u_ref[1, 2 * hl + tm - 1:2 * hl + tm, :] = jnp.where(keep_next, ue[hl + tm:hl + tm + 1, :], -bias)
        return (c0 * u_ref[0, hl:hl + tm, :] + c1 * ue[hl:hl + tm, :] + c2 * u_ref[1, 2 * hl:2 * hl + tm, :]
                + (bias * (c0 + c1 + c2) + cb_ref[:, cols]))

    n_chunks = d_ff // cn

    def up(j):
        return (_dot(hb, wu_ref[:, j * cn:(j + 1) * cn]), _dot(hb, wu_ref[:, d_ff + j * cn:d_ff + (j + 1) * cn]))

    def down(j, act):
        part = _dot(act, wd_ref[j * cn:(j + 1) * cn, :])
        if j == 0:
            acc_ref[...] = part
        else:
            acc_ref[...] += part

    ue = up(0)
    act_prev = None
    for j in range(n_chunks):
        ue_next = up(j + 1) if j + 1 < n_chunks else None
        if act_prev is not None:
            down(j - 1, act_prev)
        gate = conv(u_bufs[j % 2][0], ue[0], slice(j * cn, (j + 1) * cn))
        val = conv(u_bufs[j % 2][1], ue[1], slice(d_ff + j * cn, d_ff + (j + 1) * cn))
        act_prev = (gate * jax.nn.sigmoid(gate) * val).astype(BF16)
        ue = ue_next
    down(n_chunks - 1, act_prev)
    y = x + acc_ref[...] + bd_ref[...]
    if final_norm:
        y = _rms(y, fg_ref[...])
    out_ref[...] = y


def _ffn(x, g, w_up, b_up, conv_w, conv_b, w_down, b_down, final_g, *, tm, seq, cn, final_norm):
    t, d = x.shape
    d_ff = w_down.shape[0]
    hl = FFN_HALO
    blk_per_tile = tm // hl
    n_hl = t // hl
    row = pl.BlockSpec((tm, d), lambda i: (i, 0))
    return pl.pallas_call(
        functools.partial(_ffn_kernel, tm=tm, seq=seq, d_ff=d_ff, cn=cn, final_norm=final_norm),
        grid=(t // tm,),
        in_specs=[pl.BlockSpec((hl, d), lambda i: (jnp.maximum(i * blk_per_tile - 1, 0), 0)),
                  row,
                  pl.BlockSpec((hl, d), lambda i: (jnp.minimum((i + 1) * blk_per_tile, n_hl - 1), 0)),
                  _full((1, d)), _full(w_up.shape), _full((1, 2 * d_ff)), _full(conv_w.shape),
                  _full((1, 2 * d_ff)), _full(w_down.shape), _full((1, d)), _full((1, d))],
        out_specs=row,
        out_shape=jax.ShapeDtypeStruct((t, d), F32),
        scratch_shapes=[pltpu.VMEM((2, tm + 3 * hl, cn), F32)] * 4 + [pltpu.VMEM((tm, d), F32)],
        compiler_params=_cparams(1),
        name="conv_ffn",
    )(x, x, x, g, w_up, b_up, conv_w, conv_b, w_down, b_down, final_g)


def _block_diag(blocks):
    n, a, b = blocks.shape
    eye = jnp.eye(n, dtype=blocks.dtype)
    return (eye[:, None, :, None] * blocks[:, :, None, :]).reshape(n * a, n * b)


def kernel(x, mem, rel_table, mem_norm_g, norm_mix_g, w_in, b_in, gmlp_v_g, gmlp_w_s, gmlp_b_s, pool_w, pool_b, pool_scale, w_out, b_out, norm_mem_g, xattn_w_q, xattn_w_kv, xattn_w_o, xattn_b_o, norm_ffn_g, ffn_w_up, ffn_b_up, ffn_conv_w, ffn_conv_b, ffn_w_down, ffn_b_down, final_norm_g):
    batch, seq, d = x.shape
    mem_len = mem.shape[1]
    depth = w_in.shape[0]
    t = batch * seq
    assert all(w // (2 * dil) == RADIUS for w, dil in DILATED_CONFIGS)

    row1 = lambda a: a.reshape(1, -1)
    ones_bd = jnp.asarray(np.kron(np.eye(A_HEADS), np.ones((HEAD_DIM, HEAD_DIM))), BF16)
    biases = [_attn_bias(rel_table, g, dil) for g, (_, dil) in enumerate(DILATED_CONFIGS)]
    kv = _kv_proj(mem.reshape(batch * mem_len, d), row1(mem_norm_g), xattn_w_kv.astype(BF16))

    xf = x.reshape(t, d)
    for l in range(depth):
        ws_cat = (gmlp_w_s[l].reshape(A_HEADS // 2, 2, GMLP_CHUNK, GMLP_CHUNK)
                  .transpose(0, 2, 1, 3).reshape(A_HEADS // 2, GMLP_CHUNK, 2 * GMLP_CHUNK).astype(BF16))
        bs_full = jnp.repeat(gmlp_b_s[l].T, HEAD_DIM, axis=1)
        ya, zb, *qkv = _mix_in(xf, row1(norm_mix_g[l]), w_in[l].astype(BF16), row1(b_in[l]),
                               row1(gmlp_v_g[l]), ws_cat, bs_full, ones_bd, tm=512)
        o, lse = [], []
        for g, (_, dil) in enumerate(DILATED_CONFIGS):
            og, lg = _attn(qkv[g], biases[g], dilation=dil, batch=batch, seq=seq, lq=1024 // min(dil, 4),
                           n_sub=4)
            o.append(og)
            lse.append(lg)
        xf = _mix_out(xf, ya, zb, o, lse, _block_diag(pool_w[l]).astype(BF16), row1(pool_b[l]),
                      row1(pool_scale[l]), w_out[l].astype(BF16), row1(b_out[l]), tm=512, seq=seq)
        xf = _xattn(xf, row1(norm_mem_g[l]), xattn_w_q[l].astype(BF16), kv, l,
                    xattn_w_o[l].astype(BF16), row1(xattn_b_o[l]), tm=512, seq=seq, mem_len=mem_len)
        xf = _ffn(xf, row1(norm_ffn_g[l]), ffn_w_up[l].astype(BF16), row1(ffn_b_up[l]), ffn_conv_w[l],
                  row1(ffn_conv_b[l]), ffn_w_down[l].astype(BF16), row1(ffn_b_down[l]), row1(final_norm_g),
                  tm=256, seq=seq, cn=256, final_norm=(l == depth - 1))
    return xf.reshape(batch, seq, d)
```

```python
import functools
import math

import numpy as np
import jax
import jax.numpy as jnp
from jax import lax
from jax.experimental import pallas as pl
from jax.experimental.pallas import tpu as pltpu

F32 = jnp.float32
BF16 = jnp.bfloat16

HEAD_DIM = 64
A_HEADS = 6
A_WIDTH = A_HEADS * HEAD_DIM
GMLP_CHUNK = 128
POOL_WINDOWS = (2, 4, 8, 16)
B_GROUP_DIM = 64
B_WIDTH = len(POOL_WINDOWS) * B_GROUP_DIM
DILATED_CONFIGS = ((128, 1), (512, 4), (2048, 16))
C_GROUPS = len(DILATED_CONFIGS)
C_GROUP_WIDTH = 2 * HEAD_DIM
C_WIDTH = C_GROUPS * C_GROUP_WIDTH
RADIUS = 64
N_BUCKETS = 32
MAX_DISTANCE = 1024
X_HEADS = 4
EPS = 1e-6
NEG_INF = -1e30

LANES = 128
POOL_HALO = 8
FFN_HALO = 8
VMEM_LIMIT = 56 * 1024 * 1024


def _cparams(n_axes):
    return pltpu.CompilerParams(dimension_semantics=("parallel",) * n_axes,
                                vmem_limit_bytes=VMEM_LIMIT)


def _rms(x, g):
    ms = jnp.mean(x * x, axis=-1, keepdims=True)
    return x * lax.rsqrt(ms + EPS) * g


def _gelu_tanh(x):
    return x * (0.5 * (1.0 + jnp.tanh(math.sqrt(2.0 / math.pi) * (x + 0.044715 * (x * x * x)))))


def _dot(a, b):
    return jnp.dot(a, b, preferred_element_type=F32)


def _dot_nt(a, b):
    return lax.dot_general(a, b, (((1,), (1,)), ((), ())), preferred_element_type=F32)


def _full(shape):
    return pl.BlockSpec(shape, lambda *_: (0,) * len(shape))


def _mix_in_kernel(x_ref, g_ref, w_ref, b_ref, vg_ref, ws_ref, bs_ref, ones_ref,
                   ya_ref, zb_ref, g0_ref, g1_ref, g2_ref, qkv_ref, *, tm):
    hb = _rms(x_ref[...], g_ref[...]).astype(BF16)

    def seg(a, b):
        return _dot(hb, w_ref[:, a:b]) + b_ref[:, a:b]

    uv = _gelu_tanh(seg(0, 2 * A_WIDTH))
    u, v = uv[:, :A_WIDTH], uv[:, A_WIDTH:]
    ssum = _dot((v * v).astype(BF16), ones_ref[...])
    vn = v * lax.rsqrt(ssum * (1.0 / HEAD_DIM) + EPS) * vg_ref[...]

    lane = lax.broadcasted_iota(jnp.int32, (GMLP_CHUNK, LANES), 1)
    first_head = lane < HEAD_DIM
    for c in range(tm // GMLP_CHUNK):
        rows = slice(c * GMLP_CHUNK, (c + 1) * GMLP_CHUNK)
        for j in range(A_WIDTH // LANES):
            cols = slice(j * LANES, (j + 1) * LANES)
            vp = vn[rows, cols]
            rhs = jnp.concatenate([jnp.where(first_head, vp, 0.0).astype(BF16),
                                   jnp.where(first_head, 0.0, vp).astype(BF16)], axis=0)
            s = _dot(ws_ref[j], rhs) + bs_ref[:, cols]
            ya_ref[rows, cols] = (u[rows, cols] * s).astype(BF16)

    off = 2 * A_WIDTH
    zb_ref[...] = seg(off, off + B_WIDTH)
    off += B_WIDTH
    qk = seg(off, off + 2 * C_WIDTH)
    parts = (qk[:, :C_WIDTH] * (HEAD_DIM ** -0.5), qk[:, C_WIDTH:], seg(off + 2 * C_WIDTH, off + 3 * C_WIDTH))
    for g, (out_ref, (_, dil)) in enumerate(zip((g0_ref, g1_ref, g2_ref), DILATED_CONFIGS)):
        for part in range(3):
            tile = parts[part][:, g * C_GROUP_WIDTH:(g + 1) * C_GROUP_WIDTH]
            if dil == 1:
                out_ref[part] = tile.astype(BF16)
                continue
            qkv_ref[part, g] = tile
            for r in range(dil):
                piece = qkv_ref[part, g, pl.ds(r, tm // dil, stride=dil), :]
                out_ref[part, :, r * C_GROUP_WIDTH:(r + 1) * C_GROUP_WIDTH] = piece.astype(BF16)


def _mix_in(x, g, w_in, b_in, v_gain, ws_cat, bs_full, ones_bd, *, tm):
    t, d = x.shape
    in_width = w_in.shape[1]
    qkv_shapes = [jax.ShapeDtypeStruct((3, t // dil, dil * C_GROUP_WIDTH), BF16) for _, dil in DILATED_CONFIGS]
    qkv_specs = [pl.BlockSpec((3, tm // dil, dil * C_GROUP_WIDTH), lambda i: (0, i, 0))
                 for _, dil in DILATED_CONFIGS]
    return pl.pallas_call(
        functools.partial(_mix_in_kernel, tm=tm),
        grid=(t // tm,),
        in_specs=[pl.BlockSpec((tm, d), lambda i: (i, 0)),
                  _full((1, d)), _full((d, in_width)), _full((1, in_width)), _full((1, A_WIDTH)),
                  _full(ws_cat.shape), _full(bs_full.shape), _full(ones_bd.shape)],
        out_specs=[pl.BlockSpec((tm, A_WIDTH), lambda i: (i, 0)),
                   pl.BlockSpec((tm, B_WIDTH), lambda i: (i, 0))] + qkv_specs,
        out_shape=[jax.ShapeDtypeStruct((t, A_WIDTH), BF16),
                   jax.ShapeDtypeStruct((t, B_WIDTH), F32)] + qkv_shapes,
        scratch_shapes=[pltpu.VMEM((3, C_GROUPS, tm, C_GROUP_WIDTH), F32)],
        compiler_params=_cparams(1),
        name="mix_in",
    )(x, g, w_in, b_in, v_gain, ws_cat, bs_full, ones_bd)


def _attn_kernel(q_ref, kp_ref, k_ref, kn_ref, vp_ref, v_ref, vn_ref, bias_ref,
                 o_ref, lse_ref, kx_ref, vx_ref, *, lq, n_blocks, n_sub):
    i = pl.program_id(2)
    kx_ref[0:RADIUS] = kp_ref[...]
    kx_ref[RADIUS:RADIUS + lq] = k_ref[...]
    kx_ref[RADIUS + lq:] = kn_ref[...]
    vx_ref[0:RADIUS] = vp_ref[...]
    vx_ref[RADIUS:RADIUS + lq] = v_ref[...]
    vx_ref[RADIUS + lq:] = vn_ref[...]

    qc = 2 * RADIUS
    kc = qc + 2 * RADIUS
    n_chunks = lq // qc
    first_head = lax.broadcasted_iota(jnp.int32, (qc, LANES), 1) < HEAD_DIM
    for c in range(n_chunks):
        rows = slice(c * qc, (c + 1) * qc)
        variant = 1
        if c == 0:
            variant = jnp.where(i == 0, 0, 1)
        if c == n_chunks - 1:
            variant = jnp.where(i == n_blocks - 1, 2, variant)
        for r in range(n_sub):
            lanes = slice(r * LANES, (r + 1) * LANES)
            q = q_ref[rows, lanes]
            keys = kx_ref[c * qc:c * qc + kc, lanes]
            vals = vx_ref[c * qc:c * qc + kc, lanes]
            o_heads, lse_heads = [], []
            for h in range(2):
                qh = jnp.where(first_head if h == 0 else ~first_head, q, jnp.zeros_like(q))
                s = _dot_nt(qh, keys) + bias_ref[variant, h]
                m = jnp.max(s, axis=-1, keepdims=True)
                p = jnp.exp(s - m)
                l = jnp.sum(p, axis=-1, keepdims=True)
                o_heads.append(_dot(p.astype(BF16), vals) / l)
                lse_heads.append(jnp.broadcast_to(m + jnp.log(l), (qc, LANES)))
            o_ref[rows, lanes] = jnp.where(first_head, o_heads[0], o_heads[1])
            lse_ref[rows, lanes] = jnp.where(first_head, lse_heads[0], lse_heads[1])


def _attn(qkv, bias, *, dilation, batch, seq, lq, n_sub):
    sub_len = seq // dilation
    lq = min(lq, sub_len)
    n_sub = min(n_sub, dilation)
    n_blocks = sub_len // lq
    assert sub_len >= 4 * RADIUS
    qkv = qkv.reshape(3, batch, sub_len, dilation * C_GROUP_WIDTH)
    halo_per_blk = lq // RADIUS
    n_halo = sub_len // RADIUS
    width = n_sub * C_GROUP_WIDTH

    def main(part):
        return pl.BlockSpec((None, None, lq, width), lambda b, r, i: (part, b, i, r))

    def prev(part):
        return pl.BlockSpec((None, None, RADIUS, width),
                            lambda b, r, i: (part, b, jnp.maximum(i * halo_per_blk - 1, 0), r))

    def nxt(part):
        return pl.BlockSpec((None, None, RADIUS, width),
                            lambda b, r, i: (part, b, jnp.minimum((i + 1) * halo_per_blk, n_halo - 1), r))

    out_spec = pl.BlockSpec((None, lq, width), lambda b, r, i: (b, i, r))
    out_shape = jax.ShapeDtypeStruct((batch, sub_len, dilation * C_GROUP_WIDTH), F32)
    o, lse = pl.pallas_call(
        functools.partial(_attn_kernel, lq=lq, n_blocks=n_blocks, n_sub=n_sub),
        grid=(batch, dilation // n_sub, n_blocks),
        in_specs=[main(0), prev(1), main(1), nxt(1), prev(2), main(2), nxt(2), _full(bias.shape)],
        out_specs=[out_spec, out_spec],
        out_shape=[out_shape, out_shape],
        scratch_shapes=[pltpu.VMEM((lq + 2 * RADIUS, width), BF16),
                        pltpu.VMEM((lq + 2 * RADIUS, width), BF16)],
        compiler_params=_cparams(3),
        name=f"dilated_attn_d{dilation}",
    )(qkv, qkv, qkv, qkv, qkv, qkv, qkv, bias)
    rows = batch * sub_len
    return o.reshape(rows, dilation * C_GROUP_WIDTH), lse.reshape(rows, dilation * C_GROUP_WIDTH)


def _t5_bucket(rel):
    nb = N_BUCKETS // 2
    ret = (rel > 0).astype(np.int32) * nb
    n = np.abs(rel)
    max_exact = nb // 2
    large = max_exact + (np.log(np.maximum(n, 1) / max_exact)
                         / math.log(MAX_DISTANCE / max_exact) * (nb - max_exact)).astype(np.int32)
    large = np.minimum(large, nb - 1)
    return ret + np.where(n < max_exact, n, large)


def _attn_bias(rel_table, group, dilation):
    qc, kc = 2 * RADIUS, 4 * RADIUS
    n = qc + kc - 1
    delta = (np.arange(n) + qc - 1) % n - (qc + RADIUS - 1)
    onehot = (_t5_bucket(delta * dilation)[:, None] == np.arange(N_BUCKETS)[None, :]).astype(np.float32)
    per_delta = jnp.dot(onehot, rel_table[:, 2 * group:2 * group + 2], precision=lax.Precision.HIGHEST)
    per_delta = jnp.where((np.abs(delta) <= RADIUS)[:, None], per_delta, NEG_INF).T
    reps = -(-(qc * (n - 1)) // n)
    band = jnp.tile(per_delta, (1, reps))[:, :qc * (n - 1)].reshape(2, qc, n - 1)[:, :, :kc]
    key = np.arange(kc)[None, None, :]
    return jnp.stack([jnp.where(key < RADIUS, NEG_INF, band), band, jnp.where(key >= kc - RADIUS, NEG_INF, band)])


def _mix_out_kernel(x_ref, ya_ref, zp_ref, zb_ref, zn_ref, o0_ref, o1_ref, o2_ref, l0_ref, l1_ref, l2_ref,
                    wp_ref, bp_ref, ps_ref, wo_ref, bo_ref, xg_ref, xwq_ref, kv_ref, xwo_ref, xbo_ref,
                    out_ref, zx_ref, nat_ref, xo_ref, *, tm, seq):
    tiles = seq // tm
    t = pl.program_id(0) % tiles
    hl = POOL_HALO
    zx_ref[0:hl] = jnp.where(t > 0, zp_ref[...], 0.0)
    zx_ref[hl:hl + tm] = zb_ref[...]
    zx_ref[hl + tm:] = jnp.where(t < tiles - 1, zn_ref[...], 0.0)

    pos = t * tm + lax.broadcasted_iota(jnp.int32, (tm, LANES), 0)
    lane = lax.broadcasted_iota(jnp.int32, (tm, LANES), 1)
    first_group = lane < B_GROUP_DIM
    pooled = []
    for cb in range(B_WIDTH // LANES):
        cols = slice(cb * LANES, (cb + 1) * LANES)
        half_a, half_b = POOL_WINDOWS[2 * cb] // 2, POOL_WINDOWS[2 * cb + 1] // 2

        def shifted(j):
            return zx_ref[hl + j:hl + j + tm, cols]

        acc = shifted(-1) + shifted(0)
        sums = {1: acc}
        for half in (2, 4, 8):
            if half > half_b:
                break
            for j in range(half // 2, half):
                acc = acc + shifted(-j - 1) + shifted(j)
            sums[half] = acc
        half = jnp.where(first_group, half_a, half_b)
        cnt = (jnp.minimum(pos + half, seq) - jnp.maximum(pos - half, 0)).astype(F32)
        wsum = jnp.where(first_group, sums[half_a], sums[half_b])
        pooled.append(wsum / cnt - zb_ref[:, cols])
    pooled = jnp.concatenate(pooled, axis=-1).astype(BF16)
    yb = (_dot(pooled, wp_ref[...]) + bp_ref[...]) * ps_ref[...]

    def natural(src_ref, slot, dil):
        if dil == 1:
            return src_ref[...]
        for r in range(dil):
            nat_ref[slot, pl.ds(r, tm // dil, stride=dil), :] = src_ref[:, r * C_GROUP_WIDTH:(r + 1) * C_GROUP_WIDTH]
        return nat_ref[slot]

    dils = [dil for _, dil in DILATED_CONFIGS]
    o0, o1, o2 = [natural(r, s, dl) for s, (r, dl) in enumerate(zip((o0_ref, o1_ref, o2_ref), dils))]
    l0, l1, l2 = [natural(r, s + 3, dl) for s, (r, dl) in enumerate(zip((l0_ref, l1_ref, l2_ref), dils))]
    m = jnp.maximum(l0, jnp.maximum(l1, l2))
    e0, e1, e2 = jnp.exp(l0 - m), jnp.exp(l1 - m), jnp.exp(l2 - m)
    yc = (e0 * o0 + e1 * o1 + e2 * o2) / (e0 + e1 + e2)

    y = (_dot(ya_ref[...], wo_ref[0:A_WIDTH, :])
         + _dot(yb.astype(BF16), wo_ref[A_WIDTH:A_WIDTH + B_WIDTH, :])
         + _dot(yc.astype(BF16), wo_ref[A_WIDTH + B_WIDTH:, :]))
    x1 = x_ref[...] + y + bo_ref[...]
    out_ref[...] = _cross_attend(x1, xg_ref, xwq_ref, kv_ref, xwo_ref, xbo_ref, xo_ref)


def _mix_out(x, ya, zb, o, lse, wp_bd, bp, ps, w_out, b_out, xg, xw_q, kv, layer, xw_o, xb_o, *, tm, seq, mem_len):
    t, d = x.shape
    tiles = seq // tm
    kv = kv.reshape(kv.shape[0], t // seq, mem_len, 2 * d)
    hl = POOL_HALO
    blk_per_tile = tm // hl
    n_hl = t // hl
    row = lambda w: pl.BlockSpec((tm, w), lambda i: (i, 0))
    grp = [pl.BlockSpec((tm // dil, dil * C_GROUP_WIDTH), lambda i: (i, 0)) for _, dil in DILATED_CONFIGS]
    return pl.pallas_call(
        functools.partial(_mix_out_kernel, tm=tm, seq=seq),
        grid=(t // tm,),
        in_specs=[row(d), row(A_WIDTH),
                  pl.BlockSpec((hl, B_WIDTH), lambda i: (jnp.maximum(i * blk_per_tile - 1, 0), 0)),
                  row(B_WIDTH),
                  pl.BlockSpec((hl, B_WIDTH), lambda i: (jnp.minimum((i + 1) * blk_per_tile, n_hl - 1), 0)),
                  *grp, *grp,
                  _full(wp_bd.shape), _full((1, B_WIDTH)), _full((1, B_WIDTH)),
                  _full(w_out.shape), _full((1, d)),
                  _full((1, d)), _full((d, d)),
                  pl.BlockSpec((None, None, mem_len, 2 * d), lambda i: (layer, i // tiles, 0, 0)),
                  _full((d, d)), _full((1, d))],
        out_specs=row(d),
        out_shape=jax.ShapeDtypeStruct((t, d), F32),
        scratch_shapes=[pltpu.VMEM((tm + 2 * hl, B_WIDTH), F32),
                        pltpu.VMEM((2 * C_GROUPS, tm, C_GROUP_WIDTH), F32),
                        pltpu.VMEM((tm, d), BF16)],
        compiler_params=_cparams(1),
        name="mix_out_xattn",
    )(x, ya, zb, zb, zb, o[0], o[1], o[2], lse[0], lse[1], lse[2], wp_bd, bp, ps, w_out, b_out,
      xg, xw_q, kv, xw_o, xb_o)


def _kv_kernel(mem_ref, g_ref, w_ref, kv_ref):
    mem_n = _rms(mem_ref[...], g_ref[...]).astype(BF16)
    kv_ref[...] = _dot(mem_n, w_ref[...]).astype(BF16)


def _kv_proj(mem, g, w_kv):
    depth, d, d2 = w_kv.shape
    rows = mem.shape[0]
    return pl.pallas_call(
        _kv_kernel,
        grid=(depth,),
        in_specs=[_full((rows, d)), _full((1, d)), pl.BlockSpec((None, d, d2), lambda l: (l, 0, 0))],
        out_specs=pl.BlockSpec((None, rows, d2), lambda l: (l, 0, 0)),
        out_shape=jax.ShapeDtypeStruct((depth, rows, d2), BF16),
        compiler_params=_cparams(1),
        name="mem_kv",
    )(mem, g, w_kv)


def _cross_attend(x, g_ref, wq_ref, kv_ref, wo_ref, bo_ref, o_ref):
    d = x.shape[1]
    hb = _rms(x, g_ref[...]).astype(BF16)
    hd = d // X_HEADS
    q = (_dot(hb, wq_ref[...]) * (hd ** -0.5)).astype(BF16)
    for h in range(X_HEADS):
        cols = slice(h * hd, (h + 1) * hd)
        s = _dot_nt(q[:, cols], kv_ref[:, cols])
        m = jnp.max(s, axis=-1, keepdims=True)
        p = jnp.exp(s - m)
        l = jnp.sum(p, axis=-1, keepdims=True)
        oh = _dot(p.astype(BF16), kv_ref[:, d + h * hd:d + (h + 1) * hd]) / l
        o_ref[:, cols] = oh.astype(BF16)
    return x + _dot(o_ref[...], wo_ref[...]) + bo_ref[...]


def _ffn_kernel(xp_ref, x_ref, xn_ref, g_ref, wu_ref, bu_ref, cw_ref, cb_ref, wd_ref, bd_ref, fg_ref,
                out_ref, hb_ref, u0_ref, u1_ref, u2_ref, u3_ref, acc_ref, *, tm, sub, seq, d_ff, cn, final_norm):
    u_bufs = ((u0_ref, u1_ref), (u2_ref, u3_ref))
    tiles = seq // tm
    t = pl.program_id(0) % tiles
    hl = FFN_HALO
    g = g_ref[...]
    hb_ref[...] = jnp.concatenate([_rms(xp_ref[...], g), _rms(x_ref[...], g), _rms(xn_ref[...], g)],
                                  axis=0).astype(BF16)
    n_sub = tm // sub
    rows = sub + 2 * hl
    n_chunks = d_ff // cn

    def conv(u_ref, ue, cols, a):
        bias = bu_ref[:, cols]
        c0, c1, c2 = cw_ref[0:1, cols], cw_ref[1:2, cols], cw_ref[2:3, cols]
        u_ref[0, 1:1 + rows, :] = ue
        u_ref[1, hl - 1:hl - 1 + rows, :] = ue
        if a == 0:
            u_ref[0, hl:hl + 1, :] = jnp.where(t > 0, ue[hl - 1:hl, :], -bias)
        if a == n_sub - 1:
            u_ref[1, 2 * hl + sub - 1:2 * hl + sub, :] = jnp.where(t < tiles - 1, ue[hl + sub:hl + sub + 1, :], -bias)
        return (c0 * u_ref[0, hl:hl + sub, :] + c1 * ue[hl:hl + sub, :] + c2 * u_ref[1, 2 * hl:2 * hl + sub, :]
                + (bias * (c0 + c1 + c2) + cb_ref[:, cols]))

    def up(a, j):
        hb = hb_ref[a * sub:a * sub + rows, :]
        return (_dot(hb, wu_ref[:, j * cn:(j + 1) * cn]), _dot(hb, wu_ref[:, d_ff + j * cn:d_ff + (j + 1) * cn]))

    def down(a, j, act):
        part = _dot(act, wd_ref[j * cn:(j + 1) * cn, :])
        if j == 0:
            acc_ref[...] = part
        else:
            acc_ref[...] += part
        if j == n_chunks - 1:
            out_rows = slice(a * sub, (a + 1) * sub)
            y = x_ref[out_rows, :] + acc_ref[...] + bd_ref[...]
            if final_norm:
                y = _rms(y, fg_ref[...])
            out_ref[out_rows, :] = y

    items = [(a, j) for a in range(n_sub) for j in range(n_chunks)]
    ue = up(*items[0])
    act_prev = None
    for idx, (a, j) in enumerate(items):
        ue_next = up(*items[idx + 1]) if idx + 1 < len(items) else None
        if act_prev is not None:
            down(*items[idx - 1], act_prev)
        gate = conv(u_bufs[idx % 2][0], ue[0], slice(j * cn, (j + 1) * cn), a)
        val = conv(u_bufs[idx % 2][1], ue[1], slice(d_ff + j * cn, d_ff + (j + 1) * cn), a)
        act_prev = (gate * jax.nn.sigmoid(gate) * val).astype(BF16)
        ue = ue_next
    down(*items[-1], act_prev)


def _ffn(x, g, w_up, b_up, conv_w, conv_b, w_down, b_down, final_g, *, tm, sub, seq, cn, final_norm):
    t, d = x.shape
    d_ff = w_down.shape[0]
    hl = FFN_HALO
    blk_per_tile = tm // hl
    n_hl = t // hl
    row = pl.BlockSpec((tm, d), lambda i: (i, 0))
    return pl.pallas_call(
        functools.partial(_ffn_kernel, tm=tm, sub=sub, seq=seq, d_ff=d_ff, cn=cn, final_norm=final_norm),
        grid=(t // tm,),
        in_specs=[pl.BlockSpec((hl, d), lambda i: (jnp.maximum(i * blk_per_tile - 1, 0), 0)),
                  row,
                  pl.BlockSpec((hl, d), lambda i: (jnp.minimum((i + 1) * blk_per_tile, n_hl - 1), 0)),
                  _full((1, d)), _full(w_up.shape), _full((1, 2 * d_ff)), _full(conv_w.shape),
                  _full((1, 2 * d_ff)), _full(w_down.shape), _full((1, d)), _full((1, d))],
        out_specs=row,
        out_shape=jax.ShapeDtypeStruct((t, d), F32),
        scratch_shapes=([pltpu.VMEM((tm + 2 * hl, d), BF16)] + [pltpu.VMEM((2, sub + 3 * hl, cn), F32)] * 4
                        + [pltpu.VMEM((sub, d), F32)]),
        compiler_params=_cparams(1),
        name="conv_ffn",
    )(x, x, x, g, w_up, b_up, conv_w, conv_b, w_down, b_down, final_g)


def _block_diag(blocks):
    n, a, b = blocks.shape
    eye = jnp.eye(n, dtype=blocks.dtype)
    return (eye[:, None, :, None] * blocks[:, :, None, :]).reshape(n * a, n * b)


def kernel(x, mem, rel_table, mem_norm_g, norm_mix_g, w_in, b_in, gmlp_v_g, gmlp_w_s, gmlp_b_s, pool_w, pool_b, pool_scale, w_out, b_out, norm_mem_g, xattn_w_q, xattn_w_kv, xattn_w_o, xattn_b_o, norm_ffn_g, ffn_w_up, ffn_b_up, ffn_conv_w, ffn_conv_b, ffn_w_down, ffn_b_down, final_norm_g):
    batch, seq, d = x.shape
    mem_len = mem.shape[1]
    depth = w_in.shape[0]
    t = batch * seq
    assert all(w // (2 * dil) == RADIUS for w, dil in DILATED_CONFIGS)

    row1 = lambda a: a.reshape(1, -1)
    ones_bd = jnp.asarray(np.kron(np.eye(A_HEADS), np.ones((HEAD_DIM, HEAD_DIM))), BF16)
    biases = [_attn_bias(rel_table, g, dil) for g, (_, dil) in enumerate(DILATED_CONFIGS)]
    kv = _kv_proj(mem.reshape(batch * mem_len, d), row1(mem_norm_g), xattn_w_kv.astype(BF16))

    xf = x.reshape(t, d)
    for l in range(depth):
        ws_cat = (gmlp_w_s[l].reshape(A_HEADS // 2, 2, GMLP_CHUNK, GMLP_CHUNK)
                  .transpose(0, 2, 1, 3).reshape(A_HEADS // 2, GMLP_CHUNK, 2 * GMLP_CHUNK).astype(BF16))
        bs_full = jnp.repeat(gmlp_b_s[l].T, HEAD_DIM, axis=1)
        ya, zb, *qkv = _mix_in(xf, row1(norm_mix_g[l]), w_in[l].astype(BF16), row1(b_in[l]),
                               row1(gmlp_v_g[l]), ws_cat, bs_full, ones_bd, tm=512)
        o, lse = [], []
        for g, (_, dil) in enumerate(DILATED_CONFIGS):
            og, lg = _attn(qkv[g], biases[g], dilation=dil, batch=batch, seq=seq, lq=1024 // min(dil, 4),
                           n_sub=4)
            o.append(og)
            lse.append(lg)
        xf = _mix_out(xf, ya, zb, o, lse, _block_diag(pool_w[l]).astype(BF16), row1(pool_b[l]),
                      row1(pool_scale[l]), w_out[l].astype(BF16), row1(b_out[l]),
                      row1(norm_mem_g[l]), xattn_w_q[l].astype(BF16), kv, l,
                      xattn_w_o[l].astype(BF16), row1(xattn_b_o[l]), tm=512, seq=seq, mem_len=mem_len)
        xf = _ffn(xf, row1(norm_ffn_g[l]), ffn_w_up[l].astype(BF16), row1(ffn_b_up[l]), ffn_conv_w[l],
                  row1(ffn_conv_b[l]), ffn_w_down[l].astype(BF16), row1(ffn_b_down[l]), row1(final_norm_g),
                  tm=512, sub=256, seq=seq, cn=256, final_norm=(l == depth - 1))
    return xf.reshape(batch, seq, d)
```

```python
import functools
import math

import numpy as np
import jax
import jax.numpy as jnp
from jax import lax
from jax.experimental import pallas as pl
from jax.experimental.pallas import tpu as pltpu

F32 = jnp.float32
BF16 = jnp.bfloat16

HEAD_DIM = 64
A_HEADS = 6
A_WIDTH = A_HEADS * HEAD_DIM
GMLP_CHUNK = 128
POOL_WINDOWS = (2, 4, 8, 16)
B_GROUP_DIM = 64
B_WIDTH = len(POOL_WINDOWS) * B_GROUP_DIM
DILATED_CONFIGS = ((128, 1), (512, 4), (2048, 16))
C_GROUPS = len(DILATED_CONFIGS)
C_GROUP_WIDTH = 2 * HEAD_DIM
C_WIDTH = C_GROUPS * C_GROUP_WIDTH
RADIUS = 64
N_BUCKETS = 32
MAX_DISTANCE = 1024
X_HEADS = 4
EPS = 1e-6
NEG_INF = -1e30

LANES = 128
POOL_HALO = 8
FFN_HALO = 8
VMEM_LIMIT = 56 * 1024 * 1024


def _cparams(n_axes, semantics="parallel"):
    return pltpu.CompilerParams(dimension_semantics=(semantics,) * n_axes, vmem_limit_bytes=VMEM_LIMIT)


def _layer_block(stacked, layer):
    return pl.BlockSpec((None,) + stacked.shape[1:], lambda *_: (layer, 0, 0))


def _rms(x, g):
    ms = jnp.mean(x * x, axis=-1, keepdims=True)
    return x * lax.rsqrt(ms + EPS) * g


def _gelu_tanh(x):
    return x * (0.5 * (1.0 + jnp.tanh(math.sqrt(2.0 / math.pi) * (x + 0.044715 * (x * x * x)))))


def _dot(a, b):
    return jnp.dot(a, b, preferred_element_type=F32)


def _cast_once(src_ref, dst_ref):
    @pl.when(pl.program_id(0) == 0)
    def _():
        dst_ref[...] = src_ref[...].astype(BF16)


def _dot_nt(a, b):
    return lax.dot_general(a, b, (((1,), (1,)), ((), ())), preferred_element_type=F32)


def _full(shape):
    return pl.BlockSpec(shape, lambda *_: (0,) * len(shape))


def _mix_in_kernel(x_ref, g_ref, w_ref, b_ref, vg_ref, ws_ref, bs_ref, ones_ref,
                   ya_ref, zb_ref, g0_ref, g1_ref, g2_ref, qkv_ref, wb_ref, *, tm):
    _cast_once(w_ref, wb_ref)
    hb = _rms(x_ref[...], g_ref[...]).astype(BF16)

    def seg(a, b):
        return _dot(hb, wb_ref[:, a:b]) + b_ref[:, a:b]

    uv = _gelu_tanh(seg(0, 2 * A_WIDTH))
    u, v = uv[:, :A_WIDTH], uv[:, A_WIDTH:]
    ssum = _dot((v * v).astype(BF16), ones_ref[...])
    vn = v * lax.rsqrt(ssum * (1.0 / HEAD_DIM) + EPS) * vg_ref[...]

    lane = lax.broadcasted_iota(jnp.int32, (GMLP_CHUNK, LANES), 1)
    first_head = lane < HEAD_DIM
    for c in range(tm // GMLP_CHUNK):
        rows = slice(c * GMLP_CHUNK, (c + 1) * GMLP_CHUNK)
        for j in range(A_WIDTH // LANES):
            cols = slice(j * LANES, (j + 1) * LANES)
            vp = vn[rows, cols]
            rhs = jnp.concatenate([jnp.where(first_head, vp, 0.0).astype(BF16),
                                   jnp.where(first_head, 0.0, vp).astype(BF16)], axis=0)
            s = _dot(ws_ref[j], rhs) + bs_ref[:, cols]
            ya_ref[rows, cols] = (u[rows, cols] * s).astype(BF16)

    off = 2 * A_WIDTH
    zb_ref[...] = seg(off, off + B_WIDTH)
    off += B_WIDTH
    qk = seg(off, off + 2 * C_WIDTH)
    parts = (qk[:, :C_WIDTH] * (HEAD_DIM ** -0.5), qk[:, C_WIDTH:], seg(off + 2 * C_WIDTH, off + 3 * C_WIDTH))
    for g, (out_ref, (_, dil)) in enumerate(zip((g0_ref, g1_ref, g2_ref), DILATED_CONFIGS)):
        for part in range(3):
            tile = parts[part][:, g * C_GROUP_WIDTH:(g + 1) * C_GROUP_WIDTH]
            if dil == 1:
                out_ref[part] = tile.astype(BF16)
                continue
            qkv_ref[part, g] = tile
            for r in range(dil):
                piece = qkv_ref[part, g, pl.ds(r, tm // dil, stride=dil), :]
                out_ref[part, :, r * C_GROUP_WIDTH:(r + 1) * C_GROUP_WIDTH] = piece.astype(BF16)


def _mix_in(x, g, w_in, layer, b_in, v_gain, ws_cat, bs_full, ones_bd, *, tm):
    t, d = x.shape
    in_width = w_in.shape[2]
    qkv_shapes = [jax.ShapeDtypeStruct((3, t // dil, dil * C_GROUP_WIDTH), BF16) for _, dil in DILATED_CONFIGS]
    qkv_specs = [pl.BlockSpec((3, tm // dil, dil * C_GROUP_WIDTH), lambda i: (0, i, 0))
                 for _, dil in DILATED_CONFIGS]
    return pl.pallas_call(
        functools.partial(_mix_in_kernel, tm=tm),
        grid=(t // tm,),
        in_specs=[pl.BlockSpec((tm, d), lambda i: (i, 0)),
                  _full((1, d)), _layer_block(w_in, layer), _full((1, in_width)), _full((1, A_WIDTH)),
                  _full(ws_cat.shape), _full(bs_full.shape), _full(ones_bd.shape)],
        out_specs=[pl.BlockSpec((tm, A_WIDTH), lambda i: (i, 0)),
                   pl.BlockSpec((tm, B_WIDTH), lambda i: (i, 0))] + qkv_specs,
        out_shape=[jax.ShapeDtypeStruct((t, A_WIDTH), BF16),
                   jax.ShapeDtypeStruct((t, B_WIDTH), F32)] + qkv_shapes,
        scratch_shapes=[pltpu.VMEM((3, C_GROUPS, tm, C_GROUP_WIDTH), F32), pltpu.VMEM((d, in_width), BF16)],
        compiler_params=_cparams(1, "arbitrary"),
        name="mix_in",
    )(x, g, w_in, b_in, v_gain, ws_cat, bs_full, ones_bd)


def _attn_kernel(q_ref, kp_ref, k_ref, kn_ref, vp_ref, v_ref, vn_ref, bias_ref,
                 o_ref, lse_ref, kx_ref, vx_ref, *, lq, n_blocks, n_sub):
    i = pl.program_id(2)
    kx_ref[0:RADIUS] = kp_ref[...]
    kx_ref[RADIUS:RADIUS + lq] = k_ref[...]
    kx_ref[RADIUS + lq:] = kn_ref[...]
    vx_ref[0:RADIUS] = vp_ref[...]
    vx_ref[RADIUS:RADIUS + lq] = v_ref[...]
    vx_ref[RADIUS + lq:] = vn_ref[...]

    qc = 2 * RADIUS
    kc = qc + 2 * RADIUS
    n_chunks = lq // qc
    first_head = lax.broadcasted_iota(jnp.int32, (qc, LANES), 1) < HEAD_DIM
    for c in range(n_chunks):
        rows = slice(c * qc, (c + 1) * qc)
        variant = 1
        if c == 0:
            variant = jnp.where(i == 0, 0, 1)
        if c == n_chunks - 1:
            variant = jnp.where(i == n_blocks - 1, 2, variant)
        for r in range(n_sub):
            lanes = slice(r * LANES, (r + 1) * LANES)
            q = q_ref[rows, lanes]
            keys = kx_ref[c * qc:c * qc + kc, lanes]
            vals = vx_ref[c * qc:c * qc + kc, lanes]
            o_heads, lse_heads = [], []
            for h in range(2):
                qh = jnp.where(first_head if h == 0 else ~first_head, q, jnp.zeros_like(q))
                s = _dot_nt(qh, keys) + bias_ref[variant, h]
                m = jnp.max(s, axis=-1, keepdims=True)
                p = jnp.exp(s - m)
                l = jnp.sum(p, axis=-1, keepdims=True)
                o_heads.append(_dot(p.astype(BF16), vals) / l)
                lse_heads.append(jnp.broadcast_to(m + jnp.log(l), (qc, LANES)))
            o_ref[rows, lanes] = jnp.where(first_head, o_heads[0], o_heads[1])
            lse_ref[rows, lanes] = jnp.where(first_head, lse_heads[0], lse_heads[1])


def _attn(qkv, bias, *, dilation, batch, seq, lq, n_sub):
    sub_len = seq // dilation
    lq = min(lq, sub_len)
    n_sub = min(n_sub, dilation)
    n_blocks = sub_len // lq
    assert sub_len >= 4 * RADIUS
    qkv = qkv.reshape(3, batch, sub_len, dilation * C_GROUP_WIDTH)
    halo_per_blk = lq // RADIUS
    n_halo = sub_len // RADIUS
    width = n_sub * C_GROUP_WIDTH

    def main(part):
        return pl.BlockSpec((None, None, lq, width), lambda b, r, i: (part, b, i, r))

    def prev(part):
        return pl.BlockSpec((None, None, RADIUS, width),
                            lambda b, r, i: (part, b, jnp.maximum(i * halo_per_blk - 1, 0), r))

    def nxt(part):
        return pl.BlockSpec((None, None, RADIUS, width),
                            lambda b, r, i: (part, b, jnp.minimum((i + 1) * halo_per_blk, n_halo - 1), r))

    out_spec = pl.BlockSpec((None, lq, width), lambda b, r, i: (b, i, r))
    out_shape = jax.ShapeDtypeStruct((batch, sub_len, dilation * C_GROUP_WIDTH), F32)
    o, lse = pl.pallas_call(
        functools.partial(_attn_kernel, lq=lq, n_blocks=n_blocks, n_sub=n_sub),
        grid=(batch, dilation // n_sub, n_blocks),
        in_specs=[main(0), prev(1), main(1), nxt(1), prev(2), main(2), nxt(2), _full(bias.shape)],
        out_specs=[out_spec, out_spec],
        out_shape=[out_shape, out_shape],
        scratch_shapes=[pltpu.VMEM((lq + 2 * RADIUS, width), BF16),
                        pltpu.VMEM((lq + 2 * RADIUS, width), BF16)],
        compiler_params=_cparams(3),
        name=f"dilated_attn_d{dilation}",
    )(qkv, qkv, qkv, qkv, qkv, qkv, qkv, bias)
    rows = batch * sub_len
    return o.reshape(rows, dilation * C_GROUP_WIDTH), lse.reshape(rows, dilation * C_GROUP_WIDTH)


def _t5_bucket(rel):
    nb = N_BUCKETS // 2
    ret = (rel > 0).astype(np.int32) * nb
    n = np.abs(rel)
    max_exact = nb // 2
    large = max_exact + (np.log(np.maximum(n, 1) / max_exact)
                         / math.log(MAX_DISTANCE / max_exact) * (nb - max_exact)).astype(np.int32)
    large = np.minimum(large, nb - 1)
    return ret + np.where(n < max_exact, n, large)


def _attn_bias(rel_table, group, dilation):
    qc, kc = 2 * RADIUS, 4 * RADIUS
    n = qc + kc - 1
    delta = (np.arange(n) + qc - 1) % n - (qc + RADIUS - 1)
    onehot = (_t5_bucket(delta * dilation)[:, None] == np.arange(N_BUCKETS)[None, :]).astype(np.float32)
    per_delta = jnp.dot(onehot, rel_table[:, 2 * group:2 * group + 2], precision=lax.Precision.HIGHEST)
    per_delta = jnp.where((np.abs(delta) <= RADIUS)[:, None], per_delta, NEG_INF).T
    reps = -(-(qc * (n - 1)) // n)
    band = jnp.tile(per_delta, (1, reps))[:, :qc * (n - 1)].reshape(2, qc, n - 1)[:, :, :kc]
    key = np.arange(kc)[None, None, :]
    return jnp.stack([jnp.where(key < RADIUS, NEG_INF, band), band, jnp.where(key >= kc - RADIUS, NEG_INF, band)])


def _mix_out_kernel(x_ref, ya_ref, zp_ref, zb_ref, zn_ref, o0_ref, o1_ref, o2_ref, l0_ref, l1_ref, l2_ref,
                    wp_ref, bp_ref, ps_ref, wo_ref, bo_ref, xg_ref, qk_ref, vo_ref, xbo_ref,
                    out_ref, zx_ref, nat_ref, xp_ref, wob_ref, *, tm, sub, seq):
    _cast_once(wo_ref, wob_ref)
    tiles = seq // tm
    t = pl.program_id(0) % tiles
    hl = POOL_HALO
    zx_ref[0:hl] = jnp.where(t > 0, zp_ref[...], 0.0)
    zx_ref[hl:hl + tm] = zb_ref[...]
    zx_ref[hl + tm:] = jnp.where(t < tiles - 1, zn_ref[...], 0.0)

    dils = [dil for _, dil in DILATED_CONFIGS] * 2
    grp_refs = (o0_ref, o1_ref, o2_ref, l0_ref, l1_ref, l2_ref)
    for slot, (src_ref, dil) in enumerate(zip(grp_refs, dils)):
        for r in range(dil if dil > 1 else 0):
            nat_ref[slot, pl.ds(r, tm // dil, stride=dil), :] = src_ref[:, r * C_GROUP_WIDTH:(r + 1) * C_GROUP_WIDTH]

    def natural(slot, rows):
        return grp_refs[slot][rows, :] if dils[slot] == 1 else nat_ref[slot, rows, :]

    lane = lax.broadcasted_iota(jnp.int32, (sub, LANES), 1)
    first_group = lane < B_GROUP_DIM
    for s in range(tm // sub):
        rows = slice(s * sub, (s + 1) * sub)
        pos = t * tm + s * sub + lax.broadcasted_iota(jnp.int32, (sub, LANES), 0)
        pooled = []
        for cb in range(B_WIDTH // LANES):
            cols = slice(cb * LANES, (cb + 1) * LANES)
            half_a, half_b = POOL_WINDOWS[2 * cb] // 2, POOL_WINDOWS[2 * cb + 1] // 2

            def shifted(j):
                return zx_ref[hl + s * sub + j:hl + s * sub + j + sub, cols]

            acc = shifted(-1) + shifted(0)
            sums = {1: acc}
            for half in (2, 4, 8):
                if half > half_b:
                    break
                for j in range(half // 2, half):
                    acc = acc + shifted(-j - 1) + shifted(j)
                sums[half] = acc
            half = jnp.where(first_group, half_a, half_b)
            cnt = (jnp.minimum(pos + half, seq) - jnp.maximum(pos - half, 0)).astype(F32)
            wsum = jnp.where(first_group, sums[half_a], sums[half_b])
            pooled.append(wsum / cnt - zb_ref[rows, cols])
        pooled = jnp.concatenate(pooled, axis=-1).astype(BF16)
        yb = (_dot(pooled, wp_ref[...]) + bp_ref[...]) * ps_ref[...]

        l0, l1, l2 = natural(3, rows), natural(4, rows), natural(5, rows)
        m = jnp.maximum(l0, jnp.maximum(l1, l2))
        e0, e1, e2 = jnp.exp(l0 - m), jnp.exp(l1 - m), jnp.exp(l2 - m)
        yc = (e0 * natural(0, rows) + e1 * natural(1, rows) + e2 * natural(2, rows)) / (e0 + e1 + e2)

        y = (_dot(ya_ref[rows, :], wob_ref[0:A_WIDTH, :])
             + _dot(yb.astype(BF16), wob_ref[A_WIDTH:A_WIDTH + B_WIDTH, :])
             + _dot(yc.astype(BF16), wob_ref[A_WIDTH + B_WIDTH:, :]))
        x1 = x_ref[rows, :] + y + bo_ref[...]
        out_ref[rows, :] = _cross_attend(x1, xg_ref, qk_ref, vo_ref, xbo_ref, xp_ref.at[s])


def _mix_out(x, ya, zb, o, lse, wp_bd, bp, ps, w_out, b_out, xg, qk, vo, layer, xb_o, *, tm, sub, seq):
    t, d = x.shape
    tiles = seq // tm
    hm = qk.shape[-1]
    hl = POOL_HALO
    blk_per_tile = tm // hl
    n_hl = t // hl
    row = lambda w: pl.BlockSpec((tm, w), lambda i: (i, 0))
    grp = [pl.BlockSpec((tm // dil, dil * C_GROUP_WIDTH), lambda i: (i, 0)) for _, dil in DILATED_CONFIGS]
    return pl.pallas_call(
        functools.partial(_mix_out_kernel, tm=tm, sub=sub, seq=seq),
        grid=(t // tm,),
        in_specs=[row(d), row(A_WIDTH),
                  pl.BlockSpec((hl, B_WIDTH), lambda i: (jnp.maximum(i * blk_per_tile - 1, 0), 0)),
                  row(B_WIDTH),
                  pl.BlockSpec((hl, B_WIDTH), lambda i: (jnp.minimum((i + 1) * blk_per_tile, n_hl - 1), 0)),
                  *grp, *grp,
                  _full(wp_bd.shape), _full((1, B_WIDTH)), _full((1, B_WIDTH)),
                  _layer_block(w_out, layer), _full((1, d)),
                  _full((1, d)),
                  pl.BlockSpec((None, None, d, hm), lambda i: (layer, i // tiles, 0, 0)),
                  pl.BlockSpec((None, None, hm, d), lambda i: (layer, i // tiles, 0, 0)),
                  _full((1, d))],
        out_specs=row(d),
        out_shape=jax.ShapeDtypeStruct((t, d), F32),
        scratch_shapes=[pltpu.VMEM((tm + 2 * hl, B_WIDTH), F32),
                        pltpu.VMEM((2 * C_GROUPS, tm, C_GROUP_WIDTH), F32),
                        pltpu.VMEM((tm // sub, sub, hm), BF16),
                        pltpu.VMEM(w_out.shape[1:], BF16)],
        compiler_params=_cparams(1, "arbitrary"),
        name="mix_out_xattn",
    )(x, ya, zb, zb, zb, o[0], o[1], o[2], lse[0], lse[1], lse[2], wp_bd, bp, ps, w_out, b_out,
      xg, qk, vo, xb_o)


def _mem_fold_kernel(mem_ref, g_ref, wkv_ref, wq_ref, wo_ref, qk_ref, vo_ref):
    d = mem_ref.shape[1]
    m = mem_ref.shape[0]
    hd = d // X_HEADS
    mem_n = _rms(mem_ref[...], g_ref[...]).astype(BF16)
    kv = _dot(mem_n, wkv_ref[...].astype(BF16)).astype(BF16)
    for h in range(X_HEADS):
        cols = slice(h * hd, (h + 1) * hd)
        wq_h = wq_ref[:, cols].astype(BF16)
        qk_ref[:, h * m:(h + 1) * m] = (_dot_nt(wq_h, kv[:, cols]) * (hd ** -0.5)).astype(BF16)
        wo_h = wo_ref[cols, :].astype(BF16)
        vo_ref[h * m:(h + 1) * m, :] = _dot(kv[:, d + h * hd:d + (h + 1) * hd], wo_h).astype(BF16)


def _mem_fold(mem, g, w_kv, w_q, w_o):
    batch, m, d = mem.shape
    depth = w_kv.shape[0]

    def per_layer(stack):
        return pl.BlockSpec((None,) + stack.shape[1:], lambda l, b: (l, 0, 0))

    return pl.pallas_call(
        _mem_fold_kernel,
        grid=(depth, batch),
        in_specs=[pl.BlockSpec((None, m, d), lambda l, b: (b, 0, 0)), pl.BlockSpec((1, d), lambda l, b: (0, 0)),
                  per_layer(w_kv), per_layer(w_q), per_layer(w_o)],
        out_specs=[pl.BlockSpec((None, None, d, X_HEADS * m), lambda l, b: (l, b, 0, 0)),
                   pl.BlockSpec((None, None, X_HEADS * m, d), lambda l, b: (l, b, 0, 0))],
        out_shape=[jax.ShapeDtypeStruct((depth, batch, d, X_HEADS * m), BF16),
                   jax.ShapeDtypeStruct((depth, batch, X_HEADS * m, d), BF16)],
        compiler_params=_cparams(2),
        name="mem_fold",
    )(mem, g, w_kv, w_q, w_o)


def _cross_attend(x, g_ref, qk_ref, vo_ref, bo_ref, p_ref):
    hb = _rms(x, g_ref[...]).astype(BF16)
    s_all = _dot(hb, qk_ref[...])
    m_len = qk_ref.shape[1] // X_HEADS
    for h in range(X_HEADS):
        cols = slice(h * m_len, (h + 1) * m_len)
        s = s_all[:, cols]
        m = jnp.max(s, axis=-1, keepdims=True)
        p = jnp.exp(s - m)
        l = jnp.sum(p, axis=-1, keepdims=True)
        p_ref[:, cols] = (p / l).astype(BF16)
    return x + _dot(p_ref[...], vo_ref[...]) + bo_ref[...]


def _ffn_kernel(xp_ref, x_ref, xn_ref, g_ref, wu_ref, bu_ref, cw_ref, cb_ref, wd_ref, bd_ref, fg_ref,
                out_ref, hb_ref, u0_ref, u1_ref, u2_ref, u3_ref, acc_ref, wdb_ref, *, tm, sub, seq, d_ff, cn,
                final_norm):
    _cast_once(wd_ref, wdb_ref)
    u_bufs = ((u0_ref, u1_ref), (u2_ref, u3_ref))
    tiles = seq // tm
    t = pl.program_id(0) % tiles
    hl = FFN_HALO
    g = g_ref[...]
    hb_ref[...] = jnp.concatenate([_rms(xp_ref[...], g), _rms(x_ref[...], g), _rms(xn_ref[...], g)],
                                  axis=0).astype(BF16)
    n_sub = tm // sub
    rows = sub + 2 * hl
    n_chunks = d_ff // cn

    def conv(u_ref, ue, cols, a):
        bias = bu_ref[:, cols]
        c0, c1, c2 = cw_ref[0:1, cols], cw_ref[1:2, cols], cw_ref[2:3, cols]
        u_ref[0, 1:1 + rows, :] = ue
        u_ref[1, hl - 1:hl - 1 + rows, :] = ue
        if a == 0:
            u_ref[0, hl:hl + 1, :] = jnp.where(t > 0, ue[hl - 1:hl, :], -bias)
        if a == n_sub - 1:
            u_ref[1, 2 * hl + sub - 1:2 * hl + sub, :] = jnp.where(t < tiles - 1, ue[hl + sub:hl + sub + 1, :], -bias)
        return (c0 * u_ref[0, hl:hl + sub, :] + c1 * ue[hl:hl + sub, :] + c2 * u_ref[1, 2 * hl:2 * hl + sub, :]
                + (bias * (c0 + c1 + c2) + cb_ref[:, cols]))

    def up(a, j):
        hb = hb_ref[a * sub:a * sub + rows, :]
        return (_dot(hb, wu_ref[:, j * cn:(j + 1) * cn]), _dot(hb, wu_ref[:, d_ff + j * cn:d_ff + (j + 1) * cn]))

    def down(a, j, act):
        part = _dot(act, wdb_ref[j * cn:(j + 1) * cn, :])
        if j == 0:
            acc_ref[...] = part
        else:
            acc_ref[...] += part
        if j == n_chunks - 1:
            out_rows = slice(a * sub, (a + 1) * sub)
            y = x_ref[out_rows, :] + acc_ref[...] + bd_ref[...]
            if final_norm:
                y = _rms(y, fg_ref[...])
            out_ref[out_rows, :] = y

    items = [(a, j) for a in range(n_sub) for j in range(n_chunks)]
    ue = up(*items[0])
    act_prev = None
    for idx, (a, j) in enumerate(items):
        ue_next = up(*items[idx + 1]) if idx + 1 < len(items) else None
        if act_prev is not None:
            down(*items[idx - 1], act_prev)
        gate = conv(u_bufs[idx % 2][0], ue[0], slice(j * cn, (j + 1) * cn), a)
        val = conv(u_bufs[idx % 2][1], ue[1], slice(d_ff + j * cn, d_ff + (j + 1) * cn), a)
        act_prev = (gate * jax.nn.sigmoid(gate) * val).astype(BF16)
        ue = ue_next
    down(*items[-1], act_prev)


def _ffn(x, g, w_up, layer, b_up, conv_w, conv_b, w_down, b_down, final_g, *, tm, sub, seq, cn, final_norm):
    t, d = x.shape
    d_ff = w_down.shape[1]
    hl = FFN_HALO
    blk_per_tile = tm // hl
    n_hl = t // hl
    row = pl.BlockSpec((tm, d), lambda i: (i, 0))
    return pl.pallas_call(
        functools.partial(_ffn_kernel, tm=tm, sub=sub, seq=seq, d_ff=d_ff, cn=cn, final_norm=final_norm),
        grid=(t // tm,),
        in_specs=[pl.BlockSpec((hl, d), lambda i: (jnp.maximum(i * blk_per_tile - 1, 0), 0)),
                  row,
                  pl.BlockSpec((hl, d), lambda i: (jnp.minimum((i + 1) * blk_per_tile, n_hl - 1), 0)),
                  _full((1, d)), _layer_block(w_up, layer), _full((1, 2 * d_ff)), _full(conv_w.shape),
                  _full((1, 2 * d_ff)), _layer_block(w_down, layer), _full((1, d)), _full((1, d))],
        out_specs=row,
        out_shape=jax.ShapeDtypeStruct((t, d), F32),
        scratch_shapes=([pltpu.VMEM((tm + 2 * hl, d), BF16)] + [pltpu.VMEM((2, sub + 3 * hl, cn), F32)] * 4
                        + [pltpu.VMEM((sub, d), F32), pltpu.VMEM((d_ff, d), BF16)]),
        compiler_params=_cparams(1, "arbitrary"),
        name="conv_ffn",
    )(x, x, x, g, w_up, b_up, conv_w, conv_b, w_down, b_down, final_g)


def _block_diag(blocks):
    n, a, b = blocks.shape
    eye = jnp.eye(n, dtype=blocks.dtype)
    return (eye[:, None, :, None] * blocks[:, :, None, :]).reshape(n * a, n * b)


def kernel(x, mem, rel_table, mem_norm_g, norm_mix_g, w_in, b_in, gmlp_v_g, gmlp_w_s, gmlp_b_s, pool_w, pool_b, pool_scale, w_out, b_out, norm_mem_g, xattn_w_q, xattn_w_kv, xattn_w_o, xattn_b_o, norm_ffn_g, ffn_w_up, ffn_b_up, ffn_conv_w, ffn_conv_b, ffn_w_down, ffn_b_down, final_norm_g):
    batch, seq, d = x.shape
    mem_len = mem.shape[1]
    depth = w_in.shape[0]
    t = batch * seq
    assert all(w // (2 * dil) == RADIUS for w, dil in DILATED_CONFIGS)

    row1 = lambda a: a.reshape(1, -1)
    ones_bd = jnp.asarray(np.kron(np.eye(A_HEADS), np.ones((HEAD_DIM, HEAD_DIM))), BF16)
    biases = [_attn_bias(rel_table, g, dil) for g, (_, dil) in enumerate(DILATED_CONFIGS)]
    qk, vo = _mem_fold(mem, row1(mem_norm_g), xattn_w_kv, xattn_w_q, xattn_w_o)
    w_up_bf16 = ffn_w_up.astype(BF16)

    xf = x.reshape(t, d)
    for l in range(depth):
        ws_cat = (gmlp_w_s[l].reshape(A_HEADS // 2, 2, GMLP_CHUNK, GMLP_CHUNK)
                  .transpose(0, 2, 1, 3).reshape(A_HEADS // 2, GMLP_CHUNK, 2 * GMLP_CHUNK).astype(BF16))
        bs_full = jnp.repeat(gmlp_b_s[l].T, HEAD_DIM, axis=1)
        ya, zb, *qkv = _mix_in(xf, row1(norm_mix_g[l]), w_in, l, row1(b_in[l]),
                               row1(gmlp_v_g[l]), ws_cat, bs_full, ones_bd, tm=1024)
        o, lse = [], []
        for g, (_, dil) in enumerate(DILATED_CONFIGS):
            og, lg = _attn(qkv[g], biases[g], dilation=dil, batch=batch, seq=seq, lq=2048 // min(dil, 8),
                           n_sub=8)
            o.append(og)
            lse.append(lg)
        xf = _mix_out(xf, ya, zb, o, lse, _block_diag(pool_w[l]).astype(BF16), row1(pool_b[l]),
                      row1(pool_scale[l]), w_out, row1(b_out[l]),
                      row1(norm_mem_g[l]), qk, vo, l, row1(xattn_b_o[l]), tm=512, sub=512, seq=seq)
        xf = _ffn(xf, row1(norm_ffn_g[l]), w_up_bf16, l, row1(ffn_b_up[l]), ffn_conv_w[l],
                  row1(ffn_conv_b[l]), ffn_w_down, row1(ffn_b_down[l]), row1(final_norm_g),
                  tm=512, sub=256, seq=seq, cn=256, final_norm=(l == depth - 1))
    return xf.reshape(batch, seq, d)
```

```python
import functools
import math

import numpy as np
import jax
import jax.numpy as jnp
from jax import lax
from jax.experimental import pallas as pl
from jax.experimental.pallas import tpu as pltpu

F32 = jnp.float32
BF16 = jnp.bfloat16

HEAD_DIM = 64
A_HEADS = 6
A_WIDTH = A_HEADS * HEAD_DIM
GMLP_CHUNK = 128
POOL_WINDOWS = (2, 4, 8, 16)
B_GROUP_DIM = 64
B_WIDTH = len(POOL_WINDOWS) * B_GROUP_DIM
DILATED_CONFIGS = ((128, 1), (512, 4), (2048, 16))
C_GROUPS = len(DILATED_CONFIGS)
C_GROUP_WIDTH = 2 * HEAD_DIM
C_WIDTH = C_GROUPS * C_GROUP_WIDTH
RADIUS = 64
N_BUCKETS = 32
MAX_DISTANCE = 1024
X_HEADS = 4
EPS = 1e-6
NEG_INF = -1e30

LANES = 128
POOL_HALO = 8
FFN_HALO = 8
VMEM_LIMIT = 56 * 1024 * 1024


def _cparams(n_axes, semantics="parallel"):
    return pltpu.CompilerParams(dimension_semantics=(semantics,) * n_axes, vmem_limit_bytes=VMEM_LIMIT)


def _layer_block(stacked, layer):
    return pl.BlockSpec((None,) + stacked.shape[1:], lambda *_: (layer, 0, 0))


def _rms(x, g):
    ms = jnp.mean(x * x, axis=-1, keepdims=True)
    return x * lax.rsqrt(ms + EPS) * g


def _gelu_tanh(x):
    return x * (0.5 * (1.0 + jnp.tanh(math.sqrt(2.0 / math.pi) * (x + 0.044715 * (x * x * x)))))


def _dot(a, b):
    return jnp.dot(a, b, preferred_element_type=F32)


def _cast_once(src_ref, dst_ref):
    @pl.when(pl.program_id(0) == 0)
    def _():
        dst_ref[...] = src_ref[...].astype(BF16)


def _dot_nt(a, b):
    return lax.dot_general(a, b, (((1,), (1,)), ((), ())), preferred_element_type=F32)


def _full(shape):
    return pl.BlockSpec(shape, lambda *_: (0,) * len(shape))


def _pool_minus_identity(zx_ref, t, tm, seq):
    hl = POOL_HALO
    pos = t * tm + lax.broadcasted_iota(jnp.int32, (tm, LANES), 0)
    first_group = lax.broadcasted_iota(jnp.int32, (tm, LANES), 1) < B_GROUP_DIM
    pooled = []
    for cb in range(B_WIDTH // LANES):
        cols = slice(cb * LANES, (cb + 1) * LANES)
        half_a, half_b = POOL_WINDOWS[2 * cb] // 2, POOL_WINDOWS[2 * cb + 1] // 2

        def shifted(j):
            return zx_ref[hl + j:hl + j + tm, cols]

        acc = shifted(-1) + shifted(0)
        sums = {1: acc}
        for half in (2, 4, 8):
            if half > half_b:
                break
            for j in range(half // 2, half):
                acc = acc + shifted(-j - 1) + shifted(j)
            sums[half] = acc
        half = jnp.where(first_group, half_a, half_b)
        cnt = (jnp.minimum(pos + half, seq) - jnp.maximum(pos - half, 0)).astype(F32)
        wsum = jnp.where(first_group, sums[half_a], sums[half_b])
        pooled.append(wsum / cnt - shifted(0))
    return jnp.concatenate(pooled, axis=-1)


def _mix_in_kernel(xp_ref, x_ref, xn_ref, g_ref, w_ref, b_ref, vg_ref, ws_ref, bs_ref, ones_ref,
                   wp_ref, bp_ref, ps_ref,
                   ya_ref, yb_ref, g0_ref, g1_ref, g2_ref, qkv_ref, wb_ref, zx_ref, *, tm, seq):
    _cast_once(w_ref, wb_ref)
    tiles = seq // tm
    t = pl.program_id(0) % tiles
    g = g_ref[...]
    hb = _rms(x_ref[...], g).astype(BF16)

    def seg(a, b):
        return _dot(hb, wb_ref[:, a:b]) + b_ref[:, a:b]

    off_b = 2 * A_WIDTH
    off_c = off_b + B_WIDTH
    uv = seg(0, off_b)
    hl = POOL_HALO
    h_halo = jnp.concatenate([_rms(xp_ref[...], g), _rms(xn_ref[...], g)], axis=0).astype(BF16)
    z_halo = _dot(h_halo, wb_ref[:, off_b:off_c]) + b_ref[:, off_b:off_c]
    zx_ref[0:hl] = jnp.where(t > 0, z_halo[0:hl], 0.0)
    zx_ref[hl:hl + tm] = seg(off_b, off_c)
    zx_ref[hl + tm:] = jnp.where(t < tiles - 1, z_halo[hl:], 0.0)
    qk = seg(off_c, off_c + 2 * C_WIDTH)
    parts = (qk[:, :C_WIDTH] * (HEAD_DIM ** -0.5), qk[:, C_WIDTH:], seg(off_c + 2 * C_WIDTH, off_c + 3 * C_WIDTH))

    pooled = _pool_minus_identity(zx_ref, t, tm, seq).astype(BF16)
    yb_ref[...] = ((_dot(pooled, wp_ref[...]) + bp_ref[...]) * ps_ref[...]).astype(BF16)

    for grp, (out_ref, (_, dil)) in enumerate(zip((g0_ref, g1_ref, g2_ref), DILATED_CONFIGS)):
        for part in range(3):
            tile = parts[part][:, grp * C_GROUP_WIDTH:(grp + 1) * C_GROUP_WIDTH]
            if dil == 1:
                out_ref[part] = tile.astype(BF16)
                continue
            qkv_ref[part, grp] = tile
            for r in range(dil):
                piece = qkv_ref[part, grp, pl.ds(r, tm // dil, stride=dil), :]
                out_ref[part, :, r * C_GROUP_WIDTH:(r + 1) * C_GROUP_WIDTH] = piece.astype(BF16)

    uv = _gelu_tanh(uv)
    u, v = uv[:, :A_WIDTH], uv[:, A_WIDTH:]
    ssum = _dot((v * v).astype(BF16), ones_ref[...])
    vn = v * lax.rsqrt(ssum * (1.0 / HEAD_DIM) + EPS) * vg_ref[...]
    first_head = lax.broadcasted_iota(jnp.int32, (GMLP_CHUNK, LANES), 1) < HEAD_DIM
    for c in range(tm // GMLP_CHUNK):
        rows = slice(c * GMLP_CHUNK, (c + 1) * GMLP_CHUNK)
        for j in range(A_WIDTH // LANES):
            cols = slice(j * LANES, (j + 1) * LANES)
            vp = vn[rows, cols]
            rhs = jnp.concatenate([jnp.where(first_head, vp, 0.0).astype(BF16),
                                   jnp.where(first_head, 0.0, vp).astype(BF16)], axis=0)
            s = _dot(ws_ref[j], rhs) + bs_ref[:, cols]
            ya_ref[rows, cols] = (u[rows, cols] * s).astype(BF16)


def _mix_in(x, g, w_in, layer, b_in, v_gain, ws_cat, bs_full, ones_bd, wp_bd, bp, ps, *, tm, seq):
    t, d = x.shape
    in_width = w_in.shape[2]
    hl = POOL_HALO
    blk_per_tile = tm // hl
    n_hl = t // hl
    qkv_shapes = [jax.ShapeDtypeStruct((3, t // dil, dil * C_GROUP_WIDTH), BF16) for _, dil in DILATED_CONFIGS]
    qkv_specs = [pl.BlockSpec((3, tm // dil, dil * C_GROUP_WIDTH), lambda i: (0, i, 0))
                 for _, dil in DILATED_CONFIGS]
    return pl.pallas_call(
        functools.partial(_mix_in_kernel, tm=tm, seq=seq),
        grid=(t // tm,),
        in_specs=[pl.BlockSpec((hl, d), lambda i: (jnp.maximum(i * blk_per_tile - 1, 0), 0)),
                  pl.BlockSpec((tm, d), lambda i: (i, 0)),
                  pl.BlockSpec((hl, d), lambda i: (jnp.minimum((i + 1) * blk_per_tile, n_hl - 1), 0)),
                  _full((1, d)), _layer_block(w_in, layer), _full((1, in_width)), _full((1, A_WIDTH)),
                  _full(ws_cat.shape), _full(bs_full.shape), _full(ones_bd.shape),
                  _full(wp_bd.shape), _full((1, B_WIDTH)), _full((1, B_WIDTH))],
        out_specs=[pl.BlockSpec((tm, A_WIDTH), lambda i: (i, 0)),
                   pl.BlockSpec((tm, B_WIDTH), lambda i: (i, 0))] + qkv_specs,
        out_shape=[jax.ShapeDtypeStruct((t, A_WIDTH), BF16),
                   jax.ShapeDtypeStruct((t, B_WIDTH), BF16)] + qkv_shapes,
        scratch_shapes=[pltpu.VMEM((3, C_GROUPS, tm, C_GROUP_WIDTH), F32), pltpu.VMEM((d, in_width), BF16),
                        pltpu.VMEM((tm + 2 * hl, B_WIDTH), F32)],
        compiler_params=_cparams(1, "arbitrary"),
        name="mix_in",
    )(x, x, x, g, w_in, b_in, v_gain, ws_cat, bs_full, ones_bd, wp_bd, bp, ps)


def _attn_kernel(q_ref, kp_ref, k_ref, kn_ref, vp_ref, v_ref, vn_ref, bias_ref,
                 o_ref, lse_ref, kx_ref, vx_ref, *, lq, n_blocks, n_sub):
    i = pl.program_id(2)
    kx_ref[0:RADIUS] = kp_ref[...]
    kx_ref[RADIUS:RADIUS + lq] = k_ref[...]
    kx_ref[RADIUS + lq:] = kn_ref[...]
    vx_ref[0:RADIUS] = vp_ref[...]
    vx_ref[RADIUS:RADIUS + lq] = v_ref[...]
    vx_ref[RADIUS + lq:] = vn_ref[...]

    qc = 2 * RADIUS
    kc = qc + 2 * RADIUS
    n_chunks = lq // qc
    first_head = lax.broadcasted_iota(jnp.int32, (qc, LANES), 1) < HEAD_DIM
    for c in range(n_chunks):
        rows = slice(c * qc, (c + 1) * qc)
        variant = 1
        if c == 0:
            variant = jnp.where(i == 0, 0, 1)
        if c == n_chunks - 1:
            variant = jnp.where(i == n_blocks - 1, 2, variant)
        for r in range(n_sub):
            lanes = slice(r * LANES, (r + 1) * LANES)
            q = q_ref[rows, lanes]
            keys = kx_ref[c * qc:c * qc + kc, lanes]
            vals = vx_ref[c * qc:c * qc + kc, lanes]
            o_heads, lse_heads = [], []
            for h in range(2):
                qh = jnp.where(first_head if h == 0 else ~first_head, q, jnp.zeros_like(q))
                s = _dot_nt(qh, keys) + bias_ref[variant, h]
                m = jnp.max(s, axis=-1, keepdims=True)
                p = jnp.exp(s - m)
                l = jnp.sum(p, axis=-1, keepdims=True)
                o_heads.append(_dot(p.astype(BF16), vals) / l)
                lse_heads.append(jnp.broadcast_to(m + jnp.log(l), (qc, LANES)))
            o_ref[rows, lanes] = jnp.where(first_head, o_heads[0], o_heads[1])
            lse_ref[rows, lanes] = jnp.where(first_head, lse_heads[0], lse_heads[1])


def _attn(qkv, bias, *, dilation, batch, seq, lq, n_sub):
    sub_len = seq // dilation
    lq = min(lq, sub_len)
    n_sub = min(n_sub, dilation)
    n_blocks = sub_len // lq
    assert sub_len >= 4 * RADIUS
    qkv = qkv.reshape(3, batch, sub_len, dilation * C_GROUP_WIDTH)
    halo_per_blk = lq // RADIUS
    n_halo = sub_len // RADIUS
    width = n_sub * C_GROUP_WIDTH

    def main(part):
        return pl.BlockSpec((None, None, lq, width), lambda b, r, i: (part, b, i, r))

    def prev(part):
        return pl.BlockSpec((None, None, RADIUS, width),
                            lambda b, r, i: (part, b, jnp.maximum(i * halo_per_blk - 1, 0), r))

    def nxt(part):
        return pl.BlockSpec((None, None, RADIUS, width),
                            lambda b, r, i: (part, b, jnp.minimum((i + 1) * halo_per_blk, n_halo - 1), r))

    out_spec = pl.BlockSpec((None, lq, width), lambda b, r, i: (b, i, r))
    out_shape = jax.ShapeDtypeStruct((batch, sub_len, dilation * C_GROUP_WIDTH), F32)
    o, lse = pl.pallas_call(
        functools.partial(_attn_kernel, lq=lq, n_blocks=n_blocks, n_sub=n_sub),
        grid=(batch, dilation // n_sub, n_blocks),
        in_specs=[main(0), prev(1), main(1), nxt(1), prev(2), main(2), nxt(2), _full(bias.shape)],
        out_specs=[out_spec, out_spec],
        out_shape=[out_shape, out_shape],
        scratch_shapes=[pltpu.VMEM((lq + 2 * RADIUS, width), BF16),
                        pltpu.VMEM((lq + 2 * RADIUS, width), BF16)],
        compiler_params=_cparams(3),
        name=f"dilated_attn_d{dilation}",
    )(qkv, qkv, qkv, qkv, qkv, qkv, qkv, bias)
    rows = batch * sub_len
    return o.reshape(rows, dilation * C_GROUP_WIDTH), lse.reshape(rows, dilation * C_GROUP_WIDTH)


def _t5_bucket(rel):
    nb = N_BUCKETS // 2
    ret = (rel > 0).astype(np.int32) * nb
    n = np.abs(rel)
    max_exact = nb // 2
    large = max_exact + (np.log(np.maximum(n, 1) / max_exact)
                         / math.log(MAX_DISTANCE / max_exact) * (nb - max_exact)).astype(np.int32)
    large = np.minimum(large, nb - 1)
    return ret + np.where(n < max_exact, n, large)


def _attn_bias(rel_table, group, dilation):
    qc, kc = 2 * RADIUS, 4 * RADIUS
    n = qc + kc - 1
    delta = (np.arange(n) + qc - 1) % n - (qc + RADIUS - 1)
    onehot = (_t5_bucket(delta * dilation)[:, None] == np.arange(N_BUCKETS)[None, :]).astype(np.float32)
    per_delta = jnp.dot(onehot, rel_table[:, 2 * group:2 * group + 2], precision=lax.Precision.HIGHEST)
    per_delta = jnp.where((np.abs(delta) <= RADIUS)[:, None], per_delta, NEG_INF).T
    reps = -(-(qc * (n - 1)) // n)
    band = jnp.tile(per_delta, (1, reps))[:, :qc * (n - 1)].reshape(2, qc, n - 1)[:, :, :kc]
    key = np.arange(kc)[None, None, :]
    return jnp.stack([jnp.where(key < RADIUS, NEG_INF, band), band, jnp.where(key >= kc - RADIUS, NEG_INF, band)])


def _mix_out_kernel(x_ref, ya_ref, yb_ref, o0_ref, o1_ref, o2_ref, l0_ref, l1_ref, l2_ref,
                    wo_ref, bo_ref, xg_ref, qk_ref, vo_ref, xbo_ref,
                    out_ref, nat_ref, xp_ref, wob_ref, *, tm):
    _cast_once(wo_ref, wob_ref)

    dils = [dil for _, dil in DILATED_CONFIGS] * 2
    grp_refs = (o0_ref, o1_ref, o2_ref, l0_ref, l1_ref, l2_ref)
    for slot, (src_ref, dil) in enumerate(zip(grp_refs, dils)):
        for r in range(dil if dil > 1 else 0):
            nat_ref[slot, pl.ds(r, tm // dil, stride=dil), :] = src_ref[:, r * C_GROUP_WIDTH:(r + 1) * C_GROUP_WIDTH]

    def natural(slot):
        return grp_refs[slot][...] if dils[slot] == 1 else nat_ref[slot]

    l0, l1, l2 = natural(3), natural(4), natural(5)
    m = jnp.maximum(l0, jnp.maximum(l1, l2))
    e0, e1, e2 = jnp.exp(l0 - m), jnp.exp(l1 - m), jnp.exp(l2 - m)
    yc = (e0 * natural(0) + e1 * natural(1) + e2 * natural(2)) / (e0 + e1 + e2)

    y = (_dot(ya_ref[...], wob_ref[0:A_WIDTH, :])
         + _dot(yb_ref[...], wob_ref[A_WIDTH:A_WIDTH + B_WIDTH, :])
         + _dot(yc.astype(BF16), wob_ref[A_WIDTH + B_WIDTH:, :]))
    x1 = x_ref[...] + y + bo_ref[...]
    out_ref[...] = _cross_attend(x1, xg_ref, qk_ref, vo_ref, xbo_ref, xp_ref)


def _mix_out(x, ya, yb, o, lse, w_out, b_out, xg, qk, vo, layer, xb_o, *, tm, seq):
    t, d = x.shape
    tiles = seq // tm
    hm = qk.shape[-1]
    row = lambda w: pl.BlockSpec((tm, w), lambda i: (i, 0))
    grp = [pl.BlockSpec((tm // dil, dil * C_GROUP_WIDTH), lambda i: (i, 0)) for _, dil in DILATED_CONFIGS]
    return pl.pallas_call(
        functools.partial(_mix_out_kernel, tm=tm),
        grid=(t // tm,),
        in_specs=[row(d), row(A_WIDTH), row(B_WIDTH), *grp, *grp,
                  _layer_block(w_out, layer), _full((1, d)),
                  _full((1, d)),
                  pl.BlockSpec((None, None, d, hm), lambda i: (layer, i // tiles, 0, 0)),
                  pl.BlockSpec((None, None, hm, d), lambda i: (layer, i // tiles, 0, 0)),
                  _full((1, d))],
        out_specs=row(d),
        out_shape=jax.ShapeDtypeStruct((t, d), F32),
        scratch_shapes=[pltpu.VMEM((2 * C_GROUPS, tm, C_GROUP_WIDTH), F32),
                        pltpu.VMEM((tm, hm), BF16),
                        pltpu.VMEM(w_out.shape[1:], BF16)],
        compiler_params=_cparams(1, "arbitrary"),
        name="mix_out_xattn",
    )(x, ya, yb, o[0], o[1], o[2], lse[0], lse[1], lse[2], w_out, b_out, xg, qk, vo, xb_o)


def _mem_fold_kernel(mem_ref, g_ref, wkv_ref, wq_ref, wo_ref, qk_ref, vo_ref):
    d = mem_ref.shape[1]
    m = mem_ref.shape[0]
    hd = d // X_HEADS
    mem_n = _rms(mem_ref[...], g_ref[...]).astype(BF16)
    kv = _dot(mem_n, wkv_ref[...].astype(BF16)).astype(BF16)
    for h in range(X_HEADS):
        cols = slice(h * hd, (h + 1) * hd)
        wq_h = wq_ref[:, cols].astype(BF16)
        qk_ref[:, h * m:(h + 1) * m] = (_dot_nt(wq_h, kv[:, cols]) * (hd ** -0.5)).astype(BF16)
        wo_h = wo_ref[cols, :].astype(BF16)
        vo_ref[h * m:(h + 1) * m, :] = _dot(kv[:, d + h * hd:d + (h + 1) * hd], wo_h).astype(BF16)


def _mem_fold(mem, g, w_kv, w_q, w_o):
    batch, m, d = mem.shape
    depth = w_kv.shape[0]

    def per_layer(stack):
        return pl.BlockSpec((None,) + stack.shape[1:], lambda l, b: (l, 0, 0))

    return pl.pallas_call(
        _mem_fold_kernel,
        grid=(depth, batch),
        in_specs=[pl.BlockSpec((None, m, d), lambda l, b: (b, 0, 0)), pl.BlockSpec((1, d), lambda l, b: (0, 0)),
                  per_layer(w_kv), per_layer(w_q), per_layer(w_o)],
        out_specs=[pl.BlockSpec((None, None, d, X_HEADS * m), lambda l, b: (l, b, 0, 0)),
                   pl.BlockSpec((None, None, X_HEADS * m, d), lambda l, b: (l, b, 0, 0))],
        out_shape=[jax.ShapeDtypeStruct((depth, batch, d, X_HEADS * m), BF16),
                   jax.ShapeDtypeStruct((depth, batch, X_HEADS * m, d), BF16)],
        compiler_params=_cparams(2),
        name="mem_fold",
    )(mem, g, w_kv, w_q, w_o)


def _cross_attend(x, g_ref, qk_ref, vo_ref, bo_ref, p_ref):
    hb = _rms(x, g_ref[...]).astype(BF16)
    s_all = _dot(hb, qk_ref[...])
    m_len = qk_ref.shape[1] // X_HEADS
    for h in range(X_HEADS):
        cols = slice(h * m_len, (h + 1) * m_len)
        s = s_all[:, cols]
        m = jnp.max(s, axis=-1, keepdims=True)
        p = jnp.exp(s - m)
        l = jnp.sum(p, axis=-1, keepdims=True)
        p_ref[:, cols] = (p / l).astype(BF16)
    return x + _dot(p_ref[...], vo_ref[...]) + bo_ref[...]


def _ffn_kernel(xp_ref, x_ref, xn_ref, g_ref, wu_ref, bu_ref, cw_ref, cb_ref, wd_ref, bd_ref, fg_ref,
                out_ref, hb_ref, u0_ref, u1_ref, u2_ref, u3_ref, acc_ref, wdb_ref, *, tm, sub, seq, d_ff, cn,
                final_norm):
    _cast_once(wd_ref, wdb_ref)
    u_bufs = ((u0_ref, u1_ref), (u2_ref, u3_ref))
    tiles = seq // tm
    t = pl.program_id(0) % tiles
    hl = FFN_HALO
    g = g_ref[...]
    hb_ref[...] = jnp.concatenate([_rms(xp_ref[...], g), _rms(x_ref[...], g), _rms(xn_ref[...], g)],
                                  axis=0).astype(BF16)
    n_sub = tm // sub
    rows = sub + 2 * hl
    n_chunks = d_ff // cn

    def conv(u_ref, ue, cols, a):
        bias = bu_ref[:, cols]
        c0, c1, c2 = cw_ref[0:1, cols], cw_ref[1:2, cols], cw_ref[2:3, cols]
        u_ref[0, 1:1 + rows, :] = ue
        u_ref[1, hl - 1:hl - 1 + rows, :] = ue
        if a == 0:
            u_ref[0, hl:hl + 1, :] = jnp.where(t > 0, ue[hl - 1:hl, :], -bias)
        if a == n_sub - 1:
            u_ref[1, 2 * hl + sub - 1:2 * hl + sub, :] = jnp.where(t < tiles - 1, ue[hl + sub:hl + sub + 1, :], -bias)
        return (c0 * u_ref[0, hl:hl + sub, :] + c1 * ue[hl:hl + sub, :] + c2 * u_ref[1, 2 * hl:2 * hl + sub, :]
                + (bias * (c0 + c1 + c2) + cb_ref[:, cols]))

    def up(a, j):
        hb = hb_ref[a * sub:a * sub + rows, :]
        return (_dot(hb, wu_ref[:, j * cn:(j + 1) * cn]), _dot(hb, wu_ref[:, d_ff + j * cn:d_ff + (j + 1) * cn]))

    def down(a, j, act):
        part = _dot(act, wdb_ref[j * cn:(j + 1) * cn, :])
        if j == 0:
            acc_ref[...] = part
        else:
            acc_ref[...] += part
        if j == n_chunks - 1:
            out_rows = slice(a * sub, (a + 1) * sub)
            y = x_ref[out_rows, :] + acc_ref[...] + bd_ref[...]
            if final_norm:
                y = _rms(y, fg_ref[...])
            out_ref[out_rows, :] = y

    items = [(a, j) for a in range(n_sub) for j in range(n_chunks)]
    ue = up(*items[0])
    act_prev = None
    for idx, (a, j) in enumerate(items):
        ue_next = up(*items[idx + 1]) if idx + 1 < len(items) else None
        if act_prev is not None:
            down(*items[idx - 1], act_prev)
        gate = conv(u_bufs[idx % 2][0], ue[0], slice(j * cn, (j + 1) * cn), a)
        val = conv(u_bufs[idx % 2][1], ue[1], slice(d_ff + j * cn, d_ff + (j + 1) * cn), a)
        act_prev = (gate * jax.nn.sigmoid(gate) * val).astype(BF16)
        ue = ue_next
    down(*items[-1], act_prev)


def _ffn(x, g, w_up, layer, b_up, conv_w, conv_b, w_down, b_down, final_g, *, tm, sub, seq, cn, final_norm):
    t, d = x.shape
    d_ff = w_down.shape[1]
    hl = FFN_HALO
    blk_per_tile = tm // hl
    n_hl = t // hl
    row = pl.BlockSpec((tm, d), lambda i: (i, 0))
    return pl.pallas_call(
        functools.partial(_ffn_kernel, tm=tm, sub=sub, seq=seq, d_ff=d_ff, cn=cn, final_norm=final_norm),
        grid=(t // tm,),
        in_specs=[pl.BlockSpec((hl, d), lambda i: (jnp.maximum(i * blk_per_tile - 1, 0), 0)),
                  row,
                  pl.BlockSpec((hl, d), lambda i: (jnp.minimum((i + 1) * blk_per_tile, n_hl - 1), 0)),
                  _full((1, d)), _layer_block(w_up, layer), _full((1, 2 * d_ff)), _full(conv_w.shape),
                  _full((1, 2 * d_ff)), _layer_block(w_down, layer), _full((1, d)), _full((1, d))],
        out_specs=row,
        out_shape=jax.ShapeDtypeStruct((t, d), F32),
        scratch_shapes=([pltpu.VMEM((tm + 2 * hl, d), BF16)] + [pltpu.VMEM((2, sub + 3 * hl, cn), F32)] * 4
                        + [pltpu.VMEM((sub, d), F32), pltpu.VMEM((d_ff, d), BF16)]),
        compiler_params=_cparams(1, "arbitrary"),
        name="conv_ffn",
    )(x, x, x, g, w_up, b_up, conv_w, conv_b, w_down, b_down, final_g)


def _block_diag(blocks):
    n, a, b = blocks.shape
    eye = jnp.eye(n, dtype=blocks.dtype)
    return (eye[:, None, :, None] * blocks[:, :, None, :]).reshape(n * a, n * b)


def kernel(x, mem, rel_table, mem_norm_g, norm_mix_g, w_in, b_in, gmlp_v_g, gmlp_w_s, gmlp_b_s, pool_w, pool_b, pool_scale, w_out, b_out, norm_mem_g, xattn_w_q, xattn_w_kv, xattn_w_o, xattn_b_o, norm_ffn_g, ffn_w_up, ffn_b_up, ffn_conv_w, ffn_conv_b, ffn_w_down, ffn_b_down, final_norm_g):
    batch, seq, d = x.shape
    mem_len = mem.shape[1]
    depth = w_in.shape[0]
    t = batch * seq
    assert all(w // (2 * dil) == RADIUS for w, dil in DILATED_CONFIGS)

    row1 = lambda a: a.reshape(1, -1)
    ones_bd = jnp.asarray(np.kron(np.eye(A_HEADS), np.ones((HEAD_DIM, HEAD_DIM))), BF16)
    biases = [_attn_bias(rel_table, g, dil) for g, (_, dil) in enumerate(DILATED_CONFIGS)]
    qk, vo = _mem_fold(mem, row1(mem_norm_g), xattn_w_kv, xattn_w_q, xattn_w_o)
    w_up_bf16 = ffn_w_up.astype(BF16)

    xf = x.reshape(t, d)
    for l in range(depth):
        ws_cat = (gmlp_w_s[l].reshape(A_HEADS // 2, 2, GMLP_CHUNK, GMLP_CHUNK)
                  .transpose(0, 2, 1, 3).reshape(A_HEADS // 2, GMLP_CHUNK, 2 * GMLP_CHUNK).astype(BF16))
        bs_full = jnp.repeat(gmlp_b_s[l].T, HEAD_DIM, axis=1)
        ya, yb, *qkv = _mix_in(xf, row1(norm_mix_g[l]), w_in, l, row1(b_in[l]),
                               row1(gmlp_v_g[l]), ws_cat, bs_full, ones_bd,
                               _block_diag(pool_w[l]).astype(BF16), row1(pool_b[l]), row1(pool_scale[l]),
                               tm=1024, seq=seq)
        o, lse = [], []
        for g, (_, dil) in enumerate(DILATED_CONFIGS):
            og, lg = _attn(qkv[g], biases[g], dilation=dil, batch=batch, seq=seq, lq=2048 // min(dil, 8),
                           n_sub=8)
            o.append(og)
            lse.append(lg)
        xf = _mix_out(xf, ya, yb, o, lse, w_out, row1(b_out[l]),
                      row1(norm_mem_g[l]), qk, vo, l, row1(xattn_b_o[l]), tm=512, seq=seq)
        xf = _ffn(xf, row1(norm_ffn_g[l]), w_up_bf16, l, row1(ffn_b_up[l]), ffn_conv_w[l],
                  row1(ffn_conv_b[l]), ffn_w_down, row1(ffn_b_down[l]), row1(final_norm_g),
                  tm=512, sub=256, seq=seq, cn=256, final_norm=(l == depth - 1))
    return xf.reshape(batch, seq, d)
```

```python
import functools
import math

import numpy as np
import jax
import jax.numpy as jnp
from jax import lax
from jax.experimental import pallas as pl
from jax.experimental.pallas import tpu as pltpu

F32 = jnp.float32
BF16 = jnp.bfloat16

HEAD_DIM = 64
A_HEADS = 6
A_WIDTH = A_HEADS * HEAD_DIM
GMLP_CHUNK = 128
POOL_WINDOWS = (2, 4, 8, 16)
B_GROUP_DIM = 64
B_WIDTH = len(POOL_WINDOWS) * B_GROUP_DIM
DILATED_CONFIGS = ((128, 1), (512, 4), (2048, 16))
C_GROUPS = len(DILATED_CONFIGS)
C_GROUP_WIDTH = 2 * HEAD_DIM
C_WIDTH = C_GROUPS * C_GROUP_WIDTH
RADIUS = 64
N_BUCKETS = 32
MAX_DISTANCE = 1024
X_HEADS = 4
EPS = 1e-6
NEG_INF = -1e30

LANES = 128
POOL_HALO = 8
FFN_HALO = 8
VMEM_LIMIT = 56 * 1024 * 1024


def _cparams(n_axes, semantics="parallel"):
    return pltpu.CompilerParams(dimension_semantics=(semantics,) * n_axes, vmem_limit_bytes=VMEM_LIMIT)


def _layer_block(stacked, layer):
    return pl.BlockSpec((None,) + stacked.shape[1:], lambda *_: (layer,) + (0,) * (stacked.ndim - 1))


def _rms(x, g):
    ms = jnp.mean(x * x, axis=-1, keepdims=True)
    return x * lax.rsqrt(ms + EPS) * g


def _gelu_tanh(x):
    return x * (0.5 * (1.0 + jnp.tanh(math.sqrt(2.0 / math.pi) * (x + 0.044715 * (x * x * x)))))


def _dot(a, b):
    return jnp.dot(a, b, preferred_element_type=F32)


def _cast_once(src_ref, dst_ref):
    @pl.when(pl.program_id(0) == 0)
    def _():
        dst_ref[...] = src_ref[...].astype(BF16)


def _dot_nt(a, b):
    return lax.dot_general(a, b, (((1,), (1,)), ((), ())), preferred_element_type=F32)


def _full(shape):
    return pl.BlockSpec(shape, lambda *_: (0,) * len(shape))


def _pool_minus_identity(zx_ref, t, tm, seq):
    hl = POOL_HALO
    pos = t * tm + lax.broadcasted_iota(jnp.int32, (tm, LANES), 0)
    first_group = lax.broadcasted_iota(jnp.int32, (tm, LANES), 1) < B_GROUP_DIM
    pooled = []
    for cb in range(B_WIDTH // LANES):
        cols = slice(cb * LANES, (cb + 1) * LANES)
        half_a, half_b = POOL_WINDOWS[2 * cb] // 2, POOL_WINDOWS[2 * cb + 1] // 2

        def shifted(j):
            return zx_ref[hl + j:hl + j + tm, cols]

        acc = shifted(-1) + shifted(0)
        sums = {1: acc}
        for half in (2, 4, 8):
            if half > half_b:
                break
            for j in range(half // 2, half):
                acc = acc + shifted(-j - 1) + shifted(j)
            sums[half] = acc
        half = jnp.where(first_group, half_a, half_b)
        cnt = (jnp.minimum(pos + half, seq) - jnp.maximum(pos - half, 0)).astype(F32)
        wsum = jnp.where(first_group, sums[half_a], sums[half_b])
        pooled.append(wsum / cnt - shifted(0))
    return jnp.concatenate(pooled, axis=-1)


def _mix_in_kernel(xp_ref, x_ref, xn_ref, g_ref, w_ref, b_ref, vg_ref, ws_ref, bs_ref, ones_ref,
                   wp_ref, bp_ref, ps_ref,
                   ya_ref, yb_ref, g0_ref, g1_ref, g2_ref, qkv_ref, wb_ref, zx_ref, *, tm, seq):
    _cast_once(w_ref, wb_ref)
    tiles = seq // tm
    t = pl.program_id(0) % tiles
    g = g_ref[...]
    hb = _rms(x_ref[...], g).astype(BF16)

    def seg(a, b):
        return _dot(hb, wb_ref[:, a:b]) + b_ref[:, a:b]

    off_b = 2 * A_WIDTH
    off_c = off_b + B_WIDTH
    uv = seg(0, off_b)
    hl = POOL_HALO
    h_halo = jnp.concatenate([_rms(xp_ref[...], g), _rms(xn_ref[...], g)], axis=0).astype(BF16)
    z_halo = _dot(h_halo, wb_ref[:, off_b:off_c]) + b_ref[:, off_b:off_c]
    zx_ref[0:hl] = jnp.where(t > 0, z_halo[0:hl], 0.0)
    zx_ref[hl:hl + tm] = seg(off_b, off_c)
    zx_ref[hl + tm:] = jnp.where(t < tiles - 1, z_halo[hl:], 0.0)
    qk = seg(off_c, off_c + 2 * C_WIDTH)
    parts = (qk[:, :C_WIDTH] * (HEAD_DIM ** -0.5), qk[:, C_WIDTH:], seg(off_c + 2 * C_WIDTH, off_c + 3 * C_WIDTH))

    pooled = _pool_minus_identity(zx_ref, t, tm, seq).astype(BF16)
    yb_ref[...] = ((_dot(pooled, wp_ref[...]) + bp_ref[...]) * ps_ref[...]).astype(BF16)

    for grp, (out_ref, (_, dil)) in enumerate(zip((g0_ref, g1_ref, g2_ref), DILATED_CONFIGS)):
        for part in range(3):
            tile = parts[part][:, grp * C_GROUP_WIDTH:(grp + 1) * C_GROUP_WIDTH]
            if dil == 1:
                out_ref[part] = tile.astype(BF16)
                continue
            qkv_ref[part, grp] = tile
            for r in range(dil):
                piece = qkv_ref[part, grp, pl.ds(r, tm // dil, stride=dil), :]
                out_ref[part, :, r * C_GROUP_WIDTH:(r + 1) * C_GROUP_WIDTH] = piece.astype(BF16)

    uv = _gelu_tanh(uv)
    u, v = uv[:, :A_WIDTH], uv[:, A_WIDTH:]
    ssum = _dot((v * v).astype(BF16), ones_ref[...])
    vn = v * lax.rsqrt(ssum * (1.0 / HEAD_DIM) + EPS) * vg_ref[...]
    first_head = lax.broadcasted_iota(jnp.int32, (GMLP_CHUNK, LANES), 1) < HEAD_DIM
    for c in range(tm // GMLP_CHUNK):
        rows = slice(c * GMLP_CHUNK, (c + 1) * GMLP_CHUNK)
        for j in range(A_WIDTH // LANES):
            cols = slice(j * LANES, (j + 1) * LANES)
            vp = vn[rows, cols]
            rhs = jnp.concatenate([jnp.where(first_head, vp, 0.0).astype(BF16),
                                   jnp.where(first_head, 0.0, vp).astype(BF16)], axis=0)
            s = _dot(ws_ref[j], rhs) + bs_ref[:, cols]
            ya_ref[rows, cols] = (u[rows, cols] * s).astype(BF16)


def _mix_in(x, g, w_in, layer, b_in, v_gain, ws_cat, bs_full, ones_bd, wp_bd, bp, ps, *, tm, seq):
    t, d = x.shape
    in_width = w_in.shape[2]
    hl = POOL_HALO
    blk_per_tile = tm // hl
    n_hl = t // hl
    qkv_shapes = [jax.ShapeDtypeStruct((3, t // dil, dil * C_GROUP_WIDTH), BF16) for _, dil in DILATED_CONFIGS]
    qkv_specs = [pl.BlockSpec((3, tm // dil, dil * C_GROUP_WIDTH), lambda i: (0, i, 0))
                 for _, dil in DILATED_CONFIGS]
    return pl.pallas_call(
        functools.partial(_mix_in_kernel, tm=tm, seq=seq),
        grid=(t // tm,),
        in_specs=[pl.BlockSpec((hl, d), lambda i: (jnp.maximum(i * blk_per_tile - 1, 0), 0)),
                  pl.BlockSpec((tm, d), lambda i: (i, 0)),
                  pl.BlockSpec((hl, d), lambda i: (jnp.minimum((i + 1) * blk_per_tile, n_hl - 1), 0)),
                  _layer_block(g, layer), _layer_block(w_in, layer), _layer_block(b_in, layer),
                  _layer_block(v_gain, layer), _layer_block(ws_cat, layer), _layer_block(bs_full, layer),
                  _full(ones_bd.shape), _layer_block(wp_bd, layer), _layer_block(bp, layer),
                  _layer_block(ps, layer)],
        out_specs=[pl.BlockSpec((tm, A_WIDTH), lambda i: (i, 0)),
                   pl.BlockSpec((tm, B_WIDTH), lambda i: (i, 0))] + qkv_specs,
        out_shape=[jax.ShapeDtypeStruct((t, A_WIDTH), BF16),
                   jax.ShapeDtypeStruct((t, B_WIDTH), BF16)] + qkv_shapes,
        scratch_shapes=[pltpu.VMEM((3, C_GROUPS, tm, C_GROUP_WIDTH), F32), pltpu.VMEM((d, in_width), BF16),
                        pltpu.VMEM((tm + 2 * hl, B_WIDTH), F32)],
        compiler_params=_cparams(1, "arbitrary"),
        name="mix_in",
    )(x, x, x, g, w_in, b_in, v_gain, ws_cat, bs_full, ones_bd, wp_bd, bp, ps)


def _attn_kernel(q_ref, kp_ref, k_ref, kn_ref, vp_ref, v_ref, vn_ref, bias_ref,
                 o_ref, lse_ref, kx_ref, vx_ref, *, lq, n_blocks, n_sub):
    i = pl.program_id(2)
    kx_ref[0:RADIUS] = kp_ref[...]
    kx_ref[RADIUS:RADIUS + lq] = k_ref[...]
    kx_ref[RADIUS + lq:] = kn_ref[...]
    vx_ref[0:RADIUS] = vp_ref[...]
    vx_ref[RADIUS:RADIUS + lq] = v_ref[...]
    vx_ref[RADIUS + lq:] = vn_ref[...]

    qc = 2 * RADIUS
    kc = qc + 2 * RADIUS
    n_chunks = lq // qc
    first_head = lax.broadcasted_iota(jnp.int32, (qc, LANES), 1) < HEAD_DIM
    for c in range(n_chunks):
        rows = slice(c * qc, (c + 1) * qc)
        variant = 1
        if c == 0:
            variant = jnp.where(i == 0, 0, 1)
        if c == n_chunks - 1:
            variant = jnp.where(i == n_blocks - 1, 2, variant)
        for r in range(n_sub):
            lanes = slice(r * LANES, (r + 1) * LANES)
            q = q_ref[rows, lanes]
            keys = kx_ref[c * qc:c * qc + kc, lanes]
            vals = vx_ref[c * qc:c * qc + kc, lanes]
            o_heads, lse_heads = [], []
            for h in range(2):
                qh = jnp.where(first_head if h == 0 else ~first_head, q, jnp.zeros_like(q))
                s = _dot_nt(qh, keys) + bias_ref[variant, h]
                m = jnp.max(s, axis=-1, keepdims=True)
                p = jnp.exp(s - m)
                l = jnp.sum(p, axis=-1, keepdims=True)
                o_heads.append(_dot(p.astype(BF16), vals) / l)
                lse_heads.append(jnp.broadcast_to(m + jnp.log(l), (qc, LANES)))
            o_ref[rows, lanes] = jnp.where(first_head, o_heads[0], o_heads[1])
            lse_ref[rows, lanes] = jnp.where(first_head, lse_heads[0], lse_heads[1])


def _attn(qkv, biases, group, *, dilation, batch, seq, lq, n_sub):
    sub_len = seq // dilation
    lq = min(lq, sub_len)
    n_sub = min(n_sub, dilation)
    n_blocks = sub_len // lq
    assert sub_len >= 4 * RADIUS
    qkv = qkv.reshape(3, batch, sub_len, dilation * C_GROUP_WIDTH)
    halo_per_blk = lq // RADIUS
    n_halo = sub_len // RADIUS
    width = n_sub * C_GROUP_WIDTH

    def main(part):
        return pl.BlockSpec((None, None, lq, width), lambda b, r, i: (part, b, i, r))

    def prev(part):
        return pl.BlockSpec((None, None, RADIUS, width),
                            lambda b, r, i: (part, b, jnp.maximum(i * halo_per_blk - 1, 0), r))

    def nxt(part):
        return pl.BlockSpec((None, None, RADIUS, width),
                            lambda b, r, i: (part, b, jnp.minimum((i + 1) * halo_per_blk, n_halo - 1), r))

    out_spec = pl.BlockSpec((None, lq, width), lambda b, r, i: (b, i, r))
    out_shape = jax.ShapeDtypeStruct((batch, sub_len, dilation * C_GROUP_WIDTH), F32)
    o, lse = pl.pallas_call(
        functools.partial(_attn_kernel, lq=lq, n_blocks=n_blocks, n_sub=n_sub),
        grid=(batch, dilation // n_sub, n_blocks),
        in_specs=[main(0), prev(1), main(1), nxt(1), prev(2), main(2), nxt(2),
                  pl.BlockSpec((None,) + biases.shape[1:], lambda b, r, i: (group, 0, 0, 0, 0))],
        out_specs=[out_spec, out_spec],
        out_shape=[out_shape, out_shape],
        scratch_shapes=[pltpu.VMEM((lq + 2 * RADIUS, width), BF16),
                        pltpu.VMEM((lq + 2 * RADIUS, width), BF16)],
        compiler_params=_cparams(3),
        name=f"dilated_attn_d{dilation}",
    )(qkv, qkv, qkv, qkv, qkv, qkv, qkv, biases)
    rows = batch * sub_len
    return o.reshape(rows, dilation * C_GROUP_WIDTH), lse.reshape(rows, dilation * C_GROUP_WIDTH)


def _t5_bucket(rel):
    nb = N_BUCKETS // 2
    ret = (rel > 0).astype(np.int32) * nb
    n = np.abs(rel)
    max_exact = nb // 2
    large = max_exact + (np.log(np.maximum(n, 1) / max_exact)
                         / math.log(MAX_DISTANCE / max_exact) * (nb - max_exact)).astype(np.int32)
    large = np.minimum(large, nb - 1)
    return ret + np.where(n < max_exact, n, large)


def _attn_biases(rel_table):
    qc, kc = 2 * RADIUS, 4 * RADIUS
    n = qc + kc - 1
    delta = (np.arange(n) + qc - 1) % n - (qc + RADIUS - 1)
    dils = np.array([dil for _, dil in DILATED_CONFIGS])
    buckets = _t5_bucket(delta[None, :] * dils[:, None])
    onehot = (buckets[..., None] == np.arange(N_BUCKETS)).astype(np.float32)
    table = rel_table.reshape(N_BUCKETS, C_GROUPS, 2)
    per_delta = jnp.einsum('gnb,bgh->ghn', onehot, table, precision=lax.Precision.HIGHEST)
    per_delta = jnp.where(np.abs(delta) <= RADIUS, per_delta, NEG_INF)
    reps = -(-(qc * (n - 1)) // n)
    band = jnp.tile(per_delta, (1, 1, reps))[..., :qc * (n - 1)].reshape(C_GROUPS, 2, qc, n - 1)[..., :kc]
    key = np.arange(kc)
    return jnp.stack([jnp.where(key < RADIUS, NEG_INF, band), band, jnp.where(key >= kc - RADIUS, NEG_INF, band)],
                     axis=1)


def _mix_out_kernel(x_ref, ya_ref, yb_ref, o0_ref, o1_ref, o2_ref, l0_ref, l1_ref, l2_ref,
                    wo_ref, bo_ref, xg_ref, qk_ref, vo_ref, xbo_ref,
                    out_ref, nat_ref, xp_ref, wob_ref, *, tm):
    _cast_once(wo_ref, wob_ref)

    dils = [dil for _, dil in DILATED_CONFIGS] * 2
    grp_refs = (o0_ref, o1_ref, o2_ref, l0_ref, l1_ref, l2_ref)
    for slot, (src_ref, dil) in enumerate(zip(grp_refs, dils)):
        for r in range(dil if dil > 1 else 0):
            nat_ref[slot, pl.ds(r, tm // dil, stride=dil), :] = src_ref[:, r * C_GROUP_WIDTH:(r + 1) * C_GROUP_WIDTH]

    def natural(slot):
        return grp_refs[slot][...] if dils[slot] == 1 else nat_ref[slot]

    l0, l1, l2 = natural(3), natural(4), natural(5)
    m = jnp.maximum(l0, jnp.maximum(l1, l2))
    e0, e1, e2 = jnp.exp(l0 - m), jnp.exp(l1 - m), jnp.exp(l2 - m)
    yc = (e0 * natural(0) + e1 * natural(1) + e2 * natural(2)) / (e0 + e1 + e2)

    y = (_dot(ya_ref[...], wob_ref[0:A_WIDTH, :])
         + _dot(yb_ref[...], wob_ref[A_WIDTH:A_WIDTH + B_WIDTH, :])
         + _dot(yc.astype(BF16), wob_ref[A_WIDTH + B_WIDTH:, :]))
    x1 = x_ref[...] + y + bo_ref[...]
    out_ref[...] = _cross_attend(x1, xg_ref, qk_ref, vo_ref, xbo_ref, xp_ref)


def _mix_out(x, ya, yb, o, lse, w_out, b_out, xg, qk, vo, layer, xb_o, *, tm, seq):
    t, d = x.shape
    tiles = seq // tm
    hm = qk.shape[-1]
    row = lambda w: pl.BlockSpec((tm, w), lambda i: (i, 0))
    grp = [pl.BlockSpec((tm // dil, dil * C_GROUP_WIDTH), lambda i: (i, 0)) for _, dil in DILATED_CONFIGS]
    return pl.pallas_call(
        functools.partial(_mix_out_kernel, tm=tm),
        grid=(t // tm,),
        in_specs=[row(d), row(A_WIDTH), row(B_WIDTH), *grp, *grp,
                  _layer_block(w_out, layer), _layer_block(b_out, layer), _layer_block(xg, layer),
                  pl.BlockSpec((None, None, d, hm), lambda i: (layer, i // tiles, 0, 0)),
                  pl.BlockSpec((None, None, hm, d), lambda i: (layer, i // tiles, 0, 0)),
                  _layer_block(xb_o, layer)],
        out_specs=row(d),
        out_shape=jax.ShapeDtypeStruct((t, d), F32),
        scratch_shapes=[pltpu.VMEM((2 * C_GROUPS, tm, C_GROUP_WIDTH), F32),
                        pltpu.VMEM((tm, hm), BF16),
                        pltpu.VMEM(w_out.shape[1:], BF16)],
        compiler_params=_cparams(1, "arbitrary"),
        name="mix_out_xattn",
    )(x, ya, yb, o[0], o[1], o[2], lse[0], lse[1], lse[2], w_out, b_out, xg, qk, vo, xb_o)


def _mem_fold_kernel(mem_ref, g_ref, wkv_ref, wq_ref, wo_ref, qk_ref, vo_ref):
    d = mem_ref.shape[1]
    m = mem_ref.shape[0]
    hd = d // X_HEADS
    mem_n = _rms(mem_ref[...], g_ref[...]).astype(BF16)
    kv = _dot(mem_n, wkv_ref[...].astype(BF16)).astype(BF16)
    for h in range(X_HEADS):
        cols = slice(h * hd, (h + 1) * hd)
        wq_h = wq_ref[:, cols].astype(BF16)
        qk_ref[:, h * m:(h + 1) * m] = (_dot_nt(wq_h, kv[:, cols]) * (hd ** -0.5)).astype(BF16)
        wo_h = wo_ref[cols, :].astype(BF16)
        vo_ref[h * m:(h + 1) * m, :] = _dot(kv[:, d + h * hd:d + (h + 1) * hd], wo_h).astype(BF16)


def _mem_fold(mem, g, w_kv, w_q, w_o):
    batch, m, d = mem.shape
    depth = w_kv.shape[0]

    def per_layer(stack):
        return pl.BlockSpec((None,) + stack.shape[1:], lambda l, b: (l, 0, 0))

    return pl.pallas_call(
        _mem_fold_kernel,
        grid=(depth, batch),
        in_specs=[pl.BlockSpec((None, m, d), lambda l, b: (b, 0, 0)), pl.BlockSpec((1, d), lambda l, b: (0, 0)),
                  per_layer(w_kv), per_layer(w_q), per_layer(w_o)],
        out_specs=[pl.BlockSpec((None, None, d, X_HEADS * m), lambda l, b: (l, b, 0, 0)),
                   pl.BlockSpec((None, None, X_HEADS * m, d), lambda l, b: (l, b, 0, 0))],
        out_shape=[jax.ShapeDtypeStruct((depth, batch, d, X_HEADS * m), BF16),
                   jax.ShapeDtypeStruct((depth, batch, X_HEADS * m, d), BF16)],
        compiler_params=_cparams(2),
        name="mem_fold",
    )(mem, g, w_kv, w_q, w_o)


def _cross_attend(x, g_ref, qk_ref, vo_ref, bo_ref, p_ref):
    hb = _rms(x, g_ref[...]).astype(BF16)
    s_all = _dot(hb, qk_ref[...])
    m_len = qk_ref.shape[1] // X_HEADS
    for h in range(X_HEADS):
        cols = slice(h * m_len, (h + 1) * m_len)
        s = s_all[:, cols]
        m = jnp.max(s, axis=-1, keepdims=True)
        p = jnp.exp(s - m)
        l = jnp.sum(p, axis=-1, keepdims=True)
        p_ref[:, cols] = (p / l).astype(BF16)
    return x + _dot(p_ref[...], vo_ref[...]) + bo_ref[...]


def _ffn_kernel(xp_ref, x_ref, xn_ref, g_ref, wu_ref, bu_ref, cw_ref, cb_ref, wd_ref, bd_ref, fg_ref,
                out_ref, hb_ref, u0_ref, u1_ref, u2_ref, u3_ref, acc_ref, wdb_ref, *, tm, sub, seq, d_ff, cn,
                final_norm):
    _cast_once(wd_ref, wdb_ref)
    u_bufs = ((u0_ref, u1_ref), (u2_ref, u3_ref))
    tiles = seq // tm
    t = pl.program_id(0) % tiles
    hl = FFN_HALO
    g = g_ref[...]
    hb_ref[...] = jnp.concatenate([_rms(xp_ref[...], g), _rms(x_ref[...], g), _rms(xn_ref[...], g)],
                                  axis=0).astype(BF16)
    n_sub = tm // sub
    rows = sub + 2 * hl
    n_chunks = d_ff // cn

    def conv(u_ref, ue, cols, a):
        bias = bu_ref[:, cols]
        c0, c1, c2 = cw_ref[0:1, cols], cw_ref[1:2, cols], cw_ref[2:3, cols]
        u_ref[0, 1:1 + rows, :] = ue
        u_ref[1, hl - 1:hl - 1 + rows, :] = ue
        if a == 0:
            u_ref[0, hl:hl + 1, :] = jnp.where(t > 0, ue[hl - 1:hl, :], -bias)
        if a == n_sub - 1:
            u_ref[1, 2 * hl + sub - 1:2 * hl + sub, :] = jnp.where(t < tiles - 1, ue[hl + sub:hl + sub + 1, :], -bias)
        return (c0 * u_ref[0, hl:hl + sub, :] + c1 * ue[hl:hl + sub, :] + c2 * u_ref[1, 2 * hl:2 * hl + sub, :]
                + (bias * (c0 + c1 + c2) + cb_ref[:, cols]))

    def up(a, j):
        hb = hb_ref[a * sub:a * sub + rows, :]
        return (_dot(hb, wu_ref[:, j * cn:(j + 1) * cn]), _dot(hb, wu_ref[:, d_ff + j * cn:d_ff + (j + 1) * cn]))

    def down(a, j, act):
        part = _dot(act, wdb_ref[j * cn:(j + 1) * cn, :])
        if j == 0:
            acc_ref[...] = part
        else:
            acc_ref[...] += part
        if j == n_chunks - 1:
            out_rows = slice(a * sub, (a + 1) * sub)
            y = x_ref[out_rows, :] + acc_ref[...] + bd_ref[...]
            if final_norm:
                y = _rms(y, fg_ref[...])
            out_ref[out_rows, :] = y

    items = [(a, j) for a in range(n_sub) for j in range(n_chunks)]
    ue = up(*items[0])
    act_prev = None
    for idx, (a, j) in enumerate(items):
        ue_next = up(*items[idx + 1]) if idx + 1 < len(items) else None
        if act_prev is not None:
            down(*items[idx - 1], act_prev)
        gate = conv(u_bufs[idx % 2][0], ue[0], slice(j * cn, (j + 1) * cn), a)
        val = conv(u_bufs[idx % 2][1], ue[1], slice(d_ff + j * cn, d_ff + (j + 1) * cn), a)
        act_prev = (gate * jax.nn.sigmoid(gate) * val).astype(BF16)
        ue = ue_next
    down(*items[-1], act_prev)


def _ffn(x, g, w_up, layer, b_up, conv_w, conv_b, w_down, b_down, final_g, *, tm, sub, seq, cn, final_norm):
    t, d = x.shape
    d_ff = w_down.shape[1]
    hl = FFN_HALO
    blk_per_tile = tm // hl
    n_hl = t // hl
    row = pl.BlockSpec((tm, d), lambda i: (i, 0))
    return pl.pallas_call(
        functools.partial(_ffn_kernel, tm=tm, sub=sub, seq=seq, d_ff=d_ff, cn=cn, final_norm=final_norm),
        grid=(t // tm,),
        in_specs=[pl.BlockSpec((hl, d), lambda i: (jnp.maximum(i * blk_per_tile - 1, 0), 0)),
                  row,
                  pl.BlockSpec((hl, d), lambda i: (jnp.minimum((i + 1) * blk_per_tile, n_hl - 1), 0)),
                  _layer_block(g, layer), _layer_block(w_up, layer), _layer_block(b_up, layer),
                  _layer_block(conv_w, layer), _layer_block(conv_b, layer), _layer_block(w_down, layer),
                  _layer_block(b_down, layer), _full((1, d))],
        out_specs=row,
        out_shape=jax.ShapeDtypeStruct((t, d), F32),
        scratch_shapes=([pltpu.VMEM((tm + 2 * hl, d), BF16)] + [pltpu.VMEM((2, sub + 3 * hl, cn), F32)] * 4
                        + [pltpu.VMEM((sub, d), F32), pltpu.VMEM((d_ff, d), BF16)]),
        compiler_params=_cparams(1, "arbitrary"),
        name="conv_ffn",
    )(x, x, x, g, w_up, b_up, conv_w, conv_b, w_down, b_down, final_g)


def _block_diag(blocks):
    *lead, n, a, b = blocks.shape
    eye = jnp.eye(n, dtype=blocks.dtype)
    return (eye[:, None, :, None] * blocks[..., :, :, None, :]).reshape(*lead, n * a, n * b)


def kernel(x, mem, rel_table, mem_norm_g, norm_mix_g, w_in, b_in, gmlp_v_g, gmlp_w_s, gmlp_b_s, pool_w, pool_b, pool_scale, w_out, b_out, norm_mem_g, xattn_w_q, xattn_w_kv, xattn_w_o, xattn_b_o, norm_ffn_g, ffn_w_up, ffn_b_up, ffn_conv_w, ffn_conv_b, ffn_w_down, ffn_b_down, final_norm_g):
    batch, seq, d = x.shape
    mem_len = mem.shape[1]
    depth = w_in.shape[0]
    t = batch * seq
    assert all(w // (2 * dil) == RADIUS for w, dil in DILATED_CONFIGS)

    rows = lambda a: a.reshape(depth, 1, -1)
    ones_bd = jnp.asarray(np.kron(np.eye(A_HEADS), np.ones((HEAD_DIM, HEAD_DIM))), BF16)
    biases = _attn_biases(rel_table)
    qk, vo = _mem_fold(mem, mem_norm_g.reshape(1, d), xattn_w_kv, xattn_w_q, xattn_w_o)
    w_up_bf16 = ffn_w_up.astype(BF16)
    ws_cat = (gmlp_w_s.reshape(depth, A_HEADS // 2, 2, GMLP_CHUNK, GMLP_CHUNK).transpose(0, 1, 3, 2, 4)
              .reshape(depth, A_HEADS // 2, GMLP_CHUNK, 2 * GMLP_CHUNK).astype(BF16))
    bs_full = jnp.repeat(jnp.swapaxes(gmlp_b_s, 1, 2), HEAD_DIM, axis=2)
    wp_bd = _block_diag(pool_w).astype(BF16)

    xf = x.reshape(t, d)
    for l in range(depth):
        ya, yb, *qkv = _mix_in(xf, rows(norm_mix_g), w_in, l, rows(b_in), rows(gmlp_v_g), ws_cat, bs_full, ones_bd,
                               wp_bd, rows(pool_b), rows(pool_scale), tm=1024, seq=seq)
        o, lse = [], []
        for g, (_, dil) in enumerate(DILATED_CONFIGS):
            og, lg = _attn(qkv[g], biases, g, dilation=dil, batch=batch, seq=seq, lq=2048 // min(dil, 8),
                           n_sub=8)
            o.append(og)
            lse.append(lg)
        xf = _mix_out(xf, ya, yb, o, lse, w_out, rows(b_out), rows(norm_mem_g), qk, vo, l, rows(xattn_b_o),
                      tm=512, seq=seq)
        xf = _ffn(xf, rows(norm_ffn_g), w_up_bf16, l, rows(ffn_b_up), ffn_conv_w, rows(ffn_conv_b), ffn_w_down,
                  rows(ffn_b_down), final_norm_g.reshape(1, d),
                  tm=512, sub=256, seq=seq, cn=256, final_norm=(l == depth - 1))
    return xf.reshape(batch, seq, d)
```

```python
import functools
import math

import numpy as np
import jax
import jax.numpy as jnp
from jax import lax
from jax.experimental import pallas as pl
from jax.experimental.pallas import tpu as pltpu

F32 = jnp.float32
BF16 = jnp.bfloat16

HEAD_DIM = 64
A_HEADS = 6
A_WIDTH = A_HEADS * HEAD_DIM
GMLP_CHUNK = 128
POOL_WINDOWS = (2, 4, 8, 16)
B_GROUP_DIM = 64
B_WIDTH = len(POOL_WINDOWS) * B_GROUP_DIM
DILATED_CONFIGS = ((128, 1), (512, 4), (2048, 16))
C_GROUPS = len(DILATED_CONFIGS)
C_GROUP_WIDTH = 2 * HEAD_DIM
C_WIDTH = C_GROUPS * C_GROUP_WIDTH
RADIUS = 64
N_BUCKETS = 32
MAX_DISTANCE = 1024
X_HEADS = 4
EPS = 1e-6
NEG_INF = -1e30

LANES = 128
POOL_HALO = 8
FFN_HALO = 8
VMEM_LIMIT = 56 * 1024 * 1024


def _cparams(n_axes, semantics="parallel"):
    return pltpu.CompilerParams(dimension_semantics=(semantics,) * n_axes, vmem_limit_bytes=VMEM_LIMIT)


def _layer_block(stacked, layer):
    return pl.BlockSpec((None,) + stacked.shape[1:], lambda *_: (layer,) + (0,) * (stacked.ndim - 1))


def _rms(x, g):
    ms = jnp.mean(x * x, axis=-1, keepdims=True)
    return x * lax.rsqrt(ms + EPS) * g


def _gelu_tanh(x):
    return x * (0.5 * (1.0 + jnp.tanh(math.sqrt(2.0 / math.pi) * (x + 0.044715 * (x * x * x)))))


def _dot(a, b):
    return jnp.dot(a, b, preferred_element_type=F32)


def _cast_once(src_ref, dst_ref):
    @pl.when(pl.program_id(0) == 0)
    def _():
        dst_ref[...] = src_ref[...].astype(BF16)


def _dot_nt(a, b):
    return lax.dot_general(a, b, (((1,), (1,)), ((), ())), preferred_element_type=F32)


def _full(shape):
    return pl.BlockSpec(shape, lambda *_: (0,) * len(shape))


def _pool_minus_identity(zx_ref, t, tm, seq):
    hl = POOL_HALO
    pos = t * tm + lax.broadcasted_iota(jnp.int32, (tm, LANES), 0)
    first_group = lax.broadcasted_iota(jnp.int32, (tm, LANES), 1) < B_GROUP_DIM
    pooled = []
    for cb in range(B_WIDTH // LANES):
        cols = slice(cb * LANES, (cb + 1) * LANES)
        half_a, half_b = POOL_WINDOWS[2 * cb] // 2, POOL_WINDOWS[2 * cb + 1] // 2

        def shifted(j):
            return zx_ref[hl + j:hl + j + tm, cols]

        acc = shifted(-1) + shifted(0)
        sums = {1: acc}
        for half in (2, 4, 8):
            if half > half_b:
                break
            for j in range(half // 2, half):
                acc = acc + shifted(-j - 1) + shifted(j)
            sums[half] = acc
        half = jnp.where(first_group, half_a, half_b)
        cnt = (jnp.minimum(pos + half, seq) - jnp.maximum(pos - half, 0)).astype(F32)
        wsum = jnp.where(first_group, sums[half_a], sums[half_b])
        pooled.append(wsum / cnt - shifted(0))
    return jnp.concatenate(pooled, axis=-1)


def _mix_in_kernel(xp_ref, x_ref, xn_ref, g_ref, w_ref, b_ref, vg_ref, ws_ref, bs_ref, ones_ref,
                   wp_ref, bp_ref, ps_ref,
                   ya_ref, yb_ref, g0_ref, g1_ref, g2_ref, qkv_ref, wb_ref, zx_ref, *, tm, seq):
    _cast_once(w_ref, wb_ref)
    tiles = seq // tm
    t = pl.program_id(0) % tiles
    g = g_ref[...]
    hb = _rms(x_ref[...], g).astype(BF16)

    def seg(a, b):
        return _dot(hb, wb_ref[:, a:b]) + b_ref[:, a:b]

    off_b = 2 * A_WIDTH
    off_c = off_b + B_WIDTH
    uv = seg(0, off_b)
    hl = POOL_HALO
    h_halo = jnp.concatenate([_rms(xp_ref[...], g), _rms(xn_ref[...], g)], axis=0).astype(BF16)
    z_halo = _dot(h_halo, wb_ref[:, off_b:off_c]) + b_ref[:, off_b:off_c]
    zx_ref[0:hl] = jnp.where(t > 0, z_halo[0:hl], 0.0)
    zx_ref[hl:hl + tm] = seg(off_b, off_c)
    zx_ref[hl + tm:] = jnp.where(t < tiles - 1, z_halo[hl:], 0.0)
    qk = seg(off_c, off_c + 2 * C_WIDTH)
    parts = (qk[:, :C_WIDTH] * (HEAD_DIM ** -0.5), qk[:, C_WIDTH:], seg(off_c + 2 * C_WIDTH, off_c + 3 * C_WIDTH))

    pooled = _pool_minus_identity(zx_ref, t, tm, seq).astype(BF16)
    yb_ref[...] = ((_dot(pooled, wp_ref[...]) + bp_ref[...]) * ps_ref[...]).astype(BF16)

    for grp, (out_ref, (_, dil)) in enumerate(zip((g0_ref, g1_ref, g2_ref), DILATED_CONFIGS)):
        for part in range(3):
            tile = parts[part][:, grp * C_GROUP_WIDTH:(grp + 1) * C_GROUP_WIDTH]
            if dil == 1:
                out_ref[part] = tile.astype(BF16)
                continue
            qkv_ref[part, grp] = tile
            for r in range(dil):
                piece = qkv_ref[part, grp, pl.ds(r, tm // dil, stride=dil), :]
                out_ref[part, :, r * C_GROUP_WIDTH:(r + 1) * C_GROUP_WIDTH] = piece.astype(BF16)

    uv = _gelu_tanh(uv)
    u, v = uv[:, :A_WIDTH], uv[:, A_WIDTH:]
    ssum = _dot((v * v).astype(BF16), ones_ref[...])
    vn = v * lax.rsqrt(ssum * (1.0 / HEAD_DIM) + EPS) * vg_ref[...]
    first_head = lax.broadcasted_iota(jnp.int32, (GMLP_CHUNK, LANES), 1) < HEAD_DIM
    for c in range(tm // GMLP_CHUNK):
        rows = slice(c * GMLP_CHUNK, (c + 1) * GMLP_CHUNK)
        for j in range(A_WIDTH // LANES):
            cols = slice(j * LANES, (j + 1) * LANES)
            vp = vn[rows, cols]
            rhs = jnp.concatenate([jnp.where(first_head, vp, 0.0).astype(BF16),
                                   jnp.where(first_head, 0.0, vp).astype(BF16)], axis=0)
            s = _dot(ws_ref[j], rhs) + bs_ref[:, cols]
            ya_ref[rows, cols] = (u[rows, cols] * s).astype(BF16)


def _mix_in(x, g, w_in, layer, b_in, v_gain, ws_cat, bs_full, ones_bd, wp_bd, bp, ps, *, tm, seq):
    t, d = x.shape
    in_width = w_in.shape[2]
    hl = POOL_HALO
    blk_per_tile = tm // hl
    n_hl = t // hl
    qkv_shapes = [jax.ShapeDtypeStruct((3, t // dil, dil * C_GROUP_WIDTH), BF16) for _, dil in DILATED_CONFIGS]
    qkv_specs = [pl.BlockSpec((3, tm // dil, dil * C_GROUP_WIDTH), lambda i: (0, i, 0))
                 for _, dil in DILATED_CONFIGS]
    return pl.pallas_call(
        functools.partial(_mix_in_kernel, tm=tm, seq=seq),
        grid=(t // tm,),
        in_specs=[pl.BlockSpec((hl, d), lambda i: (jnp.maximum(i * blk_per_tile - 1, 0), 0)),
                  pl.BlockSpec((tm, d), lambda i: (i, 0)),
                  pl.BlockSpec((hl, d), lambda i: (jnp.minimum((i + 1) * blk_per_tile, n_hl - 1), 0)),
                  _layer_block(g, layer), _layer_block(w_in, layer), _layer_block(b_in, layer),
                  _layer_block(v_gain, layer), _layer_block(ws_cat, layer), _layer_block(bs_full, layer),
                  _full(ones_bd.shape), _layer_block(wp_bd, layer), _layer_block(bp, layer),
                  _layer_block(ps, layer)],
        out_specs=[pl.BlockSpec((tm, A_WIDTH), lambda i: (i, 0)),
                   pl.BlockSpec((tm, B_WIDTH), lambda i: (i, 0))] + qkv_specs,
        out_shape=[jax.ShapeDtypeStruct((t, A_WIDTH), BF16),
                   jax.ShapeDtypeStruct((t, B_WIDTH), BF16)] + qkv_shapes,
        scratch_shapes=[pltpu.VMEM((3, C_GROUPS, tm, C_GROUP_WIDTH), F32), pltpu.VMEM((d, in_width), BF16),
                        pltpu.VMEM((tm + 2 * hl, B_WIDTH), F32)],
        compiler_params=_cparams(1, "arbitrary"),
        name="mix_in",
    )(x, x, x, g, w_in, b_in, v_gain, ws_cat, bs_full, ones_bd, wp_bd, bp, ps)


def _attn_kernel(q_ref, kp_ref, k_ref, kn_ref, vp_ref, v_ref, vn_ref, bias_ref,
                 o_ref, lse_ref, kx_ref, vx_ref, *, lq, n_blocks, n_sub):
    i = pl.program_id(2)
    kx_ref[0:RADIUS] = kp_ref[...]
    kx_ref[RADIUS:RADIUS + lq] = k_ref[...]
    kx_ref[RADIUS + lq:] = kn_ref[...]
    vx_ref[0:RADIUS] = vp_ref[...]
    vx_ref[RADIUS:RADIUS + lq] = v_ref[...]
    vx_ref[RADIUS + lq:] = vn_ref[...]

    qc = 2 * RADIUS
    kc = qc + 2 * RADIUS
    n_chunks = lq // qc
    first_head = lax.broadcasted_iota(jnp.int32, (qc, LANES), 1) < HEAD_DIM
    for c in range(n_chunks):
        rows = slice(c * qc, (c + 1) * qc)
        variant = 1
        if c == 0:
            variant = jnp.where(i == 0, 0, 1)
        if c == n_chunks - 1:
            variant = jnp.where(i == n_blocks - 1, 2, variant)
        for r in range(n_sub):
            lanes = slice(r * LANES, (r + 1) * LANES)
            q = q_ref[rows, lanes]
            keys = kx_ref[c * qc:c * qc + kc, lanes]
            vals = vx_ref[c * qc:c * qc + kc, lanes]
            o_heads, lse_heads = [], []
            for h in range(2):
                qh = jnp.where(first_head if h == 0 else ~first_head, q, jnp.zeros_like(q))
                s = _dot_nt(qh, keys) + bias_ref[variant, h]
                m = jnp.max(s, axis=-1, keepdims=True)
                p = jnp.exp(s - m)
                l = jnp.sum(p, axis=-1, keepdims=True)
                o_heads.append(_dot(p.astype(BF16), vals) / l)
                lse_heads.append(jnp.broadcast_to(m + jnp.log(l), (qc, LANES)))
            o_ref[rows, lanes] = jnp.where(first_head, o_heads[0], o_heads[1])
            lse_ref[rows, lanes] = jnp.where(first_head, lse_heads[0], lse_heads[1])


def _attn(qkv, biases, group, *, dilation, batch, seq, lq, n_sub):
    sub_len = seq // dilation
    lq = min(lq, sub_len)
    n_sub = min(n_sub, dilation)
    n_blocks = sub_len // lq
    assert sub_len >= 4 * RADIUS
    qkv = qkv.reshape(3, batch, sub_len, dilation * C_GROUP_WIDTH)
    halo_per_blk = lq // RADIUS
    n_halo = sub_len // RADIUS
    width = n_sub * C_GROUP_WIDTH

    def main(part):
        return pl.BlockSpec((None, None, lq, width), lambda b, r, i: (part, b, i, r))

    def prev(part):
        return pl.BlockSpec((None, None, RADIUS, width),
                            lambda b, r, i: (part, b, jnp.maximum(i * halo_per_blk - 1, 0), r))

    def nxt(part):
        return pl.BlockSpec((None, None, RADIUS, width),
                            lambda b, r, i: (part, b, jnp.minimum((i + 1) * halo_per_blk, n_halo - 1), r))

    out_spec = pl.BlockSpec((None, lq, width), lambda b, r, i: (b, i, r))
    out_shape = jax.ShapeDtypeStruct((batch, sub_len, dilation * C_GROUP_WIDTH), F32)
    o, lse = pl.pallas_call(
        functools.partial(_attn_kernel, lq=lq, n_blocks=n_blocks, n_sub=n_sub),
        grid=(batch, dilation // n_sub, n_blocks),
        in_specs=[main(0), prev(1), main(1), nxt(1), prev(2), main(2), nxt(2),
                  pl.BlockSpec((None,) + biases.shape[1:], lambda b, r, i: (group, 0, 0, 0, 0))],
        out_specs=[out_spec, out_spec],
        out_shape=[out_shape, out_shape],
        scratch_shapes=[pltpu.VMEM((lq + 2 * RADIUS, width), BF16),
                        pltpu.VMEM((lq + 2 * RADIUS, width), BF16)],
        compiler_params=_cparams(3),
        name=f"dilated_attn_d{dilation}",
    )(qkv, qkv, qkv, qkv, qkv, qkv, qkv, biases)
    rows = batch * sub_len
    return o.reshape(rows, dilation * C_GROUP_WIDTH), lse.reshape(rows, dilation * C_GROUP_WIDTH)


def _t5_bucket(rel):
    nb = N_BUCKETS // 2
    ret = (rel > 0).astype(np.int32) * nb
    n = np.abs(rel)
    max_exact = nb // 2
    large = max_exact + (np.log(np.maximum(n, 1) / max_exact)
                         / math.log(MAX_DISTANCE / max_exact) * (nb - max_exact)).astype(np.int32)
    large = np.minimum(large, nb - 1)
    return ret + np.where(n < max_exact, n, large)


def _attn_biases(rel_table):
    qc, kc = 2 * RADIUS, 4 * RADIUS
    n = qc + kc - 1
    delta = (np.arange(n) + qc - 1) % n - (qc + RADIUS - 1)
    dils = np.array([dil for _, dil in DILATED_CONFIGS])
    buckets = _t5_bucket(delta[None, :] * dils[:, None])
    onehot = (buckets[..., None] == np.arange(N_BUCKETS)).astype(np.float32)
    table = rel_table.reshape(N_BUCKETS, C_GROUPS, 2)
    per_delta = jnp.einsum('gnb,bgh->ghn', onehot, table, precision=lax.Precision.HIGHEST)
    per_delta = jnp.where(np.abs(delta) <= RADIUS, per_delta, NEG_INF)
    reps = -(-(qc * (n - 1)) // n)
    band = jnp.tile(per_delta, (1, 1, reps))[..., :qc * (n - 1)].reshape(C_GROUPS, 2, qc, n - 1)[..., :kc]
    key = np.arange(kc)
    return jnp.stack([jnp.where(key < RADIUS, NEG_INF, band), band, jnp.where(key >= kc - RADIUS, NEG_INF, band)],
                     axis=1)


def _mix_out_kernel(x_ref, ya_ref, yb_ref, o0_ref, o1_ref, o2_ref, l0_ref, l1_ref, l2_ref,
                    wo_ref, bo_ref, xg_ref, qk_ref, vo_ref, xbo_ref,
                    out_ref, nat_ref, xp_ref, wob_ref, *, tm):
    _cast_once(wo_ref, wob_ref)

    dils = [dil for _, dil in DILATED_CONFIGS] * 2
    grp_refs = (o0_ref, o1_ref, o2_ref, l0_ref, l1_ref, l2_ref)
    for slot, (src_ref, dil) in enumerate(zip(grp_refs, dils)):
        for r in range(dil if dil > 1 else 0):
            nat_ref[slot, pl.ds(r, tm // dil, stride=dil), :] = src_ref[:, r * C_GROUP_WIDTH:(r + 1) * C_GROUP_WIDTH]

    def natural(slot):
        return grp_refs[slot][...] if dils[slot] == 1 else nat_ref[slot]

    l0, l1, l2 = natural(3), natural(4), natural(5)
    m = jnp.maximum(l0, jnp.maximum(l1, l2))
    e0, e1, e2 = jnp.exp(l0 - m), jnp.exp(l1 - m), jnp.exp(l2 - m)
    yc = (e0 * natural(0) + e1 * natural(1) + e2 * natural(2)) / (e0 + e1 + e2)

    y = (_dot(ya_ref[...], wob_ref[0:A_WIDTH, :])
         + _dot(yb_ref[...], wob_ref[A_WIDTH:A_WIDTH + B_WIDTH, :])
         + _dot(yc.astype(BF16), wob_ref[A_WIDTH + B_WIDTH:, :]))
    x1 = x_ref[...] + y + bo_ref[...]
    out_ref[...] = _cross_attend(x1, xg_ref, qk_ref, vo_ref, xbo_ref, xp_ref)


def _mix_out(x, ya, yb, o, lse, w_out, b_out, xg, qk, vo, layer, xb_o, *, tm, seq):
    t, d = x.shape
    tiles = seq // tm
    hm = qk.shape[-1]
    row = lambda w: pl.BlockSpec((tm, w), lambda i: (i, 0))
    grp = [pl.BlockSpec((tm // dil, dil * C_GROUP_WIDTH), lambda i: (i, 0)) for _, dil in DILATED_CONFIGS]
    return pl.pallas_call(
        functools.partial(_mix_out_kernel, tm=tm),
        grid=(t // tm,),
        in_specs=[row(d), row(A_WIDTH), row(B_WIDTH), *grp, *grp,
                  _layer_block(w_out, layer), _layer_block(b_out, layer), _layer_block(xg, layer),
                  pl.BlockSpec((None, None, d, hm), lambda i: (layer, i // tiles, 0, 0)),
                  pl.BlockSpec((None, None, hm, d), lambda i: (layer, i // tiles, 0, 0)),
                  _layer_block(xb_o, layer)],
        out_specs=row(d),
        out_shape=jax.ShapeDtypeStruct((t, d), F32),
        scratch_shapes=[pltpu.VMEM((2 * C_GROUPS, tm, C_GROUP_WIDTH), F32),
                        pltpu.VMEM((tm, hm), BF16),
                        pltpu.VMEM(w_out.shape[1:], BF16)],
        compiler_params=_cparams(1, "arbitrary"),
        name="mix_out_xattn",
    )(x, ya, yb, o[0], o[1], o[2], lse[0], lse[1], lse[2], w_out, b_out, xg, qk, vo, xb_o)


def _mem_fold_kernel(mem_ref, g_ref, wkv_ref, wq_ref, wo_ref, qk_ref, vo_ref):
    d = mem_ref.shape[1]
    m = mem_ref.shape[0]
    hd = d // X_HEADS
    mem_n = _rms(mem_ref[...], g_ref[...]).astype(BF16)
    kv = _dot(mem_n, wkv_ref[...].astype(BF16)).astype(BF16)
    for h in range(X_HEADS):
        cols = slice(h * hd, (h + 1) * hd)
        wq_h = wq_ref[:, cols].astype(BF16)
        qk_ref[:, h * m:(h + 1) * m] = (_dot_nt(wq_h, kv[:, cols]) * (hd ** -0.5)).astype(BF16)
        wo_h = wo_ref[cols, :].astype(BF16)
        vo_ref[h * m:(h + 1) * m, :] = _dot(kv[:, d + h * hd:d + (h + 1) * hd], wo_h).astype(BF16)


def _mem_fold(mem, g, w_kv, w_q, w_o):
    batch, m, d = mem.shape
    depth = w_kv.shape[0]

    def per_layer(stack):
        return pl.BlockSpec((None,) + stack.shape[1:], lambda l, b: (l, 0, 0))

    return pl.pallas_call(
        _mem_fold_kernel,
        grid=(depth, batch),
        in_specs=[pl.BlockSpec((None, m, d), lambda l, b: (b, 0, 0)), pl.BlockSpec((1, d), lambda l, b: (0, 0)),
                  per_layer(w_kv), per_layer(w_q), per_layer(w_o)],
        out_specs=[pl.BlockSpec((None, None, d, X_HEADS * m), lambda l, b: (l, b, 0, 0)),
                   pl.BlockSpec((None, None, X_HEADS * m, d), lambda l, b: (l, b, 0, 0))],
        out_shape=[jax.ShapeDtypeStruct((depth, batch, d, X_HEADS * m), BF16),
                   jax.ShapeDtypeStruct((depth, batch, X_HEADS * m, d), BF16)],
        compiler_params=_cparams(2),
        name="mem_fold",
    )(mem, g, w_kv, w_q, w_o)


def _cross_attend(x, g_ref, qk_ref, vo_ref, bo_ref, p_ref):
    hb = _rms(x, g_ref[...]).astype(BF16)
    s_all = _dot(hb, qk_ref[...])
    m_len = qk_ref.shape[1] // X_HEADS
    for h in range(X_HEADS):
        cols = slice(h * m_len, (h + 1) * m_len)
        s = s_all[:, cols]
        m = jnp.max(s, axis=-1, keepdims=True)
        p = jnp.exp(s - m)
        l = jnp.sum(p, axis=-1, keepdims=True)
        p_ref[:, cols] = (p / l).astype(BF16)
    return x + _dot(p_ref[...], vo_ref[...]) + bo_ref[...]


def _ffn_kernel(xp_ref, x_ref, xn_ref, g_ref, wu_ref, bu_ref, cw_ref, cb_ref, wd_ref, bd_ref, fg_ref,
                out_ref, hb_ref, u0_ref, u1_ref, u2_ref, u3_ref, acc_ref, wdb_ref, *, tm, sub, seq, d_ff, cn,
                final_norm):
    _cast_once(wd_ref, wdb_ref)
    u_bufs = ((u0_ref, u1_ref), (u2_ref, u3_ref))
    tiles = seq // tm
    t = pl.program_id(0) % tiles
    hl = FFN_HALO
    g = g_ref[...]
    hb_ref[...] = jnp.concatenate([_rms(xp_ref[...], g), _rms(x_ref[...], g), _rms(xn_ref[...], g)],
                                  axis=0).astype(BF16)
    n_sub = tm // sub
    rows = sub + 2 * hl
    n_chunks = d_ff // cn

    def conv(u_ref, ue, cols, a):
        bias = bu_ref[:, cols]
        c0, c1, c2 = cw_ref[0:1, cols], cw_ref[1:2, cols], cw_ref[2:3, cols]
        u_ref[0, 1:1 + rows, :] = ue
        u_ref[1, hl - 1:hl - 1 + rows, :] = ue
        if a == 0:
            u_ref[0, hl:hl + 1, :] = jnp.where(t > 0, ue[hl - 1:hl, :], -bias)
        if a == n_sub - 1:
            u_ref[1, 2 * hl + sub - 1:2 * hl + sub, :] = jnp.where(t < tiles - 1, ue[hl + sub:hl + sub + 1, :], -bias)
        return (c0 * u_ref[0, hl:hl + sub, :] + c1 * ue[hl:hl + sub, :] + c2 * u_ref[1, 2 * hl:2 * hl + sub, :]
                + (bias * (c0 + c1 + c2) + cb_ref[:, cols]))

    def up(a, j):
        hb = hb_ref[a * sub:a * sub + rows, :]
        return (_dot(hb, wu_ref[:, j * cn:(j + 1) * cn]), _dot(hb, wu_ref[:, d_ff + j * cn:d_ff + (j + 1) * cn]))

    def down(a, j, act):
        part = _dot(act, wdb_ref[j * cn:(j + 1) * cn, :])
        if j == 0:
            acc_ref[...] = part
        else:
            acc_ref[...] += part
        if j == n_chunks - 1:
            out_rows = slice(a * sub, (a + 1) * sub)
            y = x_ref[out_rows, :] + acc_ref[...] + bd_ref[...]
            if final_norm:
                y = _rms(y, fg_ref[...])
            out_ref[out_rows, :] = y

    items = [(a, j) for a in range(n_sub) for j in range(n_chunks)]
    ue = up(*items[0])
    act_prev = None
    for idx, (a, j) in enumerate(items):
        ue_next = up(*items[idx + 1]) if idx + 1 < len(items) else None
        if act_prev is not None:
            down(*items[idx - 1], act_prev)
        gate = conv(u_bufs[idx % 2][0], ue[0], slice(j * cn, (j + 1) * cn), a)
        val = conv(u_bufs[idx % 2][1], ue[1], slice(d_ff + j * cn, d_ff + (j + 1) * cn), a)
        act_prev = (gate * jax.nn.sigmoid(gate) * val).astype(BF16)
        ue = ue_next
    down(*items[-1], act_prev)


def _ffn(x, g, w_up, layer, b_up, conv_w, conv_b, w_down, b_down, final_g, *, tm, sub, seq, cn, final_norm):
    t, d = x.shape
    d_ff = w_down.shape[1]
    hl = FFN_HALO
    blk_per_tile = tm // hl
    n_hl = t // hl
    row = pl.BlockSpec((tm, d), lambda i: (i, 0))
    return pl.pallas_call(
        functools.partial(_ffn_kernel, tm=tm, sub=sub, seq=seq, d_ff=d_ff, cn=cn, final_norm=final_norm),
        grid=(t // tm,),
        in_specs=[pl.BlockSpec((hl, d), lambda i: (jnp.maximum(i * blk_per_tile - 1, 0), 0)),
                  row,
                  pl.BlockSpec((hl, d), lambda i: (jnp.minimum((i + 1) * blk_per_tile, n_hl - 1), 0)),
                  _layer_block(g, layer), _layer_block(w_up, layer), _layer_block(b_up, layer),
                  _layer_block(conv_w, layer), _layer_block(conv_b, layer), _layer_block(w_down, layer),
                  _layer_block(b_down, layer), _full((1, d))],
        out_specs=row,
        out_shape=jax.ShapeDtypeStruct((t, d), F32),
        scratch_shapes=([pltpu.VMEM((tm + 2 * hl, d), BF16)] + [pltpu.VMEM((2, sub + 3 * hl, cn), F32)] * 4
                        + [pltpu.VMEM((sub, d), F32), pltpu.VMEM((d_ff, d), BF16)]),
        compiler_params=_cparams(1, "arbitrary"),
        name="conv_ffn",
    )(x, x, x, g, w_up, b_up, conv_w, conv_b, w_down, b_down, final_g)


def _block_diag(blocks):
    *lead, n, a, b = blocks.shape
    eye = jnp.eye(n, dtype=blocks.dtype)
    return (eye[:, None, :, None] * blocks[..., :, :, None, :]).reshape(*lead, n * a, n * b)


def kernel(x, mem, rel_table, mem_norm_g, norm_mix_g, w_in, b_in, gmlp_v_g, gmlp_w_s, gmlp_b_s, pool_w, pool_b, pool_scale, w_out, b_out, norm_mem_g, xattn_w_q, xattn_w_kv, xattn_w_o, xattn_b_o, norm_ffn_g, ffn_w_up, ffn_b_up, ffn_conv_w, ffn_conv_b, ffn_w_down, ffn_b_down, final_norm_g):
    batch, seq, d = x.shape
    mem_len = mem.shape[1]
    depth = w_in.shape[0]
    t = batch * seq
    assert all(w // (2 * dil) == RADIUS for w, dil in DILATED_CONFIGS)

    rows = lambda a: a.reshape(depth, 1, -1)
    ones_bd = jnp.asarray(np.kron(np.eye(A_HEADS), np.ones((HEAD_DIM, HEAD_DIM))), BF16)
    biases = _attn_biases(rel_table)
    qk, vo = _mem_fold(mem, mem_norm_g.reshape(1, d), xattn_w_kv, xattn_w_q, xattn_w_o)
    w_up_bf16 = ffn_w_up.astype(BF16)
    ws_cat = (gmlp_w_s.reshape(depth, A_HEADS // 2, 2, GMLP_CHUNK, GMLP_CHUNK).transpose(0, 1, 3, 2, 4)
              .reshape(depth, A_HEADS // 2, GMLP_CHUNK, 2 * GMLP_CHUNK).astype(BF16))
    bs_full = jnp.repeat(jnp.swapaxes(gmlp_b_s, 1, 2), HEAD_DIM, axis=2)
    wp_bd = _block_diag(pool_w).astype(BF16)

    xf = x.reshape(t, d)
    for l in range(depth):
        ya, yb, *qkv = _mix_in(xf, rows(norm_mix_g), w_in, l, rows(b_in), rows(gmlp_v_g), ws_cat, bs_full, ones_bd,
                               wp_bd, rows(pool_b), rows(pool_scale), tm=1024, seq=seq)
        o, lse = [], []
        for g, (_, dil) in enumerate(DILATED_CONFIGS):
            og, lg = _attn(qkv[g], biases, g, dilation=dil, batch=batch, seq=seq, lq=2048 // min(dil, 8),
                           n_sub=8)
            o.append(og)
            lse.append(lg)
        xf = _mix_out(xf, ya, yb, o, lse, w_out, rows(b_out), rows(norm_mem_g), qk, vo, l, rows(xattn_b_o),
                      tm=1024, seq=seq)
        xf = _ffn(xf, rows(norm_ffn_g), w_up_bf16, l, rows(ffn_b_up), ffn_conv_w, rows(ffn_conv_b), ffn_w_down,
                  rows(ffn_b_down), final_norm_g.reshape(1, d),
                  tm=1024, sub=256, seq=seq, cn=256, final_norm=(l == depth - 1))
    return xf.reshape(batch, seq, d)
```

```python
import functools
import math

import numpy as np
import jax
import jax.numpy as jnp
from jax import lax
from jax.experimental import pallas as pl
from jax.experimental.pallas import tpu as pltpu

F32 = jnp.float32
BF16 = jnp.bfloat16

HEAD_DIM = 64
A_HEADS = 6
A_WIDTH = A_HEADS * HEAD_DIM
GMLP_CHUNK = 128
POOL_WINDOWS = (2, 4, 8, 16)
B_GROUP_DIM = 64
B_WIDTH = len(POOL_WINDOWS) * B_GROUP_DIM
DILATED_CONFIGS = ((128, 1), (512, 4), (2048, 16))
C_GROUPS = len(DILATED_CONFIGS)
C_GROUP_WIDTH = 2 * HEAD_DIM
C_WIDTH = C_GROUPS * C_GROUP_WIDTH
RADIUS = 64
N_BUCKETS = 32
MAX_DISTANCE = 1024
X_HEADS = 4
EPS = 1e-6
NEG_INF = -1e30

LANES = 128
POOL_HALO = 8
FFN_HALO = 8
VMEM_LIMIT = 56 * 1024 * 1024

MIX_IN_ROWS = 1024
MIX_OUT_ROWS = 1024
FFN_ROWS = 512
FFN_PASS_ROWS = 256
FFN_CHUNK = 256
ATTN_CHUNKS_PER_STEP = 16
ATTN_MAX_SUBSEQ = 8


def _cparams(n_axes, semantics="parallel"):
    return pltpu.CompilerParams(dimension_semantics=(semantics,) * n_axes, vmem_limit_bytes=VMEM_LIMIT)


def _layer_block(stacked, layer):
    return pl.BlockSpec((None,) + stacked.shape[1:], lambda *_: (layer,) + (0,) * (stacked.ndim - 1))


def _rms(x, g):
    ms = jnp.mean(x * x, axis=-1, keepdims=True)
    return x * lax.rsqrt(ms + EPS) * g


def _gelu_tanh(x):
    return x * (0.5 * (1.0 + jnp.tanh(math.sqrt(2.0 / math.pi) * (x + 0.044715 * (x * x * x)))))


def _dot(a, b):
    return jnp.dot(a, b, preferred_element_type=F32)


def _cast_once(src_ref, dst_ref):
    @pl.when(pl.program_id(0) == 0)
    def _():
        dst_ref[...] = src_ref[...].astype(BF16)


def _dot_nt(a, b):
    return lax.dot_general(a, b, (((1,), (1,)), ((), ())), preferred_element_type=F32)


def _full(shape):
    return pl.BlockSpec(shape, lambda *_: (0,) * len(shape))


def _pool_minus_identity(zx_ref, t, tm, seq):
    hl = POOL_HALO
    pos = t * tm + lax.broadcasted_iota(jnp.int32, (tm, LANES), 0)
    first_group = lax.broadcasted_iota(jnp.int32, (tm, LANES), 1) < B_GROUP_DIM
    pooled = []
    for cb in range(B_WIDTH // LANES):
        cols = slice(cb * LANES, (cb + 1) * LANES)
        half_a, half_b = POOL_WINDOWS[2 * cb] // 2, POOL_WINDOWS[2 * cb + 1] // 2

        def shifted(j):
            return zx_ref[hl + j:hl + j + tm, cols]

        acc = shifted(-1) + shifted(0)
        sums = {1: acc}
        for half in (2, 4, 8):
            if half > half_b:
                break
            for j in range(half // 2, half):
                acc = acc + shifted(-j - 1) + shifted(j)
            sums[half] = acc
        half = jnp.where(first_group, half_a, half_b)
        cnt = (jnp.minimum(pos + half, seq) - jnp.maximum(pos - half, 0)).astype(F32)
        wsum = jnp.where(first_group, sums[half_a], sums[half_b])
        pooled.append(wsum / cnt - shifted(0))
    return jnp.concatenate(pooled, axis=-1)


def _mix_in_kernel(xp_ref, x_ref, xn_ref, g_ref, w_ref, b_ref, vg_ref, ws_ref, bs_ref, ones_ref,
                   wp_ref, bp_ref, ps_ref,
                   ya_ref, yb_ref, g0_ref, g1_ref, g2_ref, qkv_ref, wb_ref, zx_ref, *, tm, seq):
    _cast_once(w_ref, wb_ref)
    tiles = seq // tm
    t = pl.program_id(0) % tiles
    g = g_ref[...]
    hb = _rms(x_ref[...], g).astype(BF16)

    def seg(a, b):
        return _dot(hb, wb_ref[:, a:b]) + b_ref[:, a:b]

    off_b = 2 * A_WIDTH
    off_c = off_b + B_WIDTH
    uv = seg(0, off_b)
    hl = POOL_HALO
    h_halo = jnp.concatenate([_rms(xp_ref[...], g), _rms(xn_ref[...], g)], axis=0).astype(BF16)
    z_halo = _dot(h_halo, wb_ref[:, off_b:off_c]) + b_ref[:, off_b:off_c]
    zx_ref[0:hl] = jnp.where(t > 0, z_halo[0:hl], 0.0)
    zx_ref[hl:hl + tm] = seg(off_b, off_c)
    zx_ref[hl + tm:] = jnp.where(t < tiles - 1, z_halo[hl:], 0.0)
    qk = seg(off_c, off_c + 2 * C_WIDTH)
    parts = (qk[:, :C_WIDTH] * (HEAD_DIM ** -0.5), qk[:, C_WIDTH:], seg(off_c + 2 * C_WIDTH, off_c + 3 * C_WIDTH))

    pooled = _pool_minus_identity(zx_ref, t, tm, seq).astype(BF16)
    yb_ref[...] = ((_dot(pooled, wp_ref[...]) + bp_ref[...]) * ps_ref[...]).astype(BF16)

    for grp, (out_ref, (_, dil)) in enumerate(zip((g0_ref, g1_ref, g2_ref), DILATED_CONFIGS)):
        for part in range(3):
            tile = parts[part][:, grp * C_GROUP_WIDTH:(grp + 1) * C_GROUP_WIDTH]
            if dil == 1:
                out_ref[part] = tile.astype(BF16)
                continue
            qkv_ref[part, grp] = tile
            for r in range(dil):
                piece = qkv_ref[part, grp, pl.ds(r, tm // dil, stride=dil), :]
                out_ref[part, :, r * C_GROUP_WIDTH:(r + 1) * C_GROUP_WIDTH] = piece.astype(BF16)

    uv = _gelu_tanh(uv)
    u, v = uv[:, :A_WIDTH], uv[:, A_WIDTH:]
    vsq = (v * v).astype(BF16)
    cut = 2 * LANES
    ssum = jnp.concatenate([_dot(vsq[:, :cut], ones_ref[:cut, :cut]), _dot(vsq[:, cut:], ones_ref[cut:, cut:])],
                           axis=1)
    vn = v * lax.rsqrt(ssum * (1.0 / HEAD_DIM) + EPS) * vg_ref[...]
    first_head = lax.broadcasted_iota(jnp.int32, (GMLP_CHUNK, LANES), 1) < HEAD_DIM
    chunk_rows = [slice(c * GMLP_CHUNK, (c + 1) * GMLP_CHUNK) for c in range(tm // GMLP_CHUNK)]
    for j in range(A_WIDTH // LANES):
        cols = slice(j * LANES, (j + 1) * LANES)
        rhs = jnp.concatenate(
            [jnp.concatenate([jnp.where(first_head, vn[rows, cols], 0.0).astype(BF16),
                              jnp.where(first_head, 0.0, vn[rows, cols]).astype(BF16)], axis=0)
             for rows in chunk_rows], axis=1)
        s = _dot(ws_ref[j], rhs)
        for c, rows in enumerate(chunk_rows):
            s_c = s[:, c * LANES:(c + 1) * LANES] + bs_ref[:, cols]
            ya_ref[rows, cols] = (u[rows, cols] * s_c).astype(BF16)


def _mix_in(x, g, w_in, layer, b_in, v_gain, ws_cat, bs_full, ones_bd, wp_bd, bp, ps, *, tm, seq):
    t, d = x.shape
    in_width = w_in.shape[2]
    hl = POOL_HALO
    blk_per_tile = tm // hl
    n_hl = t // hl
    qkv_shapes = [jax.ShapeDtypeStruct((3, t // dil, dil * C_GROUP_WIDTH), BF16) for _, dil in DILATED_CONFIGS]
    qkv_specs = [pl.BlockSpec((3, tm // dil, dil * C_GROUP_WIDTH), lambda i: (0, i, 0))
                 for _, dil in DILATED_CONFIGS]
    return pl.pallas_call(
        functools.partial(_mix_in_kernel, tm=tm, seq=seq),
        grid=(t // tm,),
        in_specs=[pl.BlockSpec((hl, d), lambda i: (jnp.maximum(i * blk_per_tile - 1, 0), 0)),
                  pl.BlockSpec((tm, d), lambda i: (i, 0)),
                  pl.BlockSpec((hl, d), lambda i: (jnp.minimum((i + 1) * blk_per_tile, n_hl - 1), 0)),
                  _layer_block(g, layer), _layer_block(w_in, layer), _layer_block(b_in, layer),
                  _layer_block(v_gain, layer), _layer_block(ws_cat, layer), _layer_block(bs_full, layer),
                  _full(ones_bd.shape), _layer_block(wp_bd, layer), _layer_block(bp, layer),
                  _layer_block(ps, layer)],
        out_specs=[pl.BlockSpec((tm, A_WIDTH), lambda i: (i, 0)),
                   pl.BlockSpec((tm, B_WIDTH), lambda i: (i, 0))] + qkv_specs,
        out_shape=[jax.ShapeDtypeStruct((t, A_WIDTH), BF16),
                   jax.ShapeDtypeStruct((t, B_WIDTH), BF16)] + qkv_shapes,
        scratch_shapes=[pltpu.VMEM((3, C_GROUPS, tm, C_GROUP_WIDTH), F32), pltpu.VMEM((d, in_width), BF16),
                        pltpu.VMEM((tm + 2 * hl, B_WIDTH), F32)],
        compiler_params=_cparams(1, "arbitrary"),
        name="mix_in",
    )(x, x, x, g, w_in, b_in, v_gain, ws_cat, bs_full, ones_bd, wp_bd, bp, ps)


def _attn_kernel(q_ref, kp_ref, k_ref, kn_ref, vp_ref, v_ref, vn_ref, bias_ref,
                 o_ref, lse_ref, kx_ref, vx_ref, *, lq, n_blocks, n_sub):
    i = pl.program_id(2)
    kx_ref[0:RADIUS] = kp_ref[...]
    kx_ref[RADIUS:RADIUS + lq] = k_ref[...]
    kx_ref[RADIUS + lq:] = kn_ref[...]
    vx_ref[0:RADIUS] = vp_ref[...]
    vx_ref[RADIUS:RADIUS + lq] = v_ref[...]
    vx_ref[RADIUS + lq:] = vn_ref[...]

    qc = 2 * RADIUS
    kc = qc + 2 * RADIUS
    n_chunks = lq // qc
    first_head = lax.broadcasted_iota(jnp.int32, (qc, LANES), 1) < HEAD_DIM
    for c in range(n_chunks):
        rows = slice(c * qc, (c + 1) * qc)
        variant = 1
        if c == 0:
            variant = jnp.where(i == 0, 0, 1)
        if c == n_chunks - 1:
            variant = jnp.where(i == n_blocks - 1, 2, variant)
        for r in range(n_sub):
            lanes = slice(r * LANES, (r + 1) * LANES)
            q = q_ref[rows, lanes]
            keys = kx_ref[c * qc:c * qc + kc, lanes]
            vals = vx_ref[c * qc:c * qc + kc, lanes]
            o_heads, lse_heads = [], []
            for h in range(2):
                qh = jnp.where(first_head if h == 0 else ~first_head, q, jnp.zeros_like(q))
                s = _dot_nt(qh, keys) + bias_ref[variant, h]
                m = jnp.max(s, axis=-1, keepdims=True)
                p = jnp.exp(s - m)
                l = jnp.sum(p, axis=-1, keepdims=True)
                o_heads.append(_dot(p.astype(BF16), vals) / l)
                lse_heads.append(jnp.broadcast_to(m + jnp.log(l), (qc, LANES)))
            o_ref[rows, lanes] = jnp.where(first_head, o_heads[0], o_heads[1])
            lse_ref[rows, lanes] = jnp.where(first_head, lse_heads[0], lse_heads[1])


def _attn(qkv, biases, group, *, dilation, batch, seq, lq, n_sub):
    sub_len = seq // dilation
    lq = min(lq, sub_len)
    n_sub = min(n_sub, dilation)
    n_blocks = sub_len // lq
    assert sub_len >= 4 * RADIUS
    qkv = qkv.reshape(3, batch, sub_len, dilation * C_GROUP_WIDTH)
    halo_per_blk = lq // RADIUS
    n_halo = sub_len // RADIUS
    width = n_sub * C_GROUP_WIDTH

    def main(part):
        return pl.BlockSpec((None, None, lq, width), lambda b, r, i: (part, b, i, r))

    def prev(part):
        return pl.BlockSpec((None, None, RADIUS, width),
                            lambda b, r, i: (part, b, jnp.maximum(i * halo_per_blk - 1, 0), r))

    def nxt(part):
        return pl.BlockSpec((None, None, RADIUS, width),
                            lambda b, r, i: (part, b, jnp.minimum((i + 1) * halo_per_blk, n_halo - 1), r))

    out_spec = pl.BlockSpec((None, lq, width), lambda b, r, i: (b, i, r))
    out_shape = jax.ShapeDtypeStruct((batch, sub_len, dilation * C_GROUP_WIDTH), F32)
    o, lse = pl.pallas_call(
        functools.partial(_attn_kernel, lq=lq, n_blocks=n_blocks, n_sub=n_sub),
        grid=(batch, dilation // n_sub, n_blocks),
        in_specs=[main(0), prev(1), main(1), nxt(1), prev(2), main(2), nxt(2),
                  pl.BlockSpec((None,) + biases.shape[1:], lambda b, r, i: (group, 0, 0, 0, 0))],
        out_specs=[out_spec, out_spec],
        out_shape=[out_shape, out_shape],
        scratch_shapes=[pltpu.VMEM((lq + 2 * RADIUS, width), BF16),
                        pltpu.VMEM((lq + 2 * RADIUS, width), BF16)],
        compiler_params=_cparams(3),
        name=f"dilated_attn_d{dilation}",
    )(qkv, qkv, qkv, qkv, qkv, qkv, qkv, biases)
    rows = batch * sub_len
    return o.reshape(rows, dilation * C_GROUP_WIDTH), lse.reshape(rows, dilation * C_GROUP_WIDTH)


def _t5_bucket(rel):
    nb = N_BUCKETS // 2
    ret = (rel > 0).astype(np.int32) * nb
    n = np.abs(rel)
    max_exact = nb // 2
    large = max_exact + (np.log(np.maximum(n, 1) / max_exact)
                         / math.log(MAX_DISTANCE / max_exact) * (nb - max_exact)).astype(np.int32)
    large = np.minimum(large, nb - 1)
    return ret + np.where(n < max_exact, n, large)


def _attn_biases(rel_table):
    qc, kc = 2 * RADIUS, 4 * RADIUS
    n = qc + kc - 1
    delta = (np.arange(n) + qc - 1) % n - (qc + RADIUS - 1)
    dils = np.array([dil for _, dil in DILATED_CONFIGS])
    buckets = _t5_bucket(delta[None, :] * dils[:, None])
    onehot = (buckets[..., None] == np.arange(N_BUCKETS)).astype(np.float32)
    table = rel_table.reshape(N_BUCKETS, C_GROUPS, 2)
    per_delta = jnp.einsum('gnb,bgh->ghn', onehot, table, precision=lax.Precision.HIGHEST)
    per_delta = jnp.where(np.abs(delta) <= RADIUS, per_delta, NEG_INF)
    reps = -(-(qc * (n - 1)) // n)
    band = jnp.tile(per_delta, (1, 1, reps))[..., :qc * (n - 1)].reshape(C_GROUPS, 2, qc, n - 1)[..., :kc]
    key = np.arange(kc)
    return jnp.stack([jnp.where(key < RADIUS, NEG_INF, band), band, jnp.where(key >= kc - RADIUS, NEG_INF, band)],
                     axis=1)


def _mix_out_kernel(x_ref, ya_ref, yb_ref, o0_ref, o1_ref, o2_ref, l0_ref, l1_ref, l2_ref,
                    wo_ref, bo_ref, xg_ref, qk_ref, vo_ref, xbo_ref,
                    out_ref, nat_ref, xp_ref, wob_ref, *, tm):
    _cast_once(wo_ref, wob_ref)

    dils = [dil for _, dil in DILATED_CONFIGS] * 2
    grp_refs = (o0_ref, o1_ref, o2_ref, l0_ref, l1_ref, l2_ref)
    for slot, (src_ref, dil) in enumerate(zip(grp_refs, dils)):
        for r in range(dil if dil > 1 else 0):
            nat_ref[slot, pl.ds(r, tm // dil, stride=dil), :] = src_ref[:, r * C_GROUP_WIDTH:(r + 1) * C_GROUP_WIDTH]

    def natural(slot):
        return grp_refs[slot][...] if dils[slot] == 1 else nat_ref[slot]

    l0, l1, l2 = natural(3), natural(4), natural(5)
    m = jnp.maximum(l0, jnp.maximum(l1, l2))
    e0, e1, e2 = jnp.exp(l0 - m), jnp.exp(l1 - m), jnp.exp(l2 - m)
    yc = (e0 * natural(0) + e1 * natural(1) + e2 * natural(2)) / (e0 + e1 + e2)

    y = (_dot(ya_ref[...], wob_ref[0:A_WIDTH, :])
         + _dot(yb_ref[...], wob_ref[A_WIDTH:A_WIDTH + B_WIDTH, :])
         + _dot(yc.astype(BF16), wob_ref[A_WIDTH + B_WIDTH:, :]))
    x1 = x_ref[...] + y + bo_ref[...]
    out_ref[...] = _cross_attend(x1, xg_ref, qk_ref, vo_ref, xbo_ref, xp_ref)


def _mix_out(x, ya, yb, o, lse, w_out, b_out, xg, qk, vo, layer, xb_o, *, tm, seq):
    t, d = x.shape
    tiles = seq // tm
    hm = qk.shape[-1]
    row = lambda w: pl.BlockSpec((tm, w), lambda i: (i, 0))
    grp = [pl.BlockSpec((tm // dil, dil * C_GROUP_WIDTH), lambda i: (i, 0)) for _, dil in DILATED_CONFIGS]
    return pl.pallas_call(
        functools.partial(_mix_out_kernel, tm=tm),
        grid=(t // tm,),
        in_specs=[row(d), row(A_WIDTH), row(B_WIDTH), *grp, *grp,
                  _layer_block(w_out, layer), _layer_block(b_out, layer), _layer_block(xg, layer),
                  pl.BlockSpec((None, None, d, hm), lambda i: (layer, i // tiles, 0, 0)),
                  pl.BlockSpec((None, None, hm, d), lambda i: (layer, i // tiles, 0, 0)),
                  _layer_block(xb_o, layer)],
        out_specs=row(d),
        out_shape=jax.ShapeDtypeStruct((t, d), F32),
        scratch_shapes=[pltpu.VMEM((2 * C_GROUPS, tm, C_GROUP_WIDTH), F32),
                        pltpu.VMEM((tm, hm), BF16),
                        pltpu.VMEM(w_out.shape[1:], BF16)],
        compiler_params=_cparams(1, "arbitrary"),
        name="mix_out_xattn",
    )(x, ya, yb, o[0], o[1], o[2], lse[0], lse[1], lse[2], w_out, b_out, xg, qk, vo, xb_o)


def _mem_fold_kernel(mem_ref, g_ref, wkv_ref, wq_ref, wo_ref, qk_ref, vo_ref):
    d = mem_ref.shape[1]
    m = mem_ref.shape[0]
    hd = d // X_HEADS
    mem_n = _rms(mem_ref[...], g_ref[...]).astype(BF16)
    kv = _dot(mem_n, wkv_ref[...].astype(BF16)).astype(BF16)
    for h in range(X_HEADS):
        cols = slice(h * hd, (h + 1) * hd)
        wq_h = wq_ref[:, cols].astype(BF16)
        qk_ref[:, h * m:(h + 1) * m] = (_dot_nt(wq_h, kv[:, cols]) * (hd ** -0.5)).astype(BF16)
        wo_h = wo_ref[cols, :].astype(BF16)
        vo_ref[h * m:(h + 1) * m, :] = _dot(kv[:, d + h * hd:d + (h + 1) * hd], wo_h).astype(BF16)


def _mem_fold(mem, g, w_kv, w_q, w_o):
    batch, m, d = mem.shape
    depth = w_kv.shape[0]

    def per_layer(stack):
        return pl.BlockSpec((None,) + stack.shape[1:], lambda l, b: (l, 0, 0))

    return pl.pallas_call(
        _mem_fold_kernel,
        grid=(depth, batch),
        in_specs=[pl.BlockSpec((None, m, d), lambda l, b: (b, 0, 0)), pl.BlockSpec((1, d), lambda l, b: (0, 0)),
                  per_layer(w_kv), per_layer(w_q), per_layer(w_o)],
        out_specs=[pl.BlockSpec((None, None, d, X_HEADS * m), lambda l, b: (l, b, 0, 0)),
                   pl.BlockSpec((None, None, X_HEADS * m, d), lambda l, b: (l, b, 0, 0))],
        out_shape=[jax.ShapeDtypeStruct((depth, batch, d, X_HEADS * m), BF16),
                   jax.ShapeDtypeStruct((depth, batch, X_HEADS * m, d), BF16)],
        compiler_params=_cparams(2),
        name="mem_fold",
    )(mem, g, w_kv, w_q, w_o)


def _cross_attend(x, g_ref, qk_ref, vo_ref, bo_ref, p_ref):
    hb = _rms(x, g_ref[...]).astype(BF16)
    s_all = _dot(hb, qk_ref[...])
    m_len = qk_ref.shape[1] // X_HEADS
    for h in range(X_HEADS):
        cols = slice(h * m_len, (h + 1) * m_len)
        s = s_all[:, cols]
        m = jnp.max(s, axis=-1, keepdims=True)
        p = jnp.exp(s - m)
        l = jnp.sum(p, axis=-1, keepdims=True)
        p_ref[:, cols] = (p / l).astype(BF16)
    return x + _dot(p_ref[...], vo_ref[...]) + bo_ref[...]


def _ffn_kernel(xp_ref, x_ref, xn_ref, g_ref, wu_ref, bu_ref, cw_ref, cb_ref, wd_ref, bd_ref, fg_ref,
                out_ref, hb_ref, u0_ref, u1_ref, u2_ref, u3_ref, acc_ref, wdb_ref, *, tm, sub, seq, d_ff, cn,
                final_norm):
    _cast_once(wd_ref, wdb_ref)
    u_bufs = ((u0_ref, u1_ref), (u2_ref, u3_ref))
    tiles = seq // tm
    t = pl.program_id(0) % tiles
    hl = FFN_HALO
    g = g_ref[...]
    hb_ref[...] = jnp.concatenate([_rms(xp_ref[...], g), _rms(x_ref[...], g), _rms(xn_ref[...], g)],
                                  axis=0).astype(BF16)
    n_sub = tm // sub
    rows = sub + 2 * hl
    n_chunks = d_ff // cn

    def conv(u_ref, ue, cols, a):
        bias = bu_ref[:, cols]
        c0, c1, c2 = cw_ref[0:1, cols], cw_ref[1:2, cols], cw_ref[2:3, cols]
        u_ref[0, 1:1 + rows, :] = ue
        u_ref[1, hl - 1:hl - 1 + rows, :] = ue
        if a == 0:
            u_ref[0, hl:hl + 1, :] = jnp.where(t > 0, ue[hl - 1:hl, :], -bias)
        if a == n_sub - 1:
            u_ref[1, 2 * hl + sub - 1:2 * hl + sub, :] = jnp.where(t < tiles - 1, ue[hl + sub:hl + sub + 1, :], -bias)
        return (c0 * u_ref[0, hl:hl + sub, :] + c1 * ue[hl:hl + sub, :] + c2 * u_ref[1, 2 * hl:2 * hl + sub, :]
                + (bias * (c0 + c1 + c2) + cb_ref[:, cols]))

    def up(a, j):
        hb = hb_ref[a * sub:a * sub + rows, :]
        return (_dot(hb, wu_ref[:, j * cn:(j + 1) * cn]), _dot(hb, wu_ref[:, d_ff + j * cn:d_ff + (j + 1) * cn]))

    def down(a, j, act):
        part = _dot(act, wdb_ref[j * cn:(j + 1) * cn, :])
        if j == 0:
            acc_ref[...] = part
        else:
            acc_ref[...] += part
        if j == n_chunks - 1:
            out_rows = slice(a * sub, (a + 1) * sub)
            y = x_ref[out_rows, :] + acc_ref[...] + bd_ref[...]
            if final_norm:
                y = _rms(y, fg_ref[...])
            out_ref[out_rows, :] = y

    items = [(a, j) for a in range(n_sub) for j in range(n_chunks)]
    ue = up(*items[0])
    act_prev = None
    for idx, (a, j) in enumerate(items):
        ue_next = up(*items[idx + 1]) if idx + 1 < len(items) else None
        if act_prev is not None:
            down(*items[idx - 1], act_prev)
        gate = conv(u_bufs[idx % 2][0], ue[0], slice(j * cn, (j + 1) * cn), a)
        val = conv(u_bufs[idx % 2][1], ue[1], slice(d_ff + j * cn, d_ff + (j + 1) * cn), a)
        act_prev = (gate * jax.nn.sigmoid(gate) * val).astype(BF16)
        ue = ue_next
    down(*items[-1], act_prev)


def _ffn(x, g, w_up, layer, b_up, conv_w, conv_b, w_down, b_down, final_g, *, tm, sub, seq, cn, final_norm):
    t, d = x.shape
    d_ff = w_down.shape[1]
    hl = FFN_HALO
    blk_per_tile = tm // hl
    n_hl = t // hl
    row = pl.BlockSpec((tm, d), lambda i: (i, 0))
    return pl.pallas_call(
        functools.partial(_ffn_kernel, tm=tm, sub=sub, seq=seq, d_ff=d_ff, cn=cn, final_norm=final_norm),
        grid=(t // tm,),
        in_specs=[pl.BlockSpec((hl, d), lambda i: (jnp.maximum(i * blk_per_tile - 1, 0), 0)),
                  row,
                  pl.BlockSpec((hl, d), lambda i: (jnp.minimum((i + 1) * blk_per_tile, n_hl - 1), 0)),
                  _layer_block(g, layer), _layer_block(w_up, layer), _layer_block(b_up, layer),
                  _layer_block(conv_w, layer), _layer_block(conv_b, layer), _layer_block(w_down, layer),
                  _layer_block(b_down, layer), _full((1, d))],
        out_specs=row,
        out_shape=jax.ShapeDtypeStruct((t, d), F32),
        scratch_shapes=([pltpu.VMEM((tm + 2 * hl, d), BF16)] + [pltpu.VMEM((2, sub + 3 * hl, cn), F32)] * 4
                        + [pltpu.VMEM((sub, d), F32), pltpu.VMEM((d_ff, d), BF16)]),
        compiler_params=_cparams(1, "arbitrary"),
        name="conv_ffn",
    )(x, x, x, g, w_up, b_up, conv_w, conv_b, w_down, b_down, final_g)


def _block_diag(blocks):
    *lead, n, a, b = blocks.shape
    eye = jnp.eye(n, dtype=blocks.dtype)
    return (eye[:, None, :, None] * blocks[..., :, :, None, :]).reshape(*lead, n * a, n * b)


def kernel(x, mem, rel_table, mem_norm_g, norm_mix_g, w_in, b_in, gmlp_v_g, gmlp_w_s, gmlp_b_s, pool_w, pool_b, pool_scale, w_out, b_out, norm_mem_g, xattn_w_q, xattn_w_kv, xattn_w_o, xattn_b_o, norm_ffn_g, ffn_w_up, ffn_b_up, ffn_conv_w, ffn_conv_b, ffn_w_down, ffn_b_down, final_norm_g):
    batch, seq, d = x.shape
    mem_len = mem.shape[1]
    depth = w_in.shape[0]
    t = batch * seq
    assert all(w // (2 * dil) == RADIUS for w, dil in DILATED_CONFIGS)
    assert x.dtype == F32 and all(seq % rows == 0 for rows in (MIX_IN_ROWS, MIX_OUT_ROWS, FFN_ROWS))
    assert FFN_ROWS % FFN_PASS_ROWS == 0 and ffn_w_down.shape[1] % FFN_CHUNK == 0

    rows = lambda a: a.reshape(depth, 1, -1)
    ones_bd = jnp.asarray(np.kron(np.eye(A_HEADS), np.ones((HEAD_DIM, HEAD_DIM))), BF16)
    biases = _attn_biases(rel_table)
    qk, vo = _mem_fold(mem, mem_norm_g.reshape(1, d), xattn_w_kv, xattn_w_q, xattn_w_o)
    w_up_bf16 = ffn_w_up.astype(BF16)
    ws_cat = (gmlp_w_s.reshape(depth, A_HEADS // 2, 2, GMLP_CHUNK, GMLP_CHUNK).transpose(0, 1, 3, 2, 4)
              .reshape(depth, A_HEADS // 2, GMLP_CHUNK, 2 * GMLP_CHUNK).astype(BF16))
    bs_full = jnp.repeat(jnp.swapaxes(gmlp_b_s, 1, 2), HEAD_DIM, axis=2)
    wp_bd = _block_diag(pool_w).astype(BF16)

    xf = x.reshape(t, d)
    for l in range(depth):
        ya, yb, *qkv = _mix_in(xf, rows(norm_mix_g), w_in, l, rows(b_in), rows(gmlp_v_g), ws_cat, bs_full, ones_bd,
                               wp_bd, rows(pool_b), rows(pool_scale), tm=MIX_IN_ROWS, seq=seq)
        o, lse = [], []
        for g, (_, dil) in enumerate(DILATED_CONFIGS):
            n_sub = min(dil, ATTN_MAX_SUBSEQ)
            og, lg = _attn(qkv[g], biases, g, dilation=dil, batch=batch, seq=seq,
                           lq=ATTN_CHUNKS_PER_STEP * 2 * RADIUS // n_sub, n_sub=n_sub)
            o.append(og)
            lse.append(lg)
        xf = _mix_out(xf, ya, yb, o, lse, w_out, rows(b_out), rows(norm_mem_g), qk, vo, l, rows(xattn_b_o),
                      tm=MIX_OUT_ROWS, seq=seq)
        xf = _ffn(xf, rows(norm_ffn_g), w_up_bf16, l, rows(ffn_b_up), ffn_conv_w, rows(ffn_conv_b), ffn_w_down,
                  rows(ffn_b_down), final_norm_g.reshape(1, d),
                  tm=FFN_ROWS, sub=FFN_PASS_ROWS, seq=seq, cn=FFN_CHUNK, final_norm=(l == depth - 1))
    return xf.reshape(batch, seq, d)
```

```python
import functools
import math

import numpy as np
import jax
import jax.numpy as jnp
from jax import lax
from jax.experimental import pallas as pl
from jax.experimental.pallas import tpu as pltpu

F32 = jnp.float32
BF16 = jnp.bfloat16

HEAD_DIM = 64
A_HEADS = 6
A_WIDTH = A_HEADS * HEAD_DIM
GMLP_CHUNK = 128
POOL_WINDOWS = (2, 4, 8, 16)
B_GROUP_DIM = 64
B_WIDTH = len(POOL_WINDOWS) * B_GROUP_DIM
DILATED_CONFIGS = ((128, 1), (512, 4), (2048, 16))
C_GROUPS = len(DILATED_CONFIGS)
C_GROUP_WIDTH = 2 * HEAD_DIM
C_WIDTH = C_GROUPS * C_GROUP_WIDTH
RADIUS = 64
N_BUCKETS = 32
MAX_DISTANCE = 1024
X_HEADS = 4
EPS = 1e-6
NEG_INF = -1e30

LANES = 128
POOL_HALO = 8
FFN_HALO = 8
VMEM_LIMIT = 56 * 1024 * 1024

MIX_IN_ROWS = 1024
MIX_OUT_ROWS = 1024
FFN_ROWS = 512
FFN_PASS_ROWS = 256
FFN_CHUNK = 256
ATTN_CHUNKS_PER_STEP = 32
ATTN_MAX_SUBSEQ = 16


def _cparams(n_axes, semantics="parallel"):
    return pltpu.CompilerParams(dimension_semantics=(semantics,) * n_axes, vmem_limit_bytes=VMEM_LIMIT)


def _layer_block(stacked, layer):
    return pl.BlockSpec((None,) + stacked.shape[1:], lambda *_: (layer,) + (0,) * (stacked.ndim - 1))


def _rms(x, g):
    ms = jnp.mean(x * x, axis=-1, keepdims=True)
    return x * lax.rsqrt(ms + EPS) * g


def _gelu_tanh(x):
    return x * (0.5 * (1.0 + jnp.tanh(math.sqrt(2.0 / math.pi) * (x + 0.044715 * (x * x * x)))))


def _dot(a, b):
    return jnp.dot(a, b, preferred_element_type=F32)


def _cast_once(src_ref, dst_ref):
    @pl.when(pl.program_id(0) == 0)
    def _():
        dst_ref[...] = src_ref[...].astype(BF16)


def _dot_nt(a, b):
    return lax.dot_general(a, b, (((1,), (1,)), ((), ())), preferred_element_type=F32)


def _full(shape):
    return pl.BlockSpec(shape, lambda *_: (0,) * len(shape))


def _pool_minus_identity(z_prev, z_main, z_next, zx_ref, ts_ref, t, tiles, tm, seq):
    hl = POOL_HALO
    pad = hl
    ext = tm + 2 * hl
    assert POOL_WINDOWS == tuple(2 ** (k + 1) for k in range(len(POOL_WINDOWS))) and POOL_WINDOWS[-1] // 2 <= pad
    zx_ref[0:pad] = jnp.zeros((pad, B_WIDTH), F32)
    zx_ref[pad:pad + hl] = jnp.where(t > 0, z_prev, 0.0)
    zx_ref[pad + hl:pad + hl + tm] = z_main
    zx_ref[pad + hl + tm:] = jnp.where(t < tiles - 1, z_next, 0.0)
    ts_ref[:, 0:pad] = jnp.zeros((len(POOL_WINDOWS), pad, B_WIDTH), F32)

    pos = t * tm + lax.broadcasted_iota(jnp.int32, (tm, LANES), 0)
    first_group = lax.broadcasted_iota(jnp.int32, (tm, LANES), 1) < B_GROUP_DIM
    pooled = []
    for cb in range(B_WIDTH // LANES):
        cols = slice(cb * LANES, (cb + 1) * LANES)
        w_a, w_b = POOL_WINDOWS[2 * cb], POOL_WINDOWS[2 * cb + 1]
        centred = {}
        for k, w in enumerate(POOL_WINDOWS):
            if w > w_b:
                break
            src = zx_ref if k == 0 else ts_ref.at[k - 1]
            shift = w // 2
            ts_ref[k, pad:pad + ext, cols] = src[pad:pad + ext, cols] + src[pad - shift:pad - shift + ext, cols]
            if w in (w_a, w_b):
                start = pad + hl + w // 2 - 1
                centred[w] = ts_ref[k, start:start + tm, cols]
        half = jnp.where(first_group, w_a // 2, w_b // 2)
        cnt = (jnp.minimum(pos + half, seq) - jnp.maximum(pos - half, 0)).astype(F32)
        wsum = jnp.where(first_group, centred[w_a], centred[w_b])
        pooled.append(wsum / cnt - zx_ref[pad + hl:pad + hl + tm, cols])
    return jnp.concatenate(pooled, axis=-1)


def _mix_in_kernel(xp_ref, x_ref, xn_ref, g_ref, w_ref, b_ref, vg_ref, ws_ref, bs_ref, ones_ref,
                   wp_ref, bp_ref, ps_ref,
                   ya_ref, yb_ref, g0_ref, g1_ref, g2_ref, qkv_ref, wb_ref, zx_ref, ts_ref, *, tm, seq):
    _cast_once(w_ref, wb_ref)
    tiles = seq // tm
    t = pl.program_id(0) % tiles
    g = g_ref[...]
    hb = _rms(x_ref[...], g).astype(BF16)

    def seg(a, b):
        return _dot(hb, wb_ref[:, a:b]) + b_ref[:, a:b]

    off_b = 2 * A_WIDTH
    off_c = off_b + B_WIDTH
    uv = seg(0, off_b)
    hl = POOL_HALO
    h_halo = jnp.concatenate([_rms(xp_ref[...], g), _rms(xn_ref[...], g)], axis=0).astype(BF16)
    z_halo = _dot(h_halo, wb_ref[:, off_b:off_c]) + b_ref[:, off_b:off_c]
    z_main = seg(off_b, off_c)
    qk = seg(off_c, off_c + 2 * C_WIDTH)
    parts = (qk[:, :C_WIDTH] * (HEAD_DIM ** -0.5), qk[:, C_WIDTH:], seg(off_c + 2 * C_WIDTH, off_c + 3 * C_WIDTH))

    pooled = _pool_minus_identity(z_halo[0:hl], z_main, z_halo[hl:], zx_ref, ts_ref, t, tiles, tm, seq).astype(BF16)
    yb_ref[...] = ((_dot(pooled, wp_ref[...]) + bp_ref[...]) * ps_ref[...]).astype(BF16)

    for grp, (out_ref, (_, dil)) in enumerate(zip((g0_ref, g1_ref, g2_ref), DILATED_CONFIGS)):
        for part in range(3):
            tile = parts[part][:, grp * C_GROUP_WIDTH:(grp + 1) * C_GROUP_WIDTH]
            if dil == 1:
                out_ref[part] = tile.astype(BF16)
                continue
            qkv_ref[part, grp] = tile
            for r in range(dil):
                piece = qkv_ref[part, grp, pl.ds(r, tm // dil, stride=dil), :]
                out_ref[part, :, r * C_GROUP_WIDTH:(r + 1) * C_GROUP_WIDTH] = piece.astype(BF16)

    uv = _gelu_tanh(uv)
    u, v = uv[:, :A_WIDTH], uv[:, A_WIDTH:]
    vsq = (v * v).astype(BF16)
    cut = 2 * LANES
    ssum = jnp.concatenate([_dot(vsq[:, :cut], ones_ref[:cut, :cut]), _dot(vsq[:, cut:], ones_ref[cut:, cut:])],
                           axis=1)
    vn = v * lax.rsqrt(ssum * (1.0 / HEAD_DIM) + EPS) * vg_ref[...]
    first_head = lax.broadcasted_iota(jnp.int32, (GMLP_CHUNK, LANES), 1) < HEAD_DIM
    chunk_rows = [slice(c * GMLP_CHUNK, (c + 1) * GMLP_CHUNK) for c in range(tm // GMLP_CHUNK)]
    for j in range(A_WIDTH // LANES):
        cols = slice(j * LANES, (j + 1) * LANES)
        rhs = jnp.concatenate(
            [jnp.concatenate([jnp.where(first_head, vn[rows, cols], 0.0).astype(BF16),
                              jnp.where(first_head, 0.0, vn[rows, cols]).astype(BF16)], axis=0)
             for rows in chunk_rows], axis=1)
        s = _dot(ws_ref[j], rhs)
        for c, rows in enumerate(chunk_rows):
            s_c = s[:, c * LANES:(c + 1) * LANES] + bs_ref[:, cols]
            ya_ref[rows, cols] = (u[rows, cols] * s_c).astype(BF16)


def _mix_in(x, g, w_in, layer, b_in, v_gain, ws_cat, bs_full, ones_bd, wp_bd, bp, ps, *, tm, seq):
    t, d = x.shape
    in_width = w_in.shape[2]
    hl = POOL_HALO
    blk_per_tile = tm // hl
    n_hl = t // hl
    qkv_shapes = [jax.ShapeDtypeStruct((3, t // dil, dil * C_GROUP_WIDTH), BF16) for _, dil in DILATED_CONFIGS]
    qkv_specs = [pl.BlockSpec((3, tm // dil, dil * C_GROUP_WIDTH), lambda i: (0, i, 0))
                 for _, dil in DILATED_CONFIGS]
    return pl.pallas_call(
        functools.partial(_mix_in_kernel, tm=tm, seq=seq),
        grid=(t // tm,),
        in_specs=[pl.BlockSpec((hl, d), lambda i: (jnp.maximum(i * blk_per_tile - 1, 0), 0)),
                  pl.BlockSpec((tm, d), lambda i: (i, 0)),
                  pl.BlockSpec((hl, d), lambda i: (jnp.minimum((i + 1) * blk_per_tile, n_hl - 1), 0)),
                  _layer_block(g, layer), _layer_block(w_in, layer), _layer_block(b_in, layer),
                  _layer_block(v_gain, layer), _layer_block(ws_cat, layer), _layer_block(bs_full, layer),
                  _full(ones_bd.shape), _layer_block(wp_bd, layer), _layer_block(bp, layer),
                  _layer_block(ps, layer)],
        out_specs=[pl.BlockSpec((tm, A_WIDTH), lambda i: (i, 0)),
                   pl.BlockSpec((tm, B_WIDTH), lambda i: (i, 0))] + qkv_specs,
        out_shape=[jax.ShapeDtypeStruct((t, A_WIDTH), BF16),
                   jax.ShapeDtypeStruct((t, B_WIDTH), BF16)] + qkv_shapes,
        scratch_shapes=[pltpu.VMEM((3, C_GROUPS, tm, C_GROUP_WIDTH), F32), pltpu.VMEM((d, in_width), BF16),
                        pltpu.VMEM((tm + 3 * hl, B_WIDTH), F32),
                        pltpu.VMEM((len(POOL_WINDOWS), tm + 3 * hl, B_WIDTH), F32)],
        compiler_params=_cparams(1, "arbitrary"),
        name="mix_in",
    )(x, x, x, g, w_in, b_in, v_gain, ws_cat, bs_full, ones_bd, wp_bd, bp, ps)


def _attn_kernel(q_ref, kp_ref, k_ref, kn_ref, vp_ref, v_ref, vn_ref, bias_ref,
                 o_ref, lse_ref, kx_ref, vx_ref, *, lq, n_blocks, n_sub):
    i = pl.program_id(2)
    kx_ref[0:RADIUS] = kp_ref[...]
    kx_ref[RADIUS:RADIUS + lq] = k_ref[...]
    kx_ref[RADIUS + lq:] = kn_ref[...]
    vx_ref[0:RADIUS] = vp_ref[...]
    vx_ref[RADIUS:RADIUS + lq] = v_ref[...]
    vx_ref[RADIUS + lq:] = vn_ref[...]

    qc = 2 * RADIUS
    kc = qc + 2 * RADIUS
    n_chunks = lq // qc
    first_head = lax.broadcasted_iota(jnp.int32, (qc, LANES), 1) < HEAD_DIM
    for c in range(n_chunks):
        rows = slice(c * qc, (c + 1) * qc)
        variant = 1
        if c == 0:
            variant = jnp.where(i == 0, 0, 1)
        if c == n_chunks - 1:
            variant = jnp.where(i == n_blocks - 1, 2, variant)
        for r in range(n_sub):
            lanes = slice(r * LANES, (r + 1) * LANES)
            q = q_ref[rows, lanes]
            keys = kx_ref[c * qc:c * qc + kc, lanes]
            vals = vx_ref[c * qc:c * qc + kc, lanes]
            zero = jnp.zeros_like(q)
            q2 = jnp.concatenate([jnp.where(first_head, q, zero), jnp.where(first_head, zero, q)], axis=0)
            s = _dot_nt(q2, keys) + bias_ref[variant]
            m = jnp.max(s, axis=-1, keepdims=True)
            p = jnp.exp(s - m).astype(BF16)
            o2 = _dot(p, jnp.concatenate([vals, jnp.ones_like(vals)], axis=1))

            def per_lane_head(a):
                a = jnp.broadcast_to(a, (2 * qc, LANES))
                return jnp.where(first_head, a[:qc], a[qc:])

            l_both = per_lane_head(o2[:, LANES:])
            o_ref[rows, lanes] = per_lane_head(o2[:, :LANES]) / l_both
            lse_ref[rows, lanes] = per_lane_head(m) + jnp.log(l_both)


def _attn(qkv, biases, group, *, dilation, batch, seq, lq, n_sub):
    sub_len = seq // dilation
    lq = min(lq, sub_len)
    n_sub = min(n_sub, dilation)
    n_blocks = sub_len // lq
    assert sub_len >= 4 * RADIUS
    qkv = qkv.reshape(3, batch, sub_len, dilation * C_GROUP_WIDTH)
    halo_per_blk = lq // RADIUS
    n_halo = sub_len // RADIUS
    width = n_sub * C_GROUP_WIDTH

    def main(part):
        return pl.BlockSpec((None, None, lq, width), lambda b, r, i: (part, b, i, r))

    def prev(part):
        return pl.BlockSpec((None, None, RADIUS, width),
                            lambda b, r, i: (part, b, jnp.maximum(i * halo_per_blk - 1, 0), r))

    def nxt(part):
        return pl.BlockSpec((None, None, RADIUS, width),
                            lambda b, r, i: (part, b, jnp.minimum((i + 1) * halo_per_blk, n_halo - 1), r))

    out_spec = pl.BlockSpec((None, lq, width), lambda b, r, i: (b, i, r))
    out_shape = jax.ShapeDtypeStruct((batch, sub_len, dilation * C_GROUP_WIDTH), F32)
    o, lse = pl.pallas_call(
        functools.partial(_attn_kernel, lq=lq, n_blocks=n_blocks, n_sub=n_sub),
        grid=(batch, dilation // n_sub, n_blocks),
        in_specs=[main(0), prev(1), main(1), nxt(1), prev(2), main(2), nxt(2),
                  pl.BlockSpec((None,) + biases.shape[1:], lambda b, r, i: (group, 0, 0, 0))],
        out_specs=[out_spec, out_spec],
        out_shape=[out_shape, out_shape],
        scratch_shapes=[pltpu.VMEM((lq + 2 * RADIUS, width), BF16),
                        pltpu.VMEM((lq + 2 * RADIUS, width), BF16)],
        compiler_params=_cparams(3),
        name=f"dilated_attn_d{dilation}",
    )(qkv, qkv, qkv, qkv, qkv, qkv, qkv, biases)
    rows = batch * sub_len
    return o.reshape(rows, dilation * C_GROUP_WIDTH), lse.reshape(rows, dilation * C_GROUP_WIDTH)


def _t5_bucket(rel):
    nb = N_BUCKETS // 2
    ret = (rel > 0).astype(np.int32) * nb
    n = np.abs(rel)
    max_exact = nb // 2
    large = max_exact + (np.log(np.maximum(n, 1) / max_exact)
                         / math.log(MAX_DISTANCE / max_exact) * (nb - max_exact)).astype(np.int32)
    large = np.minimum(large, nb - 1)
    return ret + np.where(n < max_exact, n, large)


def _attn_biases(rel_table):
    qc, kc = 2 * RADIUS, 4 * RADIUS
    n = qc + kc - 1
    delta = (np.arange(n) + qc - 1) % n - (qc + RADIUS - 1)
    dils = np.array([dil for _, dil in DILATED_CONFIGS])
    buckets = _t5_bucket(delta[None, :] * dils[:, None])
    onehot = (buckets[..., None] == np.arange(N_BUCKETS)).astype(np.float32)
    table = rel_table.reshape(N_BUCKETS, C_GROUPS, 2)
    per_delta = jnp.einsum('gnb,bgh->ghn', onehot, table, precision=lax.Precision.HIGHEST)
    per_delta = jnp.where(np.abs(delta) <= RADIUS, per_delta, NEG_INF)
    reps = -(-(qc * (n - 1)) // n)
    band = jnp.tile(per_delta, (1, 1, reps))[..., :qc * (n - 1)].reshape(C_GROUPS, 2, qc, n - 1)[..., :kc]
    key = np.arange(kc)
    variants = jnp.stack([jnp.where(key < RADIUS, NEG_INF, band), band,
                          jnp.where(key >= kc - RADIUS, NEG_INF, band)], axis=1)
    return variants.reshape(C_GROUPS, 3, 2 * qc, kc)


def _mix_out_kernel(x_ref, ya_ref, yb_ref, o0_ref, o1_ref, o2_ref, l0_ref, l1_ref, l2_ref,
                    wo_ref, bo_ref, xg_ref, qk_ref, vo_ref, xbo_ref,
                    out_ref, nat_ref, xp_ref, wob_ref, *, tm):
    _cast_once(wo_ref, wob_ref)

    dils = [dil for _, dil in DILATED_CONFIGS] * 2
    grp_refs = (o0_ref, o1_ref, o2_ref, l0_ref, l1_ref, l2_ref)
    for slot, (src_ref, dil) in enumerate(zip(grp_refs, dils)):
        for r in range(dil if dil > 1 else 0):
            nat_ref[slot, pl.ds(r, tm // dil, stride=dil), :] = src_ref[:, r * C_GROUP_WIDTH:(r + 1) * C_GROUP_WIDTH]

    def natural(slot):
        return grp_refs[slot][...] if dils[slot] == 1 else nat_ref[slot]

    l0, l1, l2 = natural(3), natural(4), natural(5)
    m = jnp.maximum(l0, jnp.maximum(l1, l2))
    e0, e1, e2 = jnp.exp(l0 - m), jnp.exp(l1 - m), jnp.exp(l2 - m)
    yc = (e0 * natural(0) + e1 * natural(1) + e2 * natural(2)) / (e0 + e1 + e2)

    y = (_dot(ya_ref[...], wob_ref[0:A_WIDTH, :])
         + _dot(yb_ref[...], wob_ref[A_WIDTH:A_WIDTH + B_WIDTH, :])
         + _dot(yc.astype(BF16), wob_ref[A_WIDTH + B_WIDTH:, :]))
    x1 = x_ref[...] + y + bo_ref[...]
    out_ref[...] = _cross_attend(x1, xg_ref, qk_ref, vo_ref, xbo_ref, xp_ref)


def _mix_out(x, ya, yb, o, lse, w_out, b_out, xg, qk, vo, layer, xb_o, *, tm, seq):
    t, d = x.shape
    tiles = seq // tm
    hm = qk.shape[-1]
    row = lambda w: pl.BlockSpec((tm, w), lambda i: (i, 0))
    grp = [pl.BlockSpec((tm // dil, dil * C_GROUP_WIDTH), lambda i: (i, 0)) for _, dil in DILATED_CONFIGS]
    return pl.pallas_call(
        functools.partial(_mix_out_kernel, tm=tm),
        grid=(t // tm,),
        in_specs=[row(d), row(A_WIDTH), row(B_WIDTH), *grp, *grp,
                  _layer_block(w_out, layer), _layer_block(b_out, layer), _layer_block(xg, layer),
                  pl.BlockSpec((None, None, d, hm), lambda i: (layer, i // tiles, 0, 0)),
                  pl.BlockSpec((None, None, hm, d), lambda i: (layer, i // tiles, 0, 0)),
                  _layer_block(xb_o, layer)],
        out_specs=row(d),
        out_shape=jax.ShapeDtypeStruct((t, d), F32),
        scratch_shapes=[pltpu.VMEM((2 * C_GROUPS, tm, C_GROUP_WIDTH), F32),
                        pltpu.VMEM((tm, hm), BF16),
                        pltpu.VMEM(w_out.shape[1:], BF16)],
        compiler_params=_cparams(1, "arbitrary"),
        name="mix_out_xattn",
    )(x, ya, yb, o[0], o[1], o[2], lse[0], lse[1], lse[2], w_out, b_out, xg, qk, vo, xb_o)


def _mem_fold_kernel(mem_ref, g_ref, wkv_ref, wq_ref, wo_ref, qk_ref, vo_ref):
    d = mem_ref.shape[1]
    m = mem_ref.shape[0]
    hd = d // X_HEADS
    mem_n = _rms(mem_ref[...], g_ref[...]).astype(BF16)
    kv = _dot(mem_n, wkv_ref[...].astype(BF16)).astype(BF16)
    for h in range(X_HEADS):
        cols = slice(h * hd, (h + 1) * hd)
        wq_h = wq_ref[:, cols].astype(BF16)
        qk_ref[:, h * m:(h + 1) * m] = (_dot_nt(wq_h, kv[:, cols]) * (hd ** -0.5)).astype(BF16)
        wo_h = wo_ref[cols, :].astype(BF16)
        vo_ref[h * m:(h + 1) * m, :] = _dot(kv[:, d + h * hd:d + (h + 1) * hd], wo_h).astype(BF16)


def _mem_fold(mem, g, w_kv, w_q, w_o):
    batch, m, d = mem.shape
    depth = w_kv.shape[0]

    def per_layer(stack):
        return pl.BlockSpec((None,) + stack.shape[1:], lambda l, b: (l, 0, 0))

    return pl.pallas_call(
        _mem_fold_kernel,
        grid=(depth, batch),
        in_specs=[pl.BlockSpec((None, m, d), lambda l, b: (b, 0, 0)), pl.BlockSpec((1, d), lambda l, b: (0, 0)),
                  per_layer(w_kv), per_layer(w_q), per_layer(w_o)],
        out_specs=[pl.BlockSpec((None, None, d, X_HEADS * m), lambda l, b: (l, b, 0, 0)),
                   pl.BlockSpec((None, None, X_HEADS * m, d), lambda l, b: (l, b, 0, 0))],
        out_shape=[jax.ShapeDtypeStruct((depth, batch, d, X_HEADS * m), BF16),
                   jax.ShapeDtypeStruct((depth, batch, X_HEADS * m, d), BF16)],
        compiler_params=_cparams(2),
        name="mem_fold",
    )(mem, g, w_kv, w_q, w_o)


def _cross_attend(x, g_ref, qk_ref, vo_ref, bo_ref, p_ref):
    hb = _rms(x, g_ref[...]).astype(BF16)
    s_all = _dot(hb, qk_ref[...])
    m_len = qk_ref.shape[1] // X_HEADS
    for h in range(X_HEADS):
        cols = slice(h * m_len, (h + 1) * m_len)
        s = s_all[:, cols]
        m = jnp.max(s, axis=-1, keepdims=True)
        p = jnp.exp(s - m)
        l = jnp.sum(p, axis=-1, keepdims=True)
        p_ref[:, cols] = (p / l).astype(BF16)
    return x + _dot(p_ref[...], vo_ref[...]) + bo_ref[...]


def _ffn_kernel(xp_ref, x_ref, xn_ref, g_ref, wu_ref, bu_ref, cw_ref, cb_ref, wd_ref, bd_ref, fg_ref,
                out_ref, hb_ref, u0_ref, u1_ref, u2_ref, u3_ref, acc_ref, wdb_ref, *, tm, sub, seq, d_ff, cn,
                final_norm):
    _cast_once(wd_ref, wdb_ref)
    u_bufs = ((u0_ref, u1_ref), (u2_ref, u3_ref))
    tiles = seq // tm
    t = pl.program_id(0) % tiles
    hl = FFN_HALO
    g = g_ref[...]
    hb_ref[...] = jnp.concatenate([_rms(xp_ref[...], g), _rms(x_ref[...], g), _rms(xn_ref[...], g)],
                                  axis=0).astype(BF16)
    n_sub = tm // sub
    rows = sub + 2 * hl
    n_chunks = d_ff // cn

    def conv(u_ref, ue, cols, a):
        bias = bu_ref[:, cols]
        c0, c1, c2 = cw_ref[0:1, cols], cw_ref[1:2, cols], cw_ref[2:3, cols]
        u_ref[0, 1:1 + rows, :] = ue
        u_ref[1, hl - 1:hl - 1 + rows, :] = ue
        if a == 0:
            u_ref[0, hl:hl + 1, :] = jnp.where(t > 0, ue[hl - 1:hl, :], -bias)
        if a == n_sub - 1:
            u_ref[1, 2 * hl + sub - 1:2 * hl + sub, :] = jnp.where(t < tiles - 1, ue[hl + sub:hl + sub + 1, :], -bias)
        return (c0 * u_ref[0, hl:hl + sub, :] + c1 * ue[hl:hl + sub, :] + c2 * u_ref[1, 2 * hl:2 * hl + sub, :]
                + (bias * (c0 + c1 + c2) + cb_ref[:, cols]))

    def up(a, j):
        hb = hb_ref[a * sub:a * sub + rows, :]
        return (_dot(hb, wu_ref[:, j * cn:(j + 1) * cn]), _dot(hb, wu_ref[:, d_ff + j * cn:d_ff + (j + 1) * cn]))

    def down(a, j, act):
        part = _dot(act, wdb_ref[j * cn:(j + 1) * cn, :])
        if j == 0:
            acc_ref[...] = part
        else:
            acc_ref[...] += part
        if j == n_chunks - 1:
            out_rows = slice(a * sub, (a + 1) * sub)
            y = x_ref[out_rows, :] + acc_ref[...] + bd_ref[...]
            if final_norm:
                y = _rms(y, fg_ref[...])
            out_ref[out_rows, :] = y

    items = [(a, j) for a in range(n_sub) for j in range(n_chunks)]
    ue = up(*items[0])
    act_prev = None
    for idx, (a, j) in enumerate(items):
        ue_next = up(*items[idx + 1]) if idx + 1 < len(items) else None
        if act_prev is not None:
            down(*items[idx - 1], act_prev)
        gate = conv(u_bufs[idx % 2][0], ue[0], slice(j * cn, (j + 1) * cn), a)
        val = conv(u_bufs[idx % 2][1], ue[1], slice(d_ff + j * cn, d_ff + (j + 1) * cn), a)
        act_prev = (gate * jax.nn.sigmoid(gate) * val).astype(BF16)
        ue = ue_next
    down(*items[-1], act_prev)


def _ffn(x, g, w_up, layer, b_up, conv_w, conv_b, w_down, b_down, final_g, *, tm, sub, seq, cn, final_norm):
    t, d = x.shape
    d_ff = w_down.shape[1]
    hl = FFN_HALO
    blk_per_tile = tm // hl
    n_hl = t // hl
    row = pl.BlockSpec((tm, d), lambda i: (i, 0))
    return pl.pallas_call(
        functools.partial(_ffn_kernel, tm=tm, sub=sub, seq=seq, d_ff=d_ff, cn=cn, final_norm=final_norm),
        grid=(t // tm,),
        in_specs=[pl.BlockSpec((hl, d), lambda i: (jnp.maximum(i * blk_per_tile - 1, 0), 0)),
                  row,
                  pl.BlockSpec((hl, d), lambda i: (jnp.minimum((i + 1) * blk_per_tile, n_hl - 1), 0)),
                  _layer_block(g, layer), _layer_block(w_up, layer), _layer_block(b_up, layer),
                  _layer_block(conv_w, layer), _layer_block(conv_b, layer), _layer_block(w_down, layer),
                  _layer_block(b_down, layer), _full((1, d))],
        out_specs=row,
        out_shape=jax.ShapeDtypeStruct((t, d), F32),
        scratch_shapes=([pltpu.VMEM((tm + 2 * hl, d), BF16)] + [pltpu.VMEM((2, sub + 3 * hl, cn), F32)] * 4
                        + [pltpu.VMEM((sub, d), F32), pltpu.VMEM((d_ff, d), BF16)]),
        compiler_params=_cparams(1, "arbitrary"),
        name="conv_ffn",
    )(x, x, x, g, w_up, b_up, conv_w, conv_b, w_down, b_down, final_g)


def _block_diag(blocks):
    *lead, n, a, b = blocks.shape
    eye = jnp.eye(n, dtype=blocks.dtype)
    return (eye[:, None, :, None] * blocks[..., :, :, None, :]).reshape(*lead, n * a, n * b)


def kernel(x, mem, rel_table, mem_norm_g, norm_mix_g, w_in, b_in, gmlp_v_g, gmlp_w_s, gmlp_b_s, pool_w, pool_b, pool_scale, w_out, b_out, norm_mem_g, xattn_w_q, xattn_w_kv, xattn_w_o, xattn_b_o, norm_ffn_g, ffn_w_up, ffn_b_up, ffn_conv_w, ffn_conv_b, ffn_w_down, ffn_b_down, final_norm_g):
    batch, seq, d = x.shape
    mem_len = mem.shape[1]
    depth = w_in.shape[0]
    t = batch * seq
    assert all(w // (2 * dil) == RADIUS for w, dil in DILATED_CONFIGS)
    assert x.dtype == F32 and all(seq % rows == 0 for rows in (MIX_IN_ROWS, MIX_OUT_ROWS, FFN_ROWS))
    assert FFN_ROWS % FFN_PASS_ROWS == 0 and ffn_w_down.shape[1] % FFN_CHUNK == 0

    rows = lambda a: a.reshape(depth, 1, -1)
    ones_bd = jnp.asarray(np.kron(np.eye(A_HEADS), np.ones((HEAD_DIM, HEAD_DIM))), BF16)
    biases = _attn_biases(rel_table)
    qk, vo = _mem_fold(mem, mem_norm_g.reshape(1, d), xattn_w_kv, xattn_w_q, xattn_w_o)
    w_up_bf16 = ffn_w_up.astype(BF16)
    ws_cat = (gmlp_w_s.reshape(depth, A_HEADS // 2, 2, GMLP_CHUNK, GMLP_CHUNK).transpose(0, 1, 3, 2, 4)
              .reshape(depth, A_HEADS // 2, GMLP_CHUNK, 2 * GMLP_CHUNK).astype(BF16))
    bs_full = jnp.repeat(jnp.swapaxes(gmlp_b_s, 1, 2), HEAD_DIM, axis=2)
    wp_bd = _block_diag(pool_w).astype(BF16)

    xf = x.reshape(t, d)
    for l in range(depth):
        ya, yb, *qkv = _mix_in(xf, rows(norm_mix_g), w_in, l, rows(b_in), rows(gmlp_v_g), ws_cat, bs_full, ones_bd,
                               wp_bd, rows(pool_b), rows(pool_scale), tm=MIX_IN_ROWS, seq=seq)
        o, lse = [], []
        for g, (_, dil) in enumerate(DILATED_CONFIGS):
            n_sub = min(dil, ATTN_MAX_SUBSEQ)
            og, lg = _attn(qkv[g], biases, g, dilation=dil, batch=batch, seq=seq,
                           lq=ATTN_CHUNKS_PER_STEP * 2 * RADIUS // n_sub, n_sub=n_sub)
            o.append(og)
            lse.append(lg)
        xf = _mix_out(xf, ya, yb, o, lse, w_out, rows(b_out), rows(norm_mem_g), qk, vo, l, rows(xattn_b_o),
                      tm=MIX_OUT_ROWS, seq=seq)
        xf = _ffn(xf, rows(norm_ffn_g), w_up_bf16, l, rows(ffn_b_up), ffn_conv_w, rows(ffn_conv_b), ffn_w_down,
                  rows(ffn_b_down), final_norm_g.reshape(1, d),
                  tm=FFN_ROWS, sub=FFN_PASS_ROWS, seq=seq, cn=FFN_CHUNK, final_norm=(l == depth - 1))
    return xf.reshape(batch, seq, d)
```

```python
import functools
import math

import numpy as np
import jax
import jax.numpy as jnp
from jax import lax
from jax.experimental import pallas as pl
from jax.experimental.pallas import tpu as pltpu

F32 = jnp.float32
BF16 = jnp.bfloat16

HEAD_DIM = 64
A_HEADS = 6
A_WIDTH = A_HEADS * HEAD_DIM
GMLP_CHUNK = 128
POOL_WINDOWS = (2, 4, 8, 16)
B_GROUP_DIM = 64
B_WIDTH = len(POOL_WINDOWS) * B_GROUP_DIM
DILATED_CONFIGS = ((128, 1), (512, 4), (2048, 16))
C_GROUPS = len(DILATED_CONFIGS)
C_GROUP_WIDTH = 2 * HEAD_DIM
C_WIDTH = C_GROUPS * C_GROUP_WIDTH
RADIUS = 64
N_BUCKETS = 32
MAX_DISTANCE = 1024
X_HEADS = 4
EPS = 1e-6
NEG_INF = -1e30

LANES = 128
POOL_HALO = 8
FFN_HALO = 8
VMEM_LIMIT = 56 * 1024 * 1024

MIX_IN_ROWS = 1024
MIX_OUT_ROWS = 1024
MIX_OUT_PASS_ROWS = 256
FFN_ROWS = 512
FFN_PASS_ROWS = 256
FFN_CHUNK = 256
ATTN_CHUNKS_PER_STEP = 32
ATTN_MAX_SUBSEQ = 16


def _cparams(n_axes, semantics="parallel"):
    return pltpu.CompilerParams(dimension_semantics=(semantics,) * n_axes, vmem_limit_bytes=VMEM_LIMIT)


def _layer_block(stacked, layer):
    return pl.BlockSpec((None,) + stacked.shape[1:], lambda *_: (layer,) + (0,) * (stacked.ndim - 1))


def _rms(x, g):
    ms = jnp.mean(x * x, axis=-1, keepdims=True)
    return x * lax.rsqrt(ms + EPS) * g


def _gelu_tanh(x):
    return x * (0.5 * (1.0 + jnp.tanh(math.sqrt(2.0 / math.pi) * (x + 0.044715 * (x * x * x)))))


def _dot(a, b):
    return jnp.dot(a, b, preferred_element_type=F32)


def _cast_once(src_ref, dst_ref):
    @pl.when(pl.program_id(0) == 0)
    def _():
        dst_ref[...] = src_ref[...].astype(BF16)


def _dot_nt(a, b):
    return lax.dot_general(a, b, (((1,), (1,)), ((), ())), preferred_element_type=F32)


def _full(shape):
    return pl.BlockSpec(shape, lambda *_: (0,) * len(shape))


def _pool_minus_identity(z_prev, z_main, z_next, zx_ref, ts_ref, t, tiles, tm, seq):
    hl = POOL_HALO
    pad = hl
    ext = tm + 2 * hl
    assert POOL_WINDOWS == tuple(2 ** (k + 1) for k in range(len(POOL_WINDOWS))) and POOL_WINDOWS[-1] // 2 <= pad
    zx_ref[0:pad] = jnp.zeros((pad, B_WIDTH), F32)
    zx_ref[pad:pad + hl] = jnp.where(t > 0, z_prev, 0.0)
    zx_ref[pad + hl:pad + hl + tm] = z_main
    zx_ref[pad + hl + tm:] = jnp.where(t < tiles - 1, z_next, 0.0)
    ts_ref[:, 0:pad] = jnp.zeros((len(POOL_WINDOWS), pad, B_WIDTH), F32)

    pos = t * tm + lax.broadcasted_iota(jnp.int32, (tm, LANES), 0)
    first_group = lax.broadcasted_iota(jnp.int32, (tm, LANES), 1) < B_GROUP_DIM
    pooled = []
    for cb in range(B_WIDTH // LANES):
        cols = slice(cb * LANES, (cb + 1) * LANES)
        w_a, w_b = POOL_WINDOWS[2 * cb], POOL_WINDOWS[2 * cb + 1]
        centred = {}
        for k, w in enumerate(POOL_WINDOWS):
            if w > w_b:
                break
            src = zx_ref if k == 0 else ts_ref.at[k - 1]
            shift = w // 2
            ts_ref[k, pad:pad + ext, cols] = src[pad:pad + ext, cols] + src[pad - shift:pad - shift + ext, cols]
            if w in (w_a, w_b):
                start = pad + hl + w // 2 - 1
                centred[w] = ts_ref[k, start:start + tm, cols]
        half = jnp.where(first_group, w_a // 2, w_b // 2)
        cnt = (jnp.minimum(pos + half, seq) - jnp.maximum(pos - half, 0)).astype(F32)
        wsum = jnp.where(first_group, centred[w_a], centred[w_b])
        pooled.append(wsum / cnt - zx_ref[pad + hl:pad + hl + tm, cols])
    return jnp.concatenate(pooled, axis=-1)


def _mix_in_kernel(xp_ref, x_ref, xn_ref, g_ref, w_ref, b_ref, vg_ref, ws_ref, bs_ref, ones_ref,
                   wp_ref, bp_ref, ps_ref,
                   ya_ref, yb_ref, g0_ref, g1_ref, g2_ref, qkv_ref, wb_ref, zx_ref, ts_ref, *, tm, seq):
    _cast_once(w_ref, wb_ref)
    tiles = seq // tm
    t = pl.program_id(0) % tiles
    g = g_ref[...]
    hb = _rms(x_ref[...], g).astype(BF16)

    def seg(a, b):
        return _dot(hb, wb_ref[:, a:b]) + b_ref[:, a:b]

    off_b = 2 * A_WIDTH
    off_c = off_b + B_WIDTH
    uv = seg(0, off_b)
    hl = POOL_HALO
    h_halo = jnp.concatenate([_rms(xp_ref[...], g), _rms(xn_ref[...], g)], axis=0).astype(BF16)
    z_halo = _dot(h_halo, wb_ref[:, off_b:off_c]) + b_ref[:, off_b:off_c]
    z_main = seg(off_b, off_c)
    qk = seg(off_c, off_c + 2 * C_WIDTH)
    parts = (qk[:, :C_WIDTH] * (HEAD_DIM ** -0.5), qk[:, C_WIDTH:], seg(off_c + 2 * C_WIDTH, off_c + 3 * C_WIDTH))

    pooled = _pool_minus_identity(z_halo[0:hl], z_main, z_halo[hl:], zx_ref, ts_ref, t, tiles, tm, seq).astype(BF16)
    yb_ref[...] = ((_dot(pooled, wp_ref[...]) + bp_ref[...]) * ps_ref[...]).astype(BF16)

    for grp, (out_ref, (_, dil)) in enumerate(zip((g0_ref, g1_ref, g2_ref), DILATED_CONFIGS)):
        for part in range(3):
            tile = parts[part][:, grp * C_GROUP_WIDTH:(grp + 1) * C_GROUP_WIDTH]
            if dil == 1:
                out_ref[part] = tile.astype(BF16)
                continue
            qkv_ref[part, grp] = tile
            for r in range(dil):
                piece = qkv_ref[part, grp, pl.ds(r, tm // dil, stride=dil), :]
                out_ref[part, :, r * C_GROUP_WIDTH:(r + 1) * C_GROUP_WIDTH] = piece.astype(BF16)

    uv = _gelu_tanh(uv)
    u, v = uv[:, :A_WIDTH], uv[:, A_WIDTH:]
    vsq = (v * v).astype(BF16)
    cut = 2 * LANES
    ssum = jnp.concatenate([_dot(vsq[:, :cut], ones_ref[:cut, :cut]), _dot(vsq[:, cut:], ones_ref[cut:, cut:])],
                           axis=1)
    vn = v * lax.rsqrt(ssum * (1.0 / HEAD_DIM) + EPS) * vg_ref[...]
    first_head = lax.broadcasted_iota(jnp.int32, (GMLP_CHUNK, LANES), 1) < HEAD_DIM
    chunk_rows = [slice(c * GMLP_CHUNK, (c + 1) * GMLP_CHUNK) for c in range(tm // GMLP_CHUNK)]
    for j in range(A_WIDTH // LANES):
        cols = slice(j * LANES, (j + 1) * LANES)
        rhs = jnp.concatenate(
            [jnp.concatenate([jnp.where(first_head, vn[rows, cols], 0.0).astype(BF16),
                              jnp.where(first_head, 0.0, vn[rows, cols]).astype(BF16)], axis=0)
             for rows in chunk_rows], axis=1)
        s = _dot(ws_ref[j], rhs)
        for c, rows in enumerate(chunk_rows):
            s_c = s[:, c * LANES:(c + 1) * LANES] + bs_ref[:, cols]
            ya_ref[rows, cols] = (u[rows, cols] * s_c).astype(BF16)


def _mix_in(x, g, w_in, layer, b_in, v_gain, ws_cat, bs_full, ones_bd, wp_bd, bp, ps, *, tm, seq):
    t, d = x.shape
    in_width = w_in.shape[2]
    hl = POOL_HALO
    blk_per_tile = tm // hl
    n_hl = t // hl
    qkv_shapes = [jax.ShapeDtypeStruct((3, t // dil, dil * C_GROUP_WIDTH), BF16) for _, dil in DILATED_CONFIGS]
    qkv_specs = [pl.BlockSpec((3, tm // dil, dil * C_GROUP_WIDTH), lambda i: (0, i, 0))
                 for _, dil in DILATED_CONFIGS]
    return pl.pallas_call(
        functools.partial(_mix_in_kernel, tm=tm, seq=seq),
        grid=(t // tm,),
        in_specs=[pl.BlockSpec((hl, d), lambda i: (jnp.maximum(i * blk_per_tile - 1, 0), 0)),
                  pl.BlockSpec((tm, d), lambda i: (i, 0)),
                  pl.BlockSpec((hl, d), lambda i: (jnp.minimum((i + 1) * blk_per_tile, n_hl - 1), 0)),
                  _layer_block(g, layer), _layer_block(w_in, layer), _layer_block(b_in, layer),
                  _layer_block(v_gain, layer), _layer_block(ws_cat, layer), _layer_block(bs_full, layer),
                  _full(ones_bd.shape), _layer_block(wp_bd, layer), _layer_block(bp, layer),
                  _layer_block(ps, layer)],
        out_specs=[pl.BlockSpec((tm, A_WIDTH), lambda i: (i, 0)),
                   pl.BlockSpec((tm, B_WIDTH), lambda i: (i, 0))] + qkv_specs,
        out_shape=[jax.ShapeDtypeStruct((t, A_WIDTH), BF16),
                   jax.ShapeDtypeStruct((t, B_WIDTH), BF16)] + qkv_shapes,
        scratch_shapes=[pltpu.VMEM((3, C_GROUPS, tm, C_GROUP_WIDTH), F32), pltpu.VMEM((d, in_width), BF16),
                        pltpu.VMEM((tm + 3 * hl, B_WIDTH), F32),
                        pltpu.VMEM((len(POOL_WINDOWS), tm + 3 * hl, B_WIDTH), F32)],
        compiler_params=_cparams(1, "arbitrary"),
        name="mix_in",
    )(x, x, x, g, w_in, b_in, v_gain, ws_cat, bs_full, ones_bd, wp_bd, bp, ps)


def _attn_kernel(q_ref, kp_ref, k_ref, kn_ref, vp_ref, v_ref, vn_ref, bias_ref,
                 o_ref, lse_ref, kx_ref, vx_ref, *, lq, n_blocks, n_sub):
    i = pl.program_id(2)
    kx_ref[0:RADIUS] = kp_ref[...]
    kx_ref[RADIUS:RADIUS + lq] = k_ref[...]
    kx_ref[RADIUS + lq:] = kn_ref[...]
    vx_ref[0:RADIUS] = vp_ref[...]
    vx_ref[RADIUS:RADIUS + lq] = v_ref[...]
    vx_ref[RADIUS + lq:] = vn_ref[...]

    qc = 2 * RADIUS
    kc = qc + 2 * RADIUS
    n_chunks = lq // qc
    first_head = lax.broadcasted_iota(jnp.int32, (qc, LANES), 1) < HEAD_DIM
    for c in range(n_chunks):
        rows = slice(c * qc, (c + 1) * qc)
        variant = 1
        if c == 0:
            variant = jnp.where(i == 0, 0, 1)
        if c == n_chunks - 1:
            variant = jnp.where(i == n_blocks - 1, 2, variant)
        for r in range(n_sub):
            lanes = slice(r * LANES, (r + 1) * LANES)
            q = q_ref[rows, lanes]
            keys = kx_ref[c * qc:c * qc + kc, lanes]
            vals = vx_ref[c * qc:c * qc + kc, lanes]
            zero = jnp.zeros_like(q)
            q2 = jnp.concatenate([jnp.where(first_head, q, zero), jnp.where(first_head, zero, q)], axis=0)
            s = _dot_nt(q2, keys) + bias_ref[variant]
            m = jnp.max(s, axis=-1, keepdims=True)
            p = jnp.exp(s - m).astype(BF16)
            o2 = _dot(p, jnp.concatenate([vals, jnp.ones_like(vals)], axis=1))

            def per_lane_head(a):
                a = jnp.broadcast_to(a, (2 * qc, LANES))
                return jnp.where(first_head, a[:qc], a[qc:])

            l_both = per_lane_head(o2[:, LANES:])
            o_ref[rows, lanes] = per_lane_head(o2[:, :LANES]) / l_both
            lse_ref[rows, lanes] = per_lane_head(m) + jnp.log(l_both)


def _attn(qkv, biases, group, *, dilation, batch, seq, lq, n_sub):
    sub_len = seq // dilation
    lq = min(lq, sub_len)
    n_sub = min(n_sub, dilation)
    n_blocks = sub_len // lq
    assert sub_len >= 4 * RADIUS
    qkv = qkv.reshape(3, batch, sub_len, dilation * C_GROUP_WIDTH)
    halo_per_blk = lq // RADIUS
    n_halo = sub_len // RADIUS
    width = n_sub * C_GROUP_WIDTH

    def main(part):
        return pl.BlockSpec((None, None, lq, width), lambda b, r, i: (part, b, i, r))

    def prev(part):
        return pl.BlockSpec((None, None, RADIUS, width),
                            lambda b, r, i: (part, b, jnp.maximum(i * halo_per_blk - 1, 0), r))

    def nxt(part):
        return pl.BlockSpec((None, None, RADIUS, width),
                            lambda b, r, i: (part, b, jnp.minimum((i + 1) * halo_per_blk, n_halo - 1), r))

    out_spec = pl.BlockSpec((None, lq, width), lambda b, r, i: (b, i, r))
    out_shape = jax.ShapeDtypeStruct((batch, sub_len, dilation * C_GROUP_WIDTH), F32)
    o, lse = pl.pallas_call(
        functools.partial(_attn_kernel, lq=lq, n_blocks=n_blocks, n_sub=n_sub),
        grid=(batch, dilation // n_sub, n_blocks),
        in_specs=[main(0), prev(1), main(1), nxt(1), prev(2), main(2), nxt(2),
                  pl.BlockSpec((None,) + biases.shape[1:], lambda b, r, i: (group, 0, 0, 0))],
        out_specs=[out_spec, out_spec],
        out_shape=[out_shape, out_shape],
        scratch_shapes=[pltpu.VMEM((lq + 2 * RADIUS, width), BF16),
                        pltpu.VMEM((lq + 2 * RADIUS, width), BF16)],
        compiler_params=_cparams(3),
        name=f"dilated_attn_d{dilation}",
    )(qkv, qkv, qkv, qkv, qkv, qkv, qkv, biases)
    rows = batch * sub_len
    return o.reshape(rows, dilation * C_GROUP_WIDTH), lse.reshape(rows, dilation * C_GROUP_WIDTH)


def _t5_bucket(rel):
    nb = N_BUCKETS // 2
    ret = (rel > 0).astype(np.int32) * nb
    n = np.abs(rel)
    max_exact = nb // 2
    large = max_exact + (np.log(np.maximum(n, 1) / max_exact)
                         / math.log(MAX_DISTANCE / max_exact) * (nb - max_exact)).astype(np.int32)
    large = np.minimum(large, nb - 1)
    return ret + np.where(n < max_exact, n, large)


def _attn_biases(rel_table):
    qc, kc = 2 * RADIUS, 4 * RADIUS
    n = qc + kc - 1
    delta = (np.arange(n) + qc - 1) % n - (qc + RADIUS - 1)
    dils = np.array([dil for _, dil in DILATED_CONFIGS])
    buckets = _t5_bucket(delta[None, :] * dils[:, None])
    onehot = (buckets[..., None] == np.arange(N_BUCKETS)).astype(np.float32)
    table = rel_table.reshape(N_BUCKETS, C_GROUPS, 2)
    per_delta = jnp.einsum('gnb,bgh->ghn', onehot, table, precision=lax.Precision.HIGHEST)
    per_delta = jnp.where(np.abs(delta) <= RADIUS, per_delta, NEG_INF)
    reps = -(-(qc * (n - 1)) // n)
    band = jnp.tile(per_delta, (1, 1, reps))[..., :qc * (n - 1)].reshape(C_GROUPS, 2, qc, n - 1)[..., :kc]
    key = np.arange(kc)
    variants = jnp.stack([jnp.where(key < RADIUS, NEG_INF, band), band,
                          jnp.where(key >= kc - RADIUS, NEG_INF, band)], axis=1)
    return variants.reshape(C_GROUPS, 3, 2 * qc, kc)


def _mix_out_kernel(x_ref, ya_ref, yb_ref, o0_ref, o1_ref, o2_ref, l0_ref, l1_ref, l2_ref,
                    wo_ref, bo_ref, xg_ref, qk_ref, vo_ref, xbo_ref,
                    out_ref, nat_ref, xp_ref, wob_ref, *, tm, sub):
    _cast_once(wo_ref, wob_ref)

    dils = [dil for _, dil in DILATED_CONFIGS] * 2
    grp_refs = (o0_ref, o1_ref, o2_ref, l0_ref, l1_ref, l2_ref)

    for s in range(tm // sub):
        rows = slice(s * sub, (s + 1) * sub)

        def natural(slot):
            src_ref, dil = grp_refs[slot], dils[slot]
            if dil == 1:
                return src_ref[rows, :]
            for r in range(dil):
                nat_ref[slot, pl.ds(s * sub + r, sub // dil, stride=dil), :] = (
                    src_ref[s * sub // dil:(s + 1) * sub // dil, r * C_GROUP_WIDTH:(r + 1) * C_GROUP_WIDTH])
            return nat_ref[slot, rows, :]

        l0, l1, l2 = natural(3), natural(4), natural(5)
        m = jnp.maximum(l0, jnp.maximum(l1, l2))
        e0, e1, e2 = jnp.exp(l0 - m), jnp.exp(l1 - m), jnp.exp(l2 - m)
        yc = (e0 * natural(0) + e1 * natural(1) + e2 * natural(2)) / (e0 + e1 + e2)

        y = (_dot(ya_ref[rows, :], wob_ref[0:A_WIDTH, :])
             + _dot(yb_ref[rows, :], wob_ref[A_WIDTH:A_WIDTH + B_WIDTH, :])
             + _dot(yc.astype(BF16), wob_ref[A_WIDTH + B_WIDTH:, :]))
        out_ref[rows, :] = x_ref[rows, :] + y + bo_ref[...]
    out_ref[...] = _cross_attend(out_ref[...], xg_ref, qk_ref, vo_ref, xbo_ref, xp_ref)


def _mix_out(x, ya, yb, o, lse, w_out, b_out, xg, qk, vo, layer, xb_o, *, tm, sub, seq):
    t, d = x.shape
    tiles = seq // tm
    hm = qk.shape[-1]
    row = lambda w: pl.BlockSpec((tm, w), lambda i: (i, 0))
    grp = [pl.BlockSpec((tm // dil, dil * C_GROUP_WIDTH), lambda i: (i, 0)) for _, dil in DILATED_CONFIGS]
    return pl.pallas_call(
        functools.partial(_mix_out_kernel, tm=tm, sub=sub),
        grid=(t // tm,),
        in_specs=[row(d), row(A_WIDTH), row(B_WIDTH), *grp, *grp,
                  _layer_block(w_out, layer), _layer_block(b_out, layer), _layer_block(xg, layer),
                  pl.BlockSpec((None, None, d, hm), lambda i: (layer, i // tiles, 0, 0)),
                  pl.BlockSpec((None, None, hm, d), lambda i: (layer, i // tiles, 0, 0)),
                  _layer_block(xb_o, layer)],
        out_specs=row(d),
        out_shape=jax.ShapeDtypeStruct((t, d), F32),
        scratch_shapes=[pltpu.VMEM((2 * C_GROUPS, tm, C_GROUP_WIDTH), F32),
                        pltpu.VMEM((tm, hm), BF16),
                        pltpu.VMEM(w_out.shape[1:], BF16)],
        compiler_params=_cparams(1, "arbitrary"),
        name="mix_out_xattn",
    )(x, ya, yb, o[0], o[1], o[2], lse[0], lse[1], lse[2], w_out, b_out, xg, qk, vo, xb_o)


def _mem_fold_kernel(mem_ref, g_ref, wkv_ref, wq_ref, wo_ref, qk_ref, vo_ref):
    d = mem_ref.shape[1]
    m = mem_ref.shape[0]
    hd = d // X_HEADS
    mem_n = _rms(mem_ref[...], g_ref[...]).astype(BF16)
    kv = _dot(mem_n, wkv_ref[...].astype(BF16)).astype(BF16)
    for h in range(X_HEADS):
        cols = slice(h * hd, (h + 1) * hd)
        wq_h = wq_ref[:, cols].astype(BF16)
        qk_ref[:, h * m:(h + 1) * m] = (_dot_nt(wq_h, kv[:, cols]) * (hd ** -0.5)).astype(BF16)
        wo_h = wo_ref[cols, :].astype(BF16)
        vo_ref[h * m:(h + 1) * m, :] = _dot(kv[:, d + h * hd:d + (h + 1) * hd], wo_h).astype(BF16)


def _mem_fold(mem, g, w_kv, w_q, w_o):
    batch, m, d = mem.shape
    depth = w_kv.shape[0]

    def per_layer(stack):
        return pl.BlockSpec((None,) + stack.shape[1:], lambda l, b: (l, 0, 0))

    return pl.pallas_call(
        _mem_fold_kernel,
        grid=(depth, batch),
        in_specs=[pl.BlockSpec((None, m, d), lambda l, b: (b, 0, 0)), pl.BlockSpec((1, d), lambda l, b: (0, 0)),
                  per_layer(w_kv), per_layer(w_q), per_layer(w_o)],
        out_specs=[pl.BlockSpec((None, None, d, X_HEADS * m), lambda l, b: (l, b, 0, 0)),
                   pl.BlockSpec((None, None, X_HEADS * m, d), lambda l, b: (l, b, 0, 0))],
        out_shape=[jax.ShapeDtypeStruct((depth, batch, d, X_HEADS * m), BF16),
                   jax.ShapeDtypeStruct((depth, batch, X_HEADS * m, d), BF16)],
        compiler_params=_cparams(2),
        name="mem_fold",
    )(mem, g, w_kv, w_q, w_o)


def _cross_attend(x, g_ref, qk_ref, vo_ref, bo_ref, p_ref):
    hb = _rms(x, g_ref[...]).astype(BF16)
    s_all = _dot(hb, qk_ref[...])
    m_len = qk_ref.shape[1] // X_HEADS
    for h in range(X_HEADS):
        cols = slice(h * m_len, (h + 1) * m_len)
        s = s_all[:, cols]
        m = jnp.max(s, axis=-1, keepdims=True)
        p = jnp.exp(s - m)
        l = jnp.sum(p, axis=-1, keepdims=True)
        p_ref[:, cols] = (p / l).astype(BF16)
    return x + _dot(p_ref[...], vo_ref[...]) + bo_ref[...]


def _ffn_kernel(xp_ref, x_ref, xn_ref, g_ref, wu_ref, bu_ref, cw_ref, cb_ref, wd_ref, bd_ref, fg_ref,
                out_ref, hb_ref, u0_ref, u1_ref, u2_ref, u3_ref, acc_ref, wdb_ref, *, tm, sub, seq, d_ff, cn,
                final_norm):
    _cast_once(wd_ref, wdb_ref)
    u_bufs = ((u0_ref, u1_ref), (u2_ref, u3_ref))
    tiles = seq // tm
    t = pl.program_id(0) % tiles
    hl = FFN_HALO
    g = g_ref[...]
    hb_ref[...] = jnp.concatenate([_rms(xp_ref[...], g), _rms(x_ref[...], g), _rms(xn_ref[...], g)],
                                  axis=0).astype(BF16)
    n_sub = tm // sub
    rows = sub + 2 * hl
    n_chunks = d_ff // cn

    def conv(u_ref, ue, cols, a):
        bias = bu_ref[:, cols]
        c0, c1, c2 = cw_ref[0:1, cols], cw_ref[1:2, cols], cw_ref[2:3, cols]
        u_ref[0, 1:1 + rows, :] = ue
        u_ref[1, hl - 1:hl - 1 + rows, :] = ue
        if a == 0:
            u_ref[0, hl:hl + 1, :] = jnp.where(t > 0, ue[hl - 1:hl, :], -bias)
        if a == n_sub - 1:
            u_ref[1, 2 * hl + sub - 1:2 * hl + sub, :] = jnp.where(t < tiles - 1, ue[hl + sub:hl + sub + 1, :], -bias)
        return (c0 * u_ref[0, hl:hl + sub, :] + c1 * ue[hl:hl + sub, :] + c2 * u_ref[1, 2 * hl:2 * hl + sub, :]
                + (bias * (c0 + c1 + c2) + cb_ref[:, cols]))

    def up(a, j):
        hb = hb_ref[a * sub:a * sub + rows, :]
        return (_dot(hb, wu_ref[:, j * cn:(j + 1) * cn]), _dot(hb, wu_ref[:, d_ff + j * cn:d_ff + (j + 1) * cn]))

    def down(a, j, act):
        part = _dot(act, wdb_ref[j * cn:(j + 1) * cn, :])
        if j == 0:
            acc_ref[...] = part
        else:
            acc_ref[...] += part
        if j == n_chunks - 1:
            out_rows = slice(a * sub, (a + 1) * sub)
            y = x_ref[out_rows, :] + acc_ref[...] + bd_ref[...]
            if final_norm:
                y = _rms(y, fg_ref[...])
            out_ref[out_rows, :] = y

    items = [(a, j) for a in range(n_sub) for j in range(n_chunks)]
    ue = up(*items[0])
    act_prev = None
    for idx, (a, j) in enumerate(items):
        ue_next = up(*items[idx + 1]) if idx + 1 < len(items) else None
        if act_prev is not None:
            down(*items[idx - 1], act_prev)
        gate = conv(u_bufs[idx % 2][0], ue[0], slice(j * cn, (j + 1) * cn), a)
        val = conv(u_bufs[idx % 2][1], ue[1], slice(d_ff + j * cn, d_ff + (j + 1) * cn), a)
        act_prev = (gate * jax.nn.sigmoid(gate) * val).astype(BF16)
        ue = ue_next
    down(*items[-1], act_prev)


def _ffn(x, g, w_up, layer, b_up, conv_w, conv_b, w_down, b_down, final_g, *, tm, sub, seq, cn, final_norm):
    t, d = x.shape
    d_ff = w_down.shape[1]
    hl = FFN_HALO
    blk_per_tile = tm // hl
    n_hl = t // hl
    row = pl.BlockSpec((tm, d), lambda i: (i, 0))
    return pl.pallas_call(
        functools.partial(_ffn_kernel, tm=tm, sub=sub, seq=seq, d_ff=d_ff, cn=cn, final_norm=final_norm),
        grid=(t // tm,),
        in_specs=[pl.BlockSpec((hl, d), lambda i: (jnp.maximum(i * blk_per_tile - 1, 0), 0)),
                  row,
                  pl.BlockSpec((hl, d), lambda i: (jnp.minimum((i + 1) * blk_per_tile, n_hl - 1), 0)),
                  _layer_block(g, layer), _layer_block(w_up, layer), _layer_block(b_up, layer),
                  _layer_block(conv_w, layer), _layer_block(conv_b, layer), _layer_block(w_down, layer),
                  _layer_block(b_down, layer), _full((1, d))],
        out_specs=row,
        out_shape=jax.ShapeDtypeStruct((t, d), F32),
        scratch_shapes=([pltpu.VMEM((tm + 2 * hl, d), BF16)] + [pltpu.VMEM((2, sub + 3 * hl, cn), F32)] * 4
                        + [pltpu.VMEM((sub, d), F32), pltpu.VMEM((d_ff, d), BF16)]),
        compiler_params=_cparams(1, "arbitrary"),
        name="conv_ffn",
    )(x, x, x, g, w_up, b_up, conv_w, conv_b, w_down, b_down, final_g)


def _block_diag(blocks):
    *lead, n, a, b = blocks.shape
    eye = jnp.eye(n, dtype=blocks.dtype)
    return (eye[:, None, :, None] * blocks[..., :, :, None, :]).reshape(*lead, n * a, n * b)


def kernel(x, mem, rel_table, mem_norm_g, norm_mix_g, w_in, b_in, gmlp_v_g, gmlp_w_s, gmlp_b_s, pool_w, pool_b, pool_scale, w_out, b_out, norm_mem_g, xattn_w_q, xattn_w_kv, xattn_w_o, xattn_b_o, norm_ffn_g, ffn_w_up, ffn_b_up, ffn_conv_w, ffn_conv_b, ffn_w_down, ffn_b_down, final_norm_g):
    batch, seq, d = x.shape
    mem_len = mem.shape[1]
    depth = w_in.shape[0]
    t = batch * seq
    assert all(w // (2 * dil) == RADIUS for w, dil in DILATED_CONFIGS)
    assert x.dtype == F32 and all(seq % rows == 0 for rows in (MIX_IN_ROWS, MIX_OUT_ROWS, FFN_ROWS))
    assert FFN_ROWS % FFN_PASS_ROWS == 0 and ffn_w_down.shape[1] % FFN_CHUNK == 0

    rows = lambda a: a.reshape(depth, 1, -1)
    ones_bd = jnp.asarray(np.kron(np.eye(A_HEADS), np.ones((HEAD_DIM, HEAD_DIM))), BF16)
    biases = _attn_biases(rel_table)
    qk, vo = _mem_fold(mem, mem_norm_g.reshape(1, d), xattn_w_kv, xattn_w_q, xattn_w_o)
    w_up_bf16 = ffn_w_up.astype(BF16)
    ws_cat = (gmlp_w_s.reshape(depth, A_HEADS // 2, 2, GMLP_CHUNK, GMLP_CHUNK).transpose(0, 1, 3, 2, 4)
              .reshape(depth, A_HEADS // 2, GMLP_CHUNK, 2 * GMLP_CHUNK).astype(BF16))
    bs_full = jnp.repeat(jnp.swapaxes(gmlp_b_s, 1, 2), HEAD_DIM, axis=2)
    wp_bd = _block_diag(pool_w).astype(BF16)

    xf = x.reshape(t, d)
    for l in range(depth):
        ya, yb, *qkv = _mix_in(xf, rows(norm_mix_g), w_in, l, rows(b_in), rows(gmlp_v_g), ws_cat, bs_full, ones_bd,
                               wp_bd, rows(pool_b), rows(pool_scale), tm=MIX_IN_ROWS, seq=seq)
        o, lse = [], []
        for g, (_, dil) in enumerate(DILATED_CONFIGS):
            n_sub = min(dil, ATTN_MAX_SUBSEQ)
            og, lg = _attn(qkv[g], biases, g, dilation=dil, batch=batch, seq=seq,
                           lq=ATTN_CHUNKS_PER_STEP * 2 * RADIUS // n_sub, n_sub=n_sub)
            o.append(og)
            lse.append(lg)
        xf = _mix_out(xf, ya, yb, o, lse, w_out, rows(b_out), rows(norm_mem_g), qk, vo, l, rows(xattn_b_o),
                      tm=MIX_OUT_ROWS, sub=MIX_OUT_PASS_ROWS, seq=seq)
        xf = _ffn(xf, rows(norm_ffn_g), w_up_bf16, l, rows(ffn_b_up), ffn_conv_w, rows(ffn_conv_b), ffn_w_down,
                  rows(ffn_b_down), final_norm_g.reshape(1, d),
                  tm=FFN_ROWS, sub=FFN_PASS_ROWS, seq=seq, cn=FFN_CHUNK, final_norm=(l == depth - 1))
    return xf.reshape(batch, seq, d)
```

```python
import functools
import math

import numpy as np
import jax
import jax.numpy as jnp
from jax import lax
from jax.experimental import pallas as pl
from jax.experimental.pallas import tpu as pltpu

F32 = jnp.float32
BF16 = jnp.bfloat16

HEAD_DIM = 64
A_HEADS = 6
A_WIDTH = A_HEADS * HEAD_DIM
GMLP_CHUNK = 128
POOL_WINDOWS = (2, 4, 8, 16)
B_GROUP_DIM = 64
B_WIDTH = len(POOL_WINDOWS) * B_GROUP_DIM
DILATED_CONFIGS = ((128, 1), (512, 4), (2048, 16))
C_GROUPS = len(DILATED_CONFIGS)
C_GROUP_WIDTH = 2 * HEAD_DIM
C_WIDTH = C_GROUPS * C_GROUP_WIDTH
RADIUS = 64
N_BUCKETS = 32
MAX_DISTANCE = 1024
X_HEADS = 4
EPS = 1e-6
NEG_INF = -1e30

LANES = 128
POOL_HALO = 8
FFN_HALO = 8
VMEM_LIMIT = 56 * 1024 * 1024

MIX_IN_ROWS = 1024
MIX_OUT_ROWS = 1024
MIX_OUT_PASS_ROWS = 256
FFN_ROWS = 512
FFN_PASS_ROWS = 256
FFN_CHUNK = 256
FFN_STREAM_PANEL = 512
ATTN_CHUNKS_PER_STEP = 32
ATTN_MAX_SUBSEQ = 16


def _cparams(n_axes, semantics="parallel"):
    return pltpu.CompilerParams(dimension_semantics=(semantics,) * n_axes, vmem_limit_bytes=VMEM_LIMIT)


def _layer_block(stacked, layer):
    return pl.BlockSpec((None,) + stacked.shape[1:], lambda *_: (layer,) + (0,) * (stacked.ndim - 1))


def _rms(x, g):
    ms = jnp.mean(x * x, axis=-1, keepdims=True)
    return x * lax.rsqrt(ms + EPS) * g


def _gelu_tanh(x):
    return x * (0.5 * (1.0 + jnp.tanh(math.sqrt(2.0 / math.pi) * (x + 0.044715 * (x * x * x)))))


def _dot(a, b):
    return jnp.dot(a, b, preferred_element_type=F32)


def _cast_once(src_ref, dst_ref):
    @pl.when(pl.program_id(0) == 0)
    def _():
        dst_ref[...] = src_ref[...].astype(BF16)


def _stream_cast_once(src_hbm, layer, dst_ref, stage_ref, sem):
    width = stage_ref.shape[2]
    n_panels = dst_ref.shape[1] // width

    def panel_copy(c):
        return pltpu.make_async_copy(src_hbm.at[layer, :, pl.ds(c * width, width)], stage_ref.at[c % 2],
                                     sem.at[c % 2])

    @pl.when(pl.program_id(0) == 0)
    def _():
        panel_copy(0).start()
        for c in range(n_panels):
            if c + 1 < n_panels:
                panel_copy(c + 1).start()
            panel_copy(c).wait()
            dst_ref[:, c * width:(c + 1) * width] = stage_ref[c % 2].astype(BF16)


def _dot_nt(a, b):
    return lax.dot_general(a, b, (((1,), (1,)), ((), ())), preferred_element_type=F32)


def _full(shape):
    return pl.BlockSpec(shape, lambda *_: (0,) * len(shape))


def _pool_minus_identity(z_prev, z_main, z_next, zx_ref, ts_ref, t, tiles, tm, seq):
    hl = POOL_HALO
    pad = hl
    ext = tm + 2 * hl
    assert POOL_WINDOWS == tuple(2 ** (k + 1) for k in range(len(POOL_WINDOWS))) and POOL_WINDOWS[-1] // 2 <= pad
    zx_ref[0:pad] = jnp.zeros((pad, B_WIDTH), F32)
    zx_ref[pad:pad + hl] = jnp.where(t > 0, z_prev, 0.0)
    zx_ref[pad + hl:pad + hl + tm] = z_main
    zx_ref[pad + hl + tm:] = jnp.where(t < tiles - 1, z_next, 0.0)
    ts_ref[:, 0:pad] = jnp.zeros((len(POOL_WINDOWS), pad, B_WIDTH), F32)

    pos = t * tm + lax.broadcasted_iota(jnp.int32, (tm, LANES), 0)
    first_group = lax.broadcasted_iota(jnp.int32, (tm, LANES), 1) < B_GROUP_DIM
    pooled = []
    for cb in range(B_WIDTH // LANES):
        cols = slice(cb * LANES, (cb + 1) * LANES)
        w_a, w_b = POOL_WINDOWS[2 * cb], POOL_WINDOWS[2 * cb + 1]
        centred = {}
        for k, w in enumerate(POOL_WINDOWS):
            if w > w_b:
                break
            src = zx_ref if k == 0 else ts_ref.at[k - 1]
            shift = w // 2
            ts_ref[k, pad:pad + ext, cols] = src[pad:pad + ext, cols] + src[pad - shift:pad - shift + ext, cols]
            if w in (w_a, w_b):
                start = pad + hl + w // 2 - 1
                centred[w] = ts_ref[k, start:start + tm, cols]
        half = jnp.where(first_group, w_a // 2, w_b // 2)
        cnt = (jnp.minimum(pos + half, seq) - jnp.maximum(pos - half, 0)).astype(F32)
        wsum = jnp.where(first_group, centred[w_a], centred[w_b])
        pooled.append(wsum / cnt - zx_ref[pad + hl:pad + hl + tm, cols])
    return jnp.concatenate(pooled, axis=-1)


def _mix_in_kernel(xp_ref, x_ref, xn_ref, g_ref, w_ref, b_ref, vg_ref, ws_ref, bs_ref, ones_ref,
                   wp_ref, bp_ref, ps_ref,
                   ya_ref, yb_ref, g0_ref, g1_ref, g2_ref, qkv_ref, wb_ref, zx_ref, ts_ref, *, tm, seq):
    _cast_once(w_ref, wb_ref)
    tiles = seq // tm
    t = pl.program_id(0) % tiles
    g = g_ref[...]
    hb = _rms(x_ref[...], g).astype(BF16)

    def seg(a, b):
        return _dot(hb, wb_ref[:, a:b]) + b_ref[:, a:b]

    off_b = 2 * A_WIDTH
    off_c = off_b + B_WIDTH
    uv = seg(0, off_b)
    hl = POOL_HALO
    h_halo = jnp.concatenate([_rms(xp_ref[...], g), _rms(xn_ref[...], g)], axis=0).astype(BF16)
    z_halo = _dot(h_halo, wb_ref[:, off_b:off_c]) + b_ref[:, off_b:off_c]
    z_main = seg(off_b, off_c)
    qk = seg(off_c, off_c + 2 * C_WIDTH)
    parts = (qk[:, :C_WIDTH] * (HEAD_DIM ** -0.5), qk[:, C_WIDTH:], seg(off_c + 2 * C_WIDTH, off_c + 3 * C_WIDTH))

    pooled = _pool_minus_identity(z_halo[0:hl], z_main, z_halo[hl:], zx_ref, ts_ref, t, tiles, tm, seq).astype(BF16)
    yb_ref[...] = ((_dot(pooled, wp_ref[...]) + bp_ref[...]) * ps_ref[...]).astype(BF16)

    for grp, (out_ref, (_, dil)) in enumerate(zip((g0_ref, g1_ref, g2_ref), DILATED_CONFIGS)):
        for part in range(3):
            tile = parts[part][:, grp * C_GROUP_WIDTH:(grp + 1) * C_GROUP_WIDTH]
            if dil == 1:
                out_ref[part] = tile.astype(BF16)
                continue
            qkv_ref[part, grp] = tile
            for r in range(dil):
                piece = qkv_ref[part, grp, pl.ds(r, tm // dil, stride=dil), :]
                out_ref[part, :, r * C_GROUP_WIDTH:(r + 1) * C_GROUP_WIDTH] = piece.astype(BF16)

    uv = _gelu_tanh(uv)
    u, v = uv[:, :A_WIDTH], uv[:, A_WIDTH:]
    vsq = (v * v).astype(BF16)
    cut = 2 * LANES
    ssum = jnp.concatenate([_dot(vsq[:, :cut], ones_ref[:cut, :cut]), _dot(vsq[:, cut:], ones_ref[cut:, cut:])],
                           axis=1)
    vn = v * lax.rsqrt(ssum * (1.0 / HEAD_DIM) + EPS) * vg_ref[...]
    first_head = lax.broadcasted_iota(jnp.int32, (GMLP_CHUNK, LANES), 1) < HEAD_DIM
    chunk_rows = [slice(c * GMLP_CHUNK, (c + 1) * GMLP_CHUNK) for c in range(tm // GMLP_CHUNK)]
    for j in range(A_WIDTH // LANES):
        cols = slice(j * LANES, (j + 1) * LANES)
        rhs = jnp.concatenate(
            [jnp.concatenate([jnp.where(first_head, vn[rows, cols], 0.0).astype(BF16),
                              jnp.where(first_head, 0.0, vn[rows, cols]).astype(BF16)], axis=0)
             for rows in chunk_rows], axis=1)
        s = _dot(ws_ref[j], rhs)
        for c, rows in enumerate(chunk_rows):
            s_c = s[:, c * LANES:(c + 1) * LANES] + bs_ref[:, cols]
            ya_ref[rows, cols] = (u[rows, cols] * s_c).astype(BF16)


def _mix_in(x, g, w_in, layer, b_in, v_gain, ws_cat, bs_full, ones_bd, wp_bd, bp, ps, *, tm, seq):
    t, d = x.shape
    in_width = w_in.shape[2]
    hl = POOL_HALO
    blk_per_tile = tm // hl
    n_hl = t // hl
    qkv_shapes = [jax.ShapeDtypeStruct((3, t // dil, dil * C_GROUP_WIDTH), BF16) for _, dil in DILATED_CONFIGS]
    qkv_specs = [pl.BlockSpec((3, tm // dil, dil * C_GROUP_WIDTH), lambda i: (0, i, 0))
                 for _, dil in DILATED_CONFIGS]
    return pl.pallas_call(
        functools.partial(_mix_in_kernel, tm=tm, seq=seq),
        grid=(t // tm,),
        in_specs=[pl.BlockSpec((hl, d), lambda i: (jnp.maximum(i * blk_per_tile - 1, 0), 0)),
                  pl.BlockSpec((tm, d), lambda i: (i, 0)),
                  pl.BlockSpec((hl, d), lambda i: (jnp.minimum((i + 1) * blk_per_tile, n_hl - 1), 0)),
                  _layer_block(g, layer), _layer_block(w_in, layer), _layer_block(b_in, layer),
                  _layer_block(v_gain, layer), _layer_block(ws_cat, layer), _layer_block(bs_full, layer),
                  _full(ones_bd.shape), _layer_block(wp_bd, layer), _layer_block(bp, layer),
                  _layer_block(ps, layer)],
        out_specs=[pl.BlockSpec((tm, A_WIDTH), lambda i: (i, 0)),
                   pl.BlockSpec((tm, B_WIDTH), lambda i: (i, 0))] + qkv_specs,
        out_shape=[jax.ShapeDtypeStruct((t, A_WIDTH), BF16),
                   jax.ShapeDtypeStruct((t, B_WIDTH), BF16)] + qkv_shapes,
        scratch_shapes=[pltpu.VMEM((3, C_GROUPS, tm, C_GROUP_WIDTH), F32), pltpu.VMEM((d, in_width), BF16),
                        pltpu.VMEM((tm + 3 * hl, B_WIDTH), F32),
                        pltpu.VMEM((len(POOL_WINDOWS), tm + 3 * hl, B_WIDTH), F32)],
        compiler_params=_cparams(1, "arbitrary"),
        name="mix_in",
    )(x, x, x, g, w_in, b_in, v_gain, ws_cat, bs_full, ones_bd, wp_bd, bp, ps)


def _attn_kernel(q_ref, kp_ref, k_ref, kn_ref, vp_ref, v_ref, vn_ref, bias_ref,
                 o_ref, lse_ref, kx_ref, vx_ref, *, lq, n_blocks, n_sub):
    i = pl.program_id(2)
    kx_ref[0:RADIUS] = kp_ref[...]
    kx_ref[RADIUS:RADIUS + lq] = k_ref[...]
    kx_ref[RADIUS + lq:] = kn_ref[...]
    vx_ref[0:RADIUS] = vp_ref[...]
    vx_ref[RADIUS:RADIUS + lq] = v_ref[...]
    vx_ref[RADIUS + lq:] = vn_ref[...]

    qc = 2 * RADIUS
    kc = qc + 2 * RADIUS
    n_chunks = lq // qc
    first_head = lax.broadcasted_iota(jnp.int32, (qc, LANES), 1) < HEAD_DIM
    for c in range(n_chunks):
        rows = slice(c * qc, (c + 1) * qc)
        variant = 1
        if c == 0:
            variant = jnp.where(i == 0, 0, 1)
        if c == n_chunks - 1:
            variant = jnp.where(i == n_blocks - 1, 2, variant)
        for r in range(n_sub):
            lanes = slice(r * LANES, (r + 1) * LANES)
            q = q_ref[rows, lanes]
            keys = kx_ref[c * qc:c * qc + kc, lanes]
            vals = vx_ref[c * qc:c * qc + kc, lanes]
            zero = jnp.zeros_like(q)
            q2 = jnp.concatenate([jnp.where(first_head, q, zero), jnp.where(first_head, zero, q)], axis=0)
            s = _dot_nt(q2, keys) + bias_ref[variant]
            m = jnp.max(s, axis=-1, keepdims=True)
            p = jnp.exp(s - m).astype(BF16)
            o2 = _dot(p, jnp.concatenate([vals, jnp.ones_like(vals)], axis=1))

            def per_lane_head(a):
                a = jnp.broadcast_to(a, (2 * qc, LANES))
                return jnp.where(first_head, a[:qc], a[qc:])

            l_both = per_lane_head(o2[:, LANES:])
            o_ref[rows, lanes] = per_lane_head(o2[:, :LANES]) / l_both
            lse_ref[rows, lanes] = per_lane_head(m) + jnp.log(l_both)


def _attn(qkv, biases, group, *, dilation, batch, seq, lq, n_sub):
    sub_len = seq // dilation
    lq = min(lq, sub_len)
    n_sub = min(n_sub, dilation)
    n_blocks = sub_len // lq
    assert sub_len >= 4 * RADIUS
    qkv = qkv.reshape(3, batch, sub_len, dilation * C_GROUP_WIDTH)
    halo_per_blk = lq // RADIUS
    n_halo = sub_len // RADIUS
    width = n_sub * C_GROUP_WIDTH

    def main(part):
        return pl.BlockSpec((None, None, lq, width), lambda b, r, i: (part, b, i, r))

    def prev(part):
        return pl.BlockSpec((None, None, RADIUS, width),
                            lambda b, r, i: (part, b, jnp.maximum(i * halo_per_blk - 1, 0), r))

    def nxt(part):
        return pl.BlockSpec((None, None, RADIUS, width),
                            lambda b, r, i: (part, b, jnp.minimum((i + 1) * halo_per_blk, n_halo - 1), r))

    out_spec = pl.BlockSpec((None, lq, width), lambda b, r, i: (b, i, r))
    out_shape = jax.ShapeDtypeStruct((batch, sub_len, dilation * C_GROUP_WIDTH), F32)
    o, lse = pl.pallas_call(
        functools.partial(_attn_kernel, lq=lq, n_blocks=n_blocks, n_sub=n_sub),
        grid=(batch, dilation // n_sub, n_blocks),
        in_specs=[main(0), prev(1), main(1), nxt(1), prev(2), main(2), nxt(2),
                  pl.BlockSpec((None,) + biases.shape[1:], lambda b, r, i: (group, 0, 0, 0))],
        out_specs=[out_spec, out_spec],
        out_shape=[out_shape, out_shape],
        scratch_shapes=[pltpu.VMEM((lq + 2 * RADIUS, width), BF16),
                        pltpu.VMEM((lq + 2 * RADIUS, width), BF16)],
        compiler_params=_cparams(3),
        name=f"dilated_attn_d{dilation}",
    )(qkv, qkv, qkv, qkv, qkv, qkv, qkv, biases)
    rows = batch * sub_len
    return o.reshape(rows, dilation * C_GROUP_WIDTH), lse.reshape(rows, dilation * C_GROUP_WIDTH)


def _t5_bucket(rel):
    nb = N_BUCKETS // 2
    ret = (rel > 0).astype(np.int32) * nb
    n = np.abs(rel)
    max_exact = nb // 2
    large = max_exact + (np.log(np.maximum(n, 1) / max_exact)
                         / math.log(MAX_DISTANCE / max_exact) * (nb - max_exact)).astype(np.int32)
    large = np.minimum(large, nb - 1)
    return ret + np.where(n < max_exact, n, large)


def _attn_biases(rel_table):
    qc, kc = 2 * RADIUS, 4 * RADIUS
    n = qc + kc - 1
    delta = (np.arange(n) + qc - 1) % n - (qc + RADIUS - 1)
    dils = np.array([dil for _, dil in DILATED_CONFIGS])
    buckets = _t5_bucket(delta[None, :] * dils[:, None])
    onehot = (buckets[..., None] == np.arange(N_BUCKETS)).astype(np.float32)
    table = rel_table.reshape(N_BUCKETS, C_GROUPS, 2)
    per_delta = jnp.einsum('gnb,bgh->ghn', onehot, table, precision=lax.Precision.HIGHEST)
    per_delta = jnp.where(np.abs(delta) <= RADIUS, per_delta, NEG_INF)
    reps = -(-(qc * (n - 1)) // n)
    band = jnp.tile(per_delta, (1, 1, reps))[..., :qc * (n - 1)].reshape(C_GROUPS, 2, qc, n - 1)[..., :kc]
    key = np.arange(kc)
    variants = jnp.stack([jnp.where(key < RADIUS, NEG_INF, band), band,
                          jnp.where(key >= kc - RADIUS, NEG_INF, band)], axis=1)
    return variants.reshape(C_GROUPS, 3, 2 * qc, kc)


def _mix_out_kernel(x_ref, ya_ref, yb_ref, o0_ref, o1_ref, o2_ref, l0_ref, l1_ref, l2_ref,
                    wo_ref, bo_ref, xg_ref, qk_ref, vo_ref, xbo_ref,
                    out_ref, nat_ref, xp_ref, wob_ref, *, tm, sub):
    _cast_once(wo_ref, wob_ref)

    dils = [dil for _, dil in DILATED_CONFIGS] * 2
    grp_refs = (o0_ref, o1_ref, o2_ref, l0_ref, l1_ref, l2_ref)

    for s in range(tm // sub):
        rows = slice(s * sub, (s + 1) * sub)

        def natural(slot):
            src_ref, dil = grp_refs[slot], dils[slot]
            if dil == 1:
                return src_ref[rows, :]
            for r in range(dil):
                nat_ref[slot, pl.ds(s * sub + r, sub // dil, stride=dil), :] = (
                    src_ref[s * sub // dil:(s + 1) * sub // dil, r * C_GROUP_WIDTH:(r + 1) * C_GROUP_WIDTH])
            return nat_ref[slot, rows, :]

        l0, l1, l2 = natural(3), natural(4), natural(5)
        m = jnp.maximum(l0, jnp.maximum(l1, l2))
        e0, e1, e2 = jnp.exp(l0 - m), jnp.exp(l1 - m), jnp.exp(l2 - m)
        yc = (e0 * natural(0) + e1 * natural(1) + e2 * natural(2)) / (e0 + e1 + e2)

        y = (_dot(ya_ref[rows, :], wob_ref[0:A_WIDTH, :])
             + _dot(yb_ref[rows, :], wob_ref[A_WIDTH:A_WIDTH + B_WIDTH, :])
             + _dot(yc.astype(BF16), wob_ref[A_WIDTH + B_WIDTH:, :]))
        out_ref[rows, :] = x_ref[rows, :] + y + bo_ref[...]
    out_ref[...] = _cross_attend(out_ref[...], xg_ref, qk_ref, vo_ref, xbo_ref, xp_ref)


def _mix_out(x, ya, yb, o, lse, w_out, b_out, xg, qk, vo, layer, xb_o, *, tm, sub, seq):
    t, d = x.shape
    tiles = seq // tm
    hm = qk.shape[-1]
    row = lambda w: pl.BlockSpec((tm, w), lambda i: (i, 0))
    grp = [pl.BlockSpec((tm // dil, dil * C_GROUP_WIDTH), lambda i: (i, 0)) for _, dil in DILATED_CONFIGS]
    return pl.pallas_call(
        functools.partial(_mix_out_kernel, tm=tm, sub=sub),
        grid=(t // tm,),
        in_specs=[row(d), row(A_WIDTH), row(B_WIDTH), *grp, *grp,
                  _layer_block(w_out, layer), _layer_block(b_out, layer), _layer_block(xg, layer),
                  pl.BlockSpec((None, None, d, hm), lambda i: (layer, i // tiles, 0, 0)),
                  pl.BlockSpec((None, None, hm, d), lambda i: (layer, i // tiles, 0, 0)),
                  _layer_block(xb_o, layer)],
        out_specs=row(d),
        out_shape=jax.ShapeDtypeStruct((t, d), F32),
        scratch_shapes=[pltpu.VMEM((2 * C_GROUPS, tm, C_GROUP_WIDTH), F32),
                        pltpu.VMEM((tm, hm), BF16),
                        pltpu.VMEM(w_out.shape[1:], BF16)],
        compiler_params=_cparams(1, "arbitrary"),
        name="mix_out_xattn",
    )(x, ya, yb, o[0], o[1], o[2], lse[0], lse[1], lse[2], w_out, b_out, xg, qk, vo, xb_o)


def _mem_fold_kernel(mem_ref, g_ref, wkv_ref, wq_ref, wo_ref, qk_ref, vo_ref):
    d = mem_ref.shape[1]
    m = mem_ref.shape[0]
    hd = d // X_HEADS
    mem_n = _rms(mem_ref[...], g_ref[...]).astype(BF16)
    kv = _dot(mem_n, wkv_ref[...].astype(BF16)).astype(BF16)
    for h in range(X_HEADS):
        cols = slice(h * hd, (h + 1) * hd)
        wq_h = wq_ref[:, cols].astype(BF16)
        qk_ref[:, h * m:(h + 1) * m] = (_dot_nt(wq_h, kv[:, cols]) * (hd ** -0.5)).astype(BF16)
        wo_h = wo_ref[cols, :].astype(BF16)
        vo_ref[h * m:(h + 1) * m, :] = _dot(kv[:, d + h * hd:d + (h + 1) * hd], wo_h).astype(BF16)


def _mem_fold(mem, g, w_kv, w_q, w_o):
    batch, m, d = mem.shape
    depth = w_kv.shape[0]

    def per_layer(stack):
        return pl.BlockSpec((None,) + stack.shape[1:], lambda l, b: (l, 0, 0))

    return pl.pallas_call(
        _mem_fold_kernel,
        grid=(depth, batch),
        in_specs=[pl.BlockSpec((None, m, d), lambda l, b: (b, 0, 0)), pl.BlockSpec((1, d), lambda l, b: (0, 0)),
                  per_layer(w_kv), per_layer(w_q), per_layer(w_o)],
        out_specs=[pl.BlockSpec((None, None, d, X_HEADS * m), lambda l, b: (l, b, 0, 0)),
                   pl.BlockSpec((None, None, X_HEADS * m, d), lambda l, b: (l, b, 0, 0))],
        out_shape=[jax.ShapeDtypeStruct((depth, batch, d, X_HEADS * m), BF16),
                   jax.ShapeDtypeStruct((depth, batch, X_HEADS * m, d), BF16)],
        compiler_params=_cparams(2),
        name="mem_fold",
    )(mem, g, w_kv, w_q, w_o)


def _cross_attend(x, g_ref, qk_ref, vo_ref, bo_ref, p_ref):
    hb = _rms(x, g_ref[...]).astype(BF16)
    s_all = _dot(hb, qk_ref[...])
    m_len = qk_ref.shape[1] // X_HEADS
    for h in range(X_HEADS):
        cols = slice(h * m_len, (h + 1) * m_len)
        s = s_all[:, cols]
        m = jnp.max(s, axis=-1, keepdims=True)
        p = jnp.exp(s - m)
        l = jnp.sum(p, axis=-1, keepdims=True)
        p_ref[:, cols] = (p / l).astype(BF16)
    return x + _dot(p_ref[...], vo_ref[...]) + bo_ref[...]


def _ffn_kernel(xp_ref, x_ref, xn_ref, g_ref, wu_hbm, bu_ref, cw_ref, cb_ref, wd_ref, bd_ref, fg_ref,
                out_ref, hb_ref, u0_ref, u1_ref, u2_ref, u3_ref, acc_ref, wdb_ref, wub_ref, stage_ref, sem,
                *, layer, tm, sub, seq, d_ff, cn, final_norm):
    _stream_cast_once(wu_hbm, layer, wub_ref, stage_ref, sem)
    _cast_once(wd_ref, wdb_ref)
    u_bufs = ((u0_ref, u1_ref), (u2_ref, u3_ref))
    tiles = seq // tm
    t = pl.program_id(0) % tiles
    hl = FFN_HALO
    g = g_ref[...]
    hb_ref[...] = jnp.concatenate([_rms(xp_ref[...], g), _rms(x_ref[...], g), _rms(xn_ref[...], g)],
                                  axis=0).astype(BF16)
    n_sub = tm // sub
    rows = sub + 2 * hl
    n_chunks = d_ff // cn

    def conv(u_ref, ue, cols, a):
        bias = bu_ref[:, cols]
        c0, c1, c2 = cw_ref[0:1, cols], cw_ref[1:2, cols], cw_ref[2:3, cols]
        u_ref[0, 1:1 + rows, :] = ue
        u_ref[1, hl - 1:hl - 1 + rows, :] = ue
        if a == 0:
            u_ref[0, hl:hl + 1, :] = jnp.where(t > 0, ue[hl - 1:hl, :], -bias)
        if a == n_sub - 1:
            u_ref[1, 2 * hl + sub - 1:2 * hl + sub, :] = jnp.where(t < tiles - 1, ue[hl + sub:hl + sub + 1, :], -bias)
        return (c0 * u_ref[0, hl:hl + sub, :] + c1 * ue[hl:hl + sub, :] + c2 * u_ref[1, 2 * hl:2 * hl + sub, :]
                + (bias * (c0 + c1 + c2) + cb_ref[:, cols]))

    def up(a, j):
        hb = hb_ref[a * sub:a * sub + rows, :]
        return (_dot(hb, wub_ref[:, j * cn:(j + 1) * cn]), _dot(hb, wub_ref[:, d_ff + j * cn:d_ff + (j + 1) * cn]))

    def down(a, j, act):
        part = _dot(act, wdb_ref[j * cn:(j + 1) * cn, :])
        if j == 0:
            acc_ref[...] = part
        else:
            acc_ref[...] += part
        if j == n_chunks - 1:
            out_rows = slice(a * sub, (a + 1) * sub)
            y = x_ref[out_rows, :] + acc_ref[...] + bd_ref[...]
            if final_norm:
                y = _rms(y, fg_ref[...])
            out_ref[out_rows, :] = y

    items = [(a, j) for a in range(n_sub) for j in range(n_chunks)]
    ue = up(*items[0])
    act_prev = None
    for idx, (a, j) in enumerate(items):
        ue_next = up(*items[idx + 1]) if idx + 1 < len(items) else None
        if act_prev is not None:
            down(*items[idx - 1], act_prev)
        gate = conv(u_bufs[idx % 2][0], ue[0], slice(j * cn, (j + 1) * cn), a)
        val = conv(u_bufs[idx % 2][1], ue[1], slice(d_ff + j * cn, d_ff + (j + 1) * cn), a)
        act_prev = (gate * jax.nn.sigmoid(gate) * val).astype(BF16)
        ue = ue_next
    down(*items[-1], act_prev)


def _ffn(x, g, w_up, layer, b_up, conv_w, conv_b, w_down, b_down, final_g, *, tm, sub, seq, cn, final_norm):
    t, d = x.shape
    d_ff = w_down.shape[1]
    hl = FFN_HALO
    blk_per_tile = tm // hl
    n_hl = t // hl
    row = pl.BlockSpec((tm, d), lambda i: (i, 0))
    return pl.pallas_call(
        functools.partial(_ffn_kernel, layer=layer, tm=tm, sub=sub, seq=seq, d_ff=d_ff, cn=cn,
                          final_norm=final_norm),
        grid=(t // tm,),
        in_specs=[pl.BlockSpec((hl, d), lambda i: (jnp.maximum(i * blk_per_tile - 1, 0), 0)),
                  row,
                  pl.BlockSpec((hl, d), lambda i: (jnp.minimum((i + 1) * blk_per_tile, n_hl - 1), 0)),
                  _layer_block(g, layer), pl.BlockSpec(memory_space=pl.ANY), _layer_block(b_up, layer),
                  _layer_block(conv_w, layer), _layer_block(conv_b, layer), _layer_block(w_down, layer),
                  _layer_block(b_down, layer), _full((1, d))],
        out_specs=row,
        out_shape=jax.ShapeDtypeStruct((t, d), F32),
        scratch_shapes=([pltpu.VMEM((tm + 2 * hl, d), BF16)] + [pltpu.VMEM((2, sub + 3 * hl, cn), F32)] * 4
                        + [pltpu.VMEM((sub, d), F32), pltpu.VMEM((d_ff, d), BF16), pltpu.VMEM((d, 2 * d_ff), BF16),
                           pltpu.VMEM((2, d, FFN_STREAM_PANEL), F32), pltpu.SemaphoreType.DMA((2,))]),
        compiler_params=_cparams(1, "arbitrary"),
        name="conv_ffn",
    )(x, x, x, g, w_up, b_up, conv_w, conv_b, w_down, b_down, final_g)


def _block_diag(blocks):
    *lead, n, a, b = blocks.shape
    eye = jnp.eye(n, dtype=blocks.dtype)
    return (eye[:, None, :, None] * blocks[..., :, :, None, :]).reshape(*lead, n * a, n * b)


def kernel(x, mem, rel_table, mem_norm_g, norm_mix_g, w_in, b_in, gmlp_v_g, gmlp_w_s, gmlp_b_s, pool_w, pool_b, pool_scale, w_out, b_out, norm_mem_g, xattn_w_q, xattn_w_kv, xattn_w_o, xattn_b_o, norm_ffn_g, ffn_w_up, ffn_b_up, ffn_conv_w, ffn_conv_b, ffn_w_down, ffn_b_down, final_norm_g):
    batch, seq, d = x.shape
    mem_len = mem.shape[1]
    depth = w_in.shape[0]
    t = batch * seq
    assert all(w // (2 * dil) == RADIUS for w, dil in DILATED_CONFIGS)
    assert x.dtype == F32 and all(seq % rows == 0 for rows in (MIX_IN_ROWS, MIX_OUT_ROWS, FFN_ROWS))
    assert FFN_ROWS % FFN_PASS_ROWS == 0 and ffn_w_down.shape[1] % FFN_CHUNK == 0
    assert ffn_w_up.shape[2] % FFN_STREAM_PANEL == 0

    rows = lambda a: a.reshape(depth, 1, -1)
    ones_bd = jnp.asarray(np.kron(np.eye(A_HEADS), np.ones((HEAD_DIM, HEAD_DIM))), BF16)
    biases = _attn_biases(rel_table)
    qk, vo = _mem_fold(mem, mem_norm_g.reshape(1, d), xattn_w_kv, xattn_w_q, xattn_w_o)
    ws_cat = (gmlp_w_s.reshape(depth, A_HEADS // 2, 2, GMLP_CHUNK, GMLP_CHUNK).transpose(0, 1, 3, 2, 4)
              .reshape(depth, A_HEADS // 2, GMLP_CHUNK, 2 * GMLP_CHUNK).astype(BF16))
    bs_full = jnp.repeat(jnp.swapaxes(gmlp_b_s, 1, 2), HEAD_DIM, axis=2)
    wp_bd = _block_diag(pool_w).astype(BF16)

    xf = x.reshape(t, d)
    for l in range(depth):
        ya, yb, *qkv = _mix_in(xf, rows(norm_mix_g), w_in, l, rows(b_in), rows(gmlp_v_g), ws_cat, bs_full, ones_bd,
                               wp_bd, rows(pool_b), rows(pool_scale), tm=MIX_IN_ROWS, seq=seq)
        o, lse = [], []
        for g, (_, dil) in enumerate(DILATED_CONFIGS):
            n_sub = min(dil, ATTN_MAX_SUBSEQ)
            og, lg = _attn(qkv[g], biases, g, dilation=dil, batch=batch, seq=seq,
                           lq=ATTN_CHUNKS_PER_STEP * 2 * RADIUS // n_sub, n_sub=n_sub)
            o.append(og)
            lse.append(lg)
        xf = _mix_out(xf, ya, yb, o, lse, w_out, rows(b_out), rows(norm_mem_g), qk, vo, l, rows(xattn_b_o),
                      tm=MIX_OUT_ROWS, sub=MIX_OUT_PASS_ROWS, seq=seq)
        xf = _ffn(xf, rows(norm_ffn_g), ffn_w_up, l, rows(ffn_b_up), ffn_conv_w, rows(ffn_conv_b), ffn_w_down,
                  rows(ffn_b_down), final_norm_g.reshape(1, d),
                  tm=FFN_ROWS, sub=FFN_PASS_ROWS, seq=seq, cn=FFN_CHUNK, final_norm=(l == depth - 1))
    return xf.reshape(batch, seq, d)
```

```python
import functools
import math

import numpy as np
import jax
import jax.numpy as jnp
from jax import lax
from jax.experimental import pallas as pl
from jax.experimental.pallas import tpu as pltpu

F32 = jnp.float32
BF16 = jnp.bfloat16

HEAD_DIM = 64
A_HEADS = 6
A_WIDTH = A_HEADS * HEAD_DIM
GMLP_CHUNK = 128
POOL_WINDOWS = (2, 4, 8, 16)
B_GROUP_DIM = 64
B_WIDTH = len(POOL_WINDOWS) * B_GROUP_DIM
DILATED_CONFIGS = ((128, 1), (512, 4), (2048, 16))
C_GROUPS = len(DILATED_CONFIGS)
C_GROUP_WIDTH = 2 * HEAD_DIM
C_WIDTH = C_GROUPS * C_GROUP_WIDTH
RADIUS = 64
N_BUCKETS = 32
MAX_DISTANCE = 1024
X_HEADS = 4
EPS = 1e-6
NEG_INF = -1e30

LANES = 128
POOL_HALO = 8
FFN_HALO = 8
VMEM_LIMIT = 56 * 1024 * 1024

MIX_IN_ROWS = 1024
MIX_OUT_ROWS = 1024
MIX_OUT_PASS_ROWS = 256
FFN_ROWS = 512
FFN_PASS_ROWS = 256
FFN_CHUNK = 256


def _cparams(n_axes, semantics="parallel"):
    return pltpu.CompilerParams(dimension_semantics=(semantics,) * n_axes, vmem_limit_bytes=VMEM_LIMIT)


def _layer_block(stacked, layer):
    return pl.BlockSpec((None,) + stacked.shape[1:], lambda *_: (layer,) + (0,) * (stacked.ndim - 1))


def _rms(x, g):
    ms = jnp.mean(x * x, axis=-1, keepdims=True)
    return x * lax.rsqrt(ms + EPS) * g


def _gelu_tanh(x):
    return x * (0.5 * (1.0 + jnp.tanh(math.sqrt(2.0 / math.pi) * (x + 0.044715 * (x * x * x)))))


def _dot(a, b):
    return jnp.dot(a, b, preferred_element_type=F32)


def _cast_once(src_ref, dst_ref):
    @pl.when(pl.program_id(0) == 0)
    def _():
        dst_ref[...] = src_ref[...].astype(BF16)


def _dot_nt(a, b):
    return lax.dot_general(a, b, (((1,), (1,)), ((), ())), preferred_element_type=F32)


def _full(shape):
    return pl.BlockSpec(shape, lambda *_: (0,) * len(shape))


def _pool_minus_identity(z_prev, z_main, z_next, zx_ref, ts_ref, t, tiles, tm, seq):
    hl = POOL_HALO
    pad = hl
    ext = tm + 2 * hl
    assert POOL_WINDOWS == tuple(2 ** (k + 1) for k in range(len(POOL_WINDOWS))) and POOL_WINDOWS[-1] // 2 <= pad
    zx_ref[0:pad] = jnp.zeros((pad, B_WIDTH), F32)
    zx_ref[pad:pad + hl] = jnp.where(t > 0, z_prev, 0.0)
    zx_ref[pad + hl:pad + hl + tm] = z_main
    zx_ref[pad + hl + tm:] = jnp.where(t < tiles - 1, z_next, 0.0)
    ts_ref[:, 0:pad] = jnp.zeros((len(POOL_WINDOWS), pad, B_WIDTH), F32)

    pos = t * tm + lax.broadcasted_iota(jnp.int32, (tm, LANES), 0)
    first_group = lax.broadcasted_iota(jnp.int32, (tm, LANES), 1) < B_GROUP_DIM
    pooled = []
    for cb in range(B_WIDTH // LANES):
        cols = slice(cb * LANES, (cb + 1) * LANES)
        w_a, w_b = POOL_WINDOWS[2 * cb], POOL_WINDOWS[2 * cb + 1]
        centred = {}
        for k, w in enumerate(POOL_WINDOWS):
            if w > w_b:
                break
            src = zx_ref if k == 0 else ts_ref.at[k - 1]
            shift = w // 2
            ts_ref[k, pad:pad + ext, cols] = src[pad:pad + ext, cols] + src[pad - shift:pad - shift + ext, cols]
            if w in (w_a, w_b):
                start = pad + hl + w // 2 - 1
                centred[w] = ts_ref[k, start:start + tm, cols]
        half = jnp.where(first_group, w_a // 2, w_b // 2)
        cnt = (jnp.minimum(pos + half, seq) - jnp.maximum(pos - half, 0)).astype(F32)
        wsum = jnp.where(first_group, centred[w_a], centred[w_b])
        pooled.append(wsum / cnt - zx_ref[pad + hl:pad + hl + tm, cols])
    return jnp.concatenate(pooled, axis=-1)


def _mix_in_kernel(xp_ref, x_ref, xn_ref, g_ref, w_ref, b_ref, vg_ref, ws_ref, bs_ref, ones_ref,
                   wp_ref, bp_ref, ps_ref,
                   ya_ref, yb_ref, g0_ref, g1_ref, g2_ref, qkv_ref, wb_ref, zx_ref, ts_ref, *, tm, seq):
    _cast_once(w_ref, wb_ref)
    tiles = seq // tm
    t = pl.program_id(0) % tiles
    g = g_ref[...]
    hb = _rms(x_ref[...], g).astype(BF16)

    def seg(a, b):
        return _dot(hb, wb_ref[:, a:b]) + b_ref[:, a:b]

    off_b = 2 * A_WIDTH
    off_c = off_b + B_WIDTH
    uv = seg(0, off_b)
    hl = POOL_HALO
    h_halo = jnp.concatenate([_rms(xp_ref[...], g), _rms(xn_ref[...], g)], axis=0).astype(BF16)
    z_halo = _dot(h_halo, wb_ref[:, off_b:off_c]) + b_ref[:, off_b:off_c]
    z_main = seg(off_b, off_c)
    qk = seg(off_c, off_c + 2 * C_WIDTH)
    parts = (qk[:, :C_WIDTH] * (HEAD_DIM ** -0.5), qk[:, C_WIDTH:], seg(off_c + 2 * C_WIDTH, off_c + 3 * C_WIDTH))

    pooled = _pool_minus_identity(z_halo[0:hl], z_main, z_halo[hl:], zx_ref, ts_ref, t, tiles, tm, seq).astype(BF16)
    yb_ref[...] = ((_dot(pooled, wp_ref[...]) + bp_ref[...]) * ps_ref[...]).astype(BF16)

    for grp, (out_ref, (_, dil)) in enumerate(zip((g0_ref, g1_ref, g2_ref), DILATED_CONFIGS)):
        for part in range(3):
            tile = parts[part][:, grp * C_GROUP_WIDTH:(grp + 1) * C_GROUP_WIDTH]
            if dil == 1:
                out_ref[part] = tile.astype(BF16)
                continue
            qkv_ref[part, grp] = tile
            for r in range(dil):
                piece = qkv_ref[part, grp, pl.ds(r, tm // dil, stride=dil), :]
                out_ref[part, :, r * C_GROUP_WIDTH:(r + 1) * C_GROUP_WIDTH] = piece.astype(BF16)

    uv = _gelu_tanh(uv)
    u, v = uv[:, :A_WIDTH], uv[:, A_WIDTH:]
    vsq = (v * v).astype(BF16)
    cut = 2 * LANES
    ssum = jnp.concatenate([_dot(vsq[:, :cut], ones_ref[:cut, :cut]), _dot(vsq[:, cut:], ones_ref[cut:, cut:])],
                           axis=1)
    vn = v * lax.rsqrt(ssum * (1.0 / HEAD_DIM) + EPS) * vg_ref[...]
    first_head = lax.broadcasted_iota(jnp.int32, (GMLP_CHUNK, LANES), 1) < HEAD_DIM
    chunk_rows = [slice(c * GMLP_CHUNK, (c + 1) * GMLP_CHUNK) for c in range(tm // GMLP_CHUNK)]
    for j in range(A_WIDTH // LANES):
        cols = slice(j * LANES, (j + 1) * LANES)
        rhs = jnp.concatenate(
            [jnp.concatenate([jnp.where(first_head, vn[rows, cols], 0.0).astype(BF16),
                              jnp.where(first_head, 0.0, vn[rows, cols]).astype(BF16)], axis=0)
             for rows in chunk_rows], axis=1)
        s = _dot(ws_ref[j], rhs)
        for c, rows in enumerate(chunk_rows):
            s_c = s[:, c * LANES:(c + 1) * LANES] + bs_ref[:, cols]
            ya_ref[rows, cols] = (u[rows, cols] * s_c).astype(BF16)


def _mix_in(x, g, w_in, layer, b_in, v_gain, ws_cat, bs_full, ones_bd, wp_bd, bp, ps, *, tm, seq):
    t, d = x.shape
    in_width = w_in.shape[2]
    hl = POOL_HALO
    blk_per_tile = tm // hl
    n_hl = t // hl
    qkv_shapes = [jax.ShapeDtypeStruct((3, t // dil, dil * C_GROUP_WIDTH), BF16) for _, dil in DILATED_CONFIGS]
    qkv_specs = [pl.BlockSpec((3, tm // dil, dil * C_GROUP_WIDTH), lambda i: (0, i, 0))
                 for _, dil in DILATED_CONFIGS]
    return pl.pallas_call(
        functools.partial(_mix_in_kernel, tm=tm, seq=seq),
        grid=(t // tm,),
        in_specs=[pl.BlockSpec((hl, d), lambda i: (jnp.maximum(i * blk_per_tile - 1, 0), 0)),
                  pl.BlockSpec((tm, d), lambda i: (i, 0)),
                  pl.BlockSpec((hl, d), lambda i: (jnp.minimum((i + 1) * blk_per_tile, n_hl - 1), 0)),
                  _layer_block(g, layer), _layer_block(w_in, layer), _layer_block(b_in, layer),
                  _layer_block(v_gain, layer), _layer_block(ws_cat, layer), _layer_block(bs_full, layer),
                  _full(ones_bd.shape), _layer_block(wp_bd, layer), _layer_block(bp, layer),
                  _layer_block(ps, layer)],
        out_specs=[pl.BlockSpec((tm, A_WIDTH), lambda i: (i, 0)),
                   pl.BlockSpec((tm, B_WIDTH), lambda i: (i, 0))] + qkv_specs,
        out_shape=[jax.ShapeDtypeStruct((t, A_WIDTH), BF16),
                   jax.ShapeDtypeStruct((t, B_WIDTH), BF16)] + qkv_shapes,
        scratch_shapes=[pltpu.VMEM((3, C_GROUPS, tm, C_GROUP_WIDTH), F32), pltpu.VMEM((d, in_width), BF16),
                        pltpu.VMEM((tm + 3 * hl, B_WIDTH), F32),
                        pltpu.VMEM((len(POOL_WINDOWS), tm + 3 * hl, B_WIDTH), F32)],
        compiler_params=_cparams(1, "arbitrary"),
        name="mix_in",
    )(x, x, x, g, w_in, b_in, v_gain, ws_cat, bs_full, ones_bd, wp_bd, bp, ps)


def _attn_kernel(g0_ref, g1_ref, g2_ref, bias_ref, yc_ref, kx0_ref, vx0_ref, kx1_ref, vx1_ref, kx2_ref, vx2_ref,
                 nat_ref, *, seq):
    qc = 2 * RADIUS
    kc = qc + 2 * RADIUS
    first_head = lax.broadcasted_iota(jnp.int32, (qc, LANES), 1) < HEAD_DIM

    def per_lane_head(a):
        a = jnp.broadcast_to(a, (2 * qc, LANES))
        return jnp.where(first_head, a[:qc], a[qc:])

    groups = zip((g0_ref, g1_ref, g2_ref), (kx0_ref, kx1_ref, kx2_ref), (vx0_ref, vx1_ref, vx2_ref), DILATED_CONFIGS)
    for grp, (qkv_ref, kx_ref, vx_ref, (_, dil)) in enumerate(groups):
        sub_len = seq // dil
        n_chunks = sub_len // qc
        halo = jnp.zeros((RADIUS, dil * LANES), BF16)
        for x_ref, part in ((kx_ref, 1), (vx_ref, 2)):
            x_ref[0:RADIUS] = halo
            x_ref[RADIUS:RADIUS + sub_len] = qkv_ref[part]
            x_ref[RADIUS + sub_len:] = halo
        for c in range(n_chunks):
            rows = slice(c * qc, (c + 1) * qc)
            variant = 0 if c == 0 else (2 if c == n_chunks - 1 else 1)
            for r in range(dil):
                lanes = slice(r * LANES, (r + 1) * LANES)
                q = qkv_ref[0, rows, lanes]
                keys = kx_ref[c * qc:c * qc + kc, lanes]
                vals = vx_ref[c * qc:c * qc + kc, lanes]
                zero = jnp.zeros_like(q)
                q2 = jnp.concatenate([jnp.where(first_head, q, zero), jnp.where(first_head, zero, q)], axis=0)
                s = _dot_nt(q2, keys) + bias_ref[grp, variant]
                m = jnp.max(s, axis=-1, keepdims=True)
                p = jnp.exp(s - m).astype(BF16)
                o2 = _dot(p, jnp.concatenate([vals, jnp.ones_like(vals)], axis=1))
                l_both = per_lane_head(o2[:, LANES:])
                natural = slice(c * qc, (c + 1) * qc) if dil == 1 else pl.ds(c * qc * dil + r, qc, stride=dil)
                nat_ref[2 * grp, natural, :] = per_lane_head(o2[:, :LANES]) / l_both
                nat_ref[2 * grp + 1, natural, :] = per_lane_head(m) + jnp.log(l_both)

    l0, l1, l2 = nat_ref[1], nat_ref[3], nat_ref[5]
    m = jnp.maximum(l0, jnp.maximum(l1, l2))
    e0, e1, e2 = jnp.exp(l0 - m), jnp.exp(l1 - m), jnp.exp(l2 - m)
    yc_ref[...] = ((e0 * nat_ref[0] + e1 * nat_ref[2] + e2 * nat_ref[4]) / (e0 + e1 + e2)).astype(BF16)


def _attn(qkv, biases, *, batch, seq):
    assert C_GROUPS == 3 and all(seq // dil >= 4 * RADIUS for _, dil in DILATED_CONFIGS)
    views, specs, scratch = [], [], []
    for arr, (_, dil) in zip(qkv, DILATED_CONFIGS):
        sub_len, width = seq // dil, dil * C_GROUP_WIDTH
        views.append(arr.reshape(3, batch, sub_len, width))
        specs.append(pl.BlockSpec((3, None, sub_len, width), lambda b: (0, b, 0, 0)))
        scratch += [pltpu.VMEM((sub_len + 2 * RADIUS, width), BF16)] * 2
    yc = pl.pallas_call(
        functools.partial(_attn_kernel, seq=seq),
        grid=(batch,),
        in_specs=specs + [_full(biases.shape)],
        out_specs=pl.BlockSpec((None, seq, C_GROUP_WIDTH), lambda b: (b, 0, 0)),
        out_shape=jax.ShapeDtypeStruct((batch, seq, C_GROUP_WIDTH), BF16),
        scratch_shapes=scratch + [pltpu.VMEM((2 * C_GROUPS, seq, C_GROUP_WIDTH), F32)],
        compiler_params=_cparams(1),
        name="dilated_attn",
    )(*views, biases)
    return yc.reshape(batch * seq, C_GROUP_WIDTH)


def _t5_bucket(rel):
    nb = N_BUCKETS // 2
    ret = (rel > 0).astype(np.int32) * nb
    n = np.abs(rel)
    max_exact = nb // 2
    large = max_exact + (np.log(np.maximum(n, 1) / max_exact)
                         / math.log(MAX_DISTANCE / max_exact) * (nb - max_exact)).astype(np.int32)
    large = np.minimum(large, nb - 1)
    return ret + np.where(n < max_exact, n, large)


def _attn_biases(rel_table):
    qc, kc = 2 * RADIUS, 4 * RADIUS
    n = qc + kc - 1
    delta = (np.arange(n) + qc - 1) % n - (qc + RADIUS - 1)
    dils = np.array([dil for _, dil in DILATED_CONFIGS])
    buckets = _t5_bucket(delta[None, :] * dils[:, None])
    onehot = (buckets[..., None] == np.arange(N_BUCKETS)).astype(np.float32)
    table = rel_table.reshape(N_BUCKETS, C_GROUPS, 2)
    per_delta = jnp.einsum('gnb,bgh->ghn', onehot, table, precision=lax.Precision.HIGHEST)
    per_delta = jnp.where(np.abs(delta) <= RADIUS, per_delta, NEG_INF)
    reps = -(-(qc * (n - 1)) // n)
    band = jnp.tile(per_delta, (1, 1, reps))[..., :qc * (n - 1)].reshape(C_GROUPS, 2, qc, n - 1)[..., :kc]
    key = np.arange(kc)
    variants = jnp.stack([jnp.where(key < RADIUS, NEG_INF, band), band,
                          jnp.where(key >= kc - RADIUS, NEG_INF, band)], axis=1)
    return variants.reshape(C_GROUPS, 3, 2 * qc, kc)


def _mix_out_kernel(x_ref, ya_ref, yb_ref, yc_ref, wo_ref, bo_ref, xg_ref, qk_ref, vo_ref, xbo_ref,
                    out_ref, xp_ref, wob_ref, *, tm, sub):
    _cast_once(wo_ref, wob_ref)

    for s in range(tm // sub):
        rows = slice(s * sub, (s + 1) * sub)
        y = (_dot(ya_ref[rows, :], wob_ref[0:A_WIDTH, :])
             + _dot(yb_ref[rows, :], wob_ref[A_WIDTH:A_WIDTH + B_WIDTH, :])
             + _dot(yc_ref[rows, :], wob_ref[A_WIDTH + B_WIDTH:, :]))
        out_ref[rows, :] = x_ref[rows, :] + y + bo_ref[...]
    out_ref[...] = _cross_attend(out_ref[...], xg_ref, qk_ref, vo_ref, xbo_ref, xp_ref)


def _mix_out(x, ya, yb, yc, w_out, b_out, xg, qk, vo, layer, xb_o, *, tm, sub, seq):
    t, d = x.shape
    tiles = seq // tm
    hm = qk.shape[-1]
    row = lambda w: pl.BlockSpec((tm, w), lambda i: (i, 0))
    return pl.pallas_call(
        functools.partial(_mix_out_kernel, tm=tm, sub=sub),
        grid=(t // tm,),
        in_specs=[row(d), row(A_WIDTH), row(B_WIDTH), row(C_GROUP_WIDTH),
                  _layer_block(w_out, layer), _layer_block(b_out, layer), _layer_block(xg, layer),
                  pl.BlockSpec((None, None, d, hm), lambda i: (layer, i // tiles, 0, 0)),
                  pl.BlockSpec((None, None, hm, d), lambda i: (layer, i // tiles, 0, 0)),
                  _layer_block(xb_o, layer)],
        out_specs=row(d),
        out_shape=jax.ShapeDtypeStruct((t, d), F32),
        scratch_shapes=[pltpu.VMEM((tm, hm), BF16),
                        pltpu.VMEM(w_out.shape[1:], BF16)],
        compiler_params=_cparams(1, "arbitrary"),
        name="mix_out_xattn",
    )(x, ya, yb, yc, w_out, b_out, xg, qk, vo, xb_o)


def _mem_fold_kernel(mem_ref, g_ref, wkv_ref, wq_ref, wo_ref, qk_ref, vo_ref):
    d = mem_ref.shape[1]
    m = mem_ref.shape[0]
    hd = d // X_HEADS
    mem_n = _rms(mem_ref[...], g_ref[...]).astype(BF16)
    kv = _dot(mem_n, wkv_ref[...].astype(BF16)).astype(BF16)
    for h in range(X_HEADS):
        cols = slice(h * hd, (h + 1) * hd)
        wq_h = wq_ref[:, cols].astype(BF16)
        qk_ref[:, h * m:(h + 1) * m] = (_dot_nt(wq_h, kv[:, cols]) * (hd ** -0.5)).astype(BF16)
        wo_h = wo_ref[cols, :].astype(BF16)
        vo_ref[h * m:(h + 1) * m, :] = _dot(kv[:, d + h * hd:d + (h + 1) * hd], wo_h).astype(BF16)


def _mem_fold(mem, g, w_kv, w_q, w_o):
    batch, m, d = mem.shape
    depth = w_kv.shape[0]

    def per_layer(stack):
        return pl.BlockSpec((None,) + stack.shape[1:], lambda l, b: (l, 0, 0))

    return pl.pallas_call(
        _mem_fold_kernel,
        grid=(depth, batch),
        in_specs=[pl.BlockSpec((None, m, d), lambda l, b: (b, 0, 0)), pl.BlockSpec((1, d), lambda l, b: (0, 0)),
                  per_layer(w_kv), per_layer(w_q), per_layer(w_o)],
        out_specs=[pl.BlockSpec((None, None, d, X_HEADS * m), lambda l, b: (l, b, 0, 0)),
                   pl.BlockSpec((None, None, X_HEADS * m, d), lambda l, b: (l, b, 0, 0))],
        out_shape=[jax.ShapeDtypeStruct((depth, batch, d, X_HEADS * m), BF16),
                   jax.ShapeDtypeStruct((depth, batch, X_HEADS * m, d), BF16)],
        compiler_params=_cparams(2),
        name="mem_fold",
    )(mem, g, w_kv, w_q, w_o)


def _cross_attend(x, g_ref, qk_ref, vo_ref, bo_ref, p_ref):
    hb = _rms(x, g_ref[...]).astype(BF16)
    s_all = _dot(hb, qk_ref[...])
    m_len = qk_ref.shape[1] // X_HEADS
    for h in range(X_HEADS):
        cols = slice(h * m_len, (h + 1) * m_len)
        s = s_all[:, cols]
        m = jnp.max(s, axis=-1, keepdims=True)
        p = jnp.exp(s - m)
        l = jnp.sum(p, axis=-1, keepdims=True)
        p_ref[:, cols] = (p / l).astype(BF16)
    return x + _dot(p_ref[...], vo_ref[...]) + bo_ref[...]


def _ffn_kernel(xp_ref, x_ref, xn_ref, g_ref, wu_ref, bu_ref, cw_ref, cb_ref, wd_ref, bd_ref, fg_ref,
                out_ref, hb_ref, u0_ref, u1_ref, u2_ref, u3_ref, acc_ref, wdb_ref, *, tm, sub, seq, d_ff, cn,
                final_norm):
    _cast_once(wd_ref, wdb_ref)
    u_bufs = ((u0_ref, u1_ref), (u2_ref, u3_ref))
    tiles = seq // tm
    t = pl.program_id(0) % tiles
    hl = FFN_HALO
    g = g_ref[...]
    hb_ref[...] = jnp.concatenate([_rms(xp_ref[...], g), _rms(x_ref[...], g), _rms(xn_ref[...], g)],
                                  axis=0).astype(BF16)
    n_sub = tm // sub
    rows = sub + 2 * hl
    n_chunks = d_ff // cn

    def conv(u_ref, ue, cols, a):
        bias = bu_ref[:, cols]
        c0, c1, c2 = cw_ref[0:1, cols], cw_ref[1:2, cols], cw_ref[2:3, cols]
        u_ref[0, 1:1 + rows, :] = ue
        u_ref[1, hl - 1:hl - 1 + rows, :] = ue
        if a == 0:
            u_ref[0, hl:hl + 1, :] = jnp.where(t > 0, ue[hl - 1:hl, :], -bias)
        if a == n_sub - 1:
            u_ref[1, 2 * hl + sub - 1:2 * hl + sub, :] = jnp.where(t < tiles - 1, ue[hl + sub:hl + sub + 1, :], -bias)
        return (c0 * u_ref[0, hl:hl + sub, :] + c1 * ue[hl:hl + sub, :] + c2 * u_ref[1, 2 * hl:2 * hl + sub, :]
                + (bias * (c0 + c1 + c2) + cb_ref[:, cols]))

    def up(a, j):
        hb = hb_ref[a * sub:a * sub + rows, :]
        return (_dot(hb, wu_ref[:, j * cn:(j + 1) * cn]), _dot(hb, wu_ref[:, d_ff + j * cn:d_ff + (j + 1) * cn]))

    def down(a, j, act):
        part = _dot(act, wdb_ref[j * cn:(j + 1) * cn, :])
        if j == 0:
            acc_ref[...] = part
        else:
            acc_ref[...] += part
        if j == n_chunks - 1:
            out_rows = slice(a * sub, (a + 1) * sub)
            y = x_ref[out_rows, :] + acc_ref[...] + bd_ref[...]
            if final_norm:
                y = _rms(y, fg_ref[...])
            out_ref[out_rows, :] = y

    items = [(a, j) for a in range(n_sub) for j in range(n_chunks)]
    ue = up(*items[0])
    act_prev = None
    for idx, (a, j) in enumerate(items):
        ue_next = up(*items[idx + 1]) if idx + 1 < len(items) else None
        if act_prev is not None:
            down(*items[idx - 1], act_prev)
        gate = conv(u_bufs[idx % 2][0], ue[0], slice(j * cn, (j + 1) * cn), a)
        val = conv(u_bufs[idx % 2][1], ue[1], slice(d_ff + j * cn, d_ff + (j + 1) * cn), a)
        act_prev = (gate * jax.nn.sigmoid(gate) * val).astype(BF16)
        ue = ue_next
    down(*items[-1], act_prev)


def _ffn(x, g, w_up, layer, b_up, conv_w, conv_b, w_down, b_down, final_g, *, tm, sub, seq, cn, final_norm):
    t, d = x.shape
    d_ff = w_down.shape[1]
    hl = FFN_HALO
    blk_per_tile = tm // hl
    n_hl = t // hl
    row = pl.BlockSpec((tm, d), lambda i: (i, 0))
    return pl.pallas_call(
        functools.partial(_ffn_kernel, tm=tm, sub=sub, seq=seq, d_ff=d_ff, cn=cn, final_norm=final_norm),
        grid=(t // tm,),
        in_specs=[pl.BlockSpec((hl, d), lambda i: (jnp.maximum(i * blk_per_tile - 1, 0), 0)),
                  row,
                  pl.BlockSpec((hl, d), lambda i: (jnp.minimum((i + 1) * blk_per_tile, n_hl - 1), 0)),
                  _layer_block(g, layer), _layer_block(w_up, layer), _layer_block(b_up, layer),
                  _layer_block(conv_w, layer), _layer_block(conv_b, layer), _layer_block(w_down, layer),
                  _layer_block(b_down, layer), _full((1, d))],
        out_specs=row,
        out_shape=jax.ShapeDtypeStruct((t, d), F32),
        scratch_shapes=([pltpu.VMEM((tm + 2 * hl, d), BF16)] + [pltpu.VMEM((2, sub + 3 * hl, cn), F32)] * 4
                        + [pltpu.VMEM((sub, d), F32), pltpu.VMEM((d_ff, d), BF16)]),
        compiler_params=_cparams(1, "arbitrary"),
        name="conv_ffn",
    )(x, x, x, g, w_up, b_up, conv_w, conv_b, w_down, b_down, final_g)


def _block_diag(blocks):
    *lead, n, a, b = blocks.shape
    eye = jnp.eye(n, dtype=blocks.dtype)
    return (eye[:, None, :, None] * blocks[..., :, :, None, :]).reshape(*lead, n * a, n * b)


def kernel(x, mem, rel_table, mem_norm_g, norm_mix_g, w_in, b_in, gmlp_v_g, gmlp_w_s, gmlp_b_s, pool_w, pool_b, pool_scale, w_out, b_out, norm_mem_g, xattn_w_q, xattn_w_kv, xattn_w_o, xattn_b_o, norm_ffn_g, ffn_w_up, ffn_b_up, ffn_conv_w, ffn_conv_b, ffn_w_down, ffn_b_down, final_norm_g):
    batch, seq, d = x.shape
    mem_len = mem.shape[1]
    depth = w_in.shape[0]
    t = batch * seq
    assert all(w // (2 * dil) == RADIUS for w, dil in DILATED_CONFIGS)
    assert x.dtype == F32 and all(seq % rows == 0 for rows in (MIX_IN_ROWS, MIX_OUT_ROWS, FFN_ROWS))
    assert FFN_ROWS % FFN_PASS_ROWS == 0 and ffn_w_down.shape[1] % FFN_CHUNK == 0

    rows = lambda a: a.reshape(depth, 1, -1)
    ones_bd = jnp.asarray(np.kron(np.eye(A_HEADS), np.ones((HEAD_DIM, HEAD_DIM))), BF16)
    biases = _attn_biases(rel_table)
    qk, vo = _mem_fold(mem, mem_norm_g.reshape(1, d), xattn_w_kv, xattn_w_q, xattn_w_o)
    w_up_bf16 = ffn_w_up.astype(BF16)
    ws_cat = (gmlp_w_s.reshape(depth, A_HEADS // 2, 2, GMLP_CHUNK, GMLP_CHUNK).transpose(0, 1, 3, 2, 4)
              .reshape(depth, A_HEADS // 2, GMLP_CHUNK, 2 * GMLP_CHUNK).astype(BF16))
    bs_full = jnp.repeat(jnp.swapaxes(gmlp_b_s, 1, 2), HEAD_DIM, axis=2)
    wp_bd = _block_diag(pool_w).astype(BF16)

    xf = x.reshape(t, d)
    for l in range(depth):
        ya, yb, *qkv = _mix_in(xf, rows(norm_mix_g), w_in, l, rows(b_in), rows(gmlp_v_g), ws_cat, bs_full, ones_bd,
                               wp_bd, rows(pool_b), rows(pool_scale), tm=MIX_IN_ROWS, seq=seq)
        yc = _attn(qkv, biases, batch=batch, seq=seq)
        xf = _mix_out(xf, ya, yb, yc, w_out, rows(b_out), rows(norm_mem_g), qk, vo, l, rows(xattn_b_o),
                      tm=MIX_OUT_ROWS, sub=MIX_OUT_PASS_ROWS, seq=seq)
        xf = _ffn(xf, rows(norm_ffn_g), w_up_bf16, l, rows(ffn_b_up), ffn_conv_w, rows(ffn_conv_b), ffn_w_down,
                  rows(ffn_b_down), final_norm_g.reshape(1, d),
                  tm=FFN_ROWS, sub=FFN_PASS_ROWS, seq=seq, cn=FFN_CHUNK, final_norm=(l == depth - 1))
    return xf.reshape(batch, seq, d)
```

```python
import functools
import math

import numpy as np
import jax
import jax.numpy as jnp
from jax import lax
from jax.experimental import pallas as pl
from jax.experimental.pallas import tpu as pltpu

F32 = jnp.float32
BF16 = jnp.bfloat16

HEAD_DIM = 64
A_HEADS = 6
A_WIDTH = A_HEADS * HEAD_DIM
GMLP_CHUNK = 128
POOL_WINDOWS = (2, 4, 8, 16)
B_GROUP_DIM = 64
B_WIDTH = len(POOL_WINDOWS) * B_GROUP_DIM
DILATED_CONFIGS = ((128, 1), (512, 4), (2048, 16))
C_GROUPS = len(DILATED_CONFIGS)
C_GROUP_WIDTH = 2 * HEAD_DIM
C_WIDTH = C_GROUPS * C_GROUP_WIDTH
RADIUS = 64
N_BUCKETS = 32
MAX_DISTANCE = 1024
X_HEADS = 4
EPS = 1e-6
NEG_INF = -1e30

LANES = 128
BF16_SUBLANES = 16
POOL_HALO = 8
FFN_HALO = 8
VMEM_LIMIT = 56 * 1024 * 1024

MIX_IN_ROWS = 1024
MIX_OUT_ROWS = 1024
MIX_OUT_PASS_ROWS = 256
FFN_ROWS = 512
FFN_PASS_ROWS = 256
FFN_CHUNK = 256


def _cparams(n_axes, semantics="parallel"):
    return pltpu.CompilerParams(dimension_semantics=(semantics,) * n_axes, vmem_limit_bytes=VMEM_LIMIT)


def _layer_block(stacked, layer):
    return pl.BlockSpec((None,) + stacked.shape[1:], lambda *_: (layer,) + (0,) * (stacked.ndim - 1))


def _rms(x, g):
    ms = jnp.mean(x * x, axis=-1, keepdims=True)
    return x * lax.rsqrt(ms + EPS) * g


def _gelu_tanh(x):
    return x * (0.5 * (1.0 + jnp.tanh(math.sqrt(2.0 / math.pi) * (x + 0.044715 * (x * x * x)))))


def _dot(a, b):
    return jnp.dot(a, b, preferred_element_type=F32)


def _cast_once(src_ref, dst_ref):
    @pl.when(pl.program_id(0) == 0)
    def _():
        dst_ref[...] = src_ref[...].astype(BF16)


def _dot_nt(a, b):
    return lax.dot_general(a, b, (((1,), (1,)), ((), ())), preferred_element_type=F32)


def _full(shape):
    return pl.BlockSpec(shape, lambda *_: (0,) * len(shape))


def _pool_minus_identity(z_prev, z_main, z_next, zx_ref, ts_ref, t, tiles, tm, seq):
    hl = POOL_HALO
    pad = hl
    ext = tm + 2 * hl
    assert POOL_WINDOWS == tuple(2 ** (k + 1) for k in range(len(POOL_WINDOWS))) and POOL_WINDOWS[-1] // 2 <= pad
    zx_ref[0:pad] = jnp.zeros((pad, B_WIDTH), F32)
    zx_ref[pad:pad + hl] = jnp.where(t > 0, z_prev, 0.0)
    zx_ref[pad + hl:pad + hl + tm] = z_main
    zx_ref[pad + hl + tm:] = jnp.where(t < tiles - 1, z_next, 0.0)
    ts_ref[:, 0:pad] = jnp.zeros((len(POOL_WINDOWS), pad, B_WIDTH), F32)

    pos = t * tm + lax.broadcasted_iota(jnp.int32, (tm, LANES), 0)
    first_group = lax.broadcasted_iota(jnp.int32, (tm, LANES), 1) < B_GROUP_DIM
    pooled = []
    for cb in range(B_WIDTH // LANES):
        cols = slice(cb * LANES, (cb + 1) * LANES)
        w_a, w_b = POOL_WINDOWS[2 * cb], POOL_WINDOWS[2 * cb + 1]
        centred = {}
        for k, w in enumerate(POOL_WINDOWS):
            if w > w_b:
                break
            src = zx_ref if k == 0 else ts_ref.at[k - 1]
            shift = w // 2
            ts_ref[k, pad:pad + ext, cols] = src[pad:pad + ext, cols] + src[pad - shift:pad - shift + ext, cols]
            if w in (w_a, w_b):
                start = pad + hl + w // 2 - 1
                centred[w] = ts_ref[k, start:start + tm, cols]
        half = jnp.where(first_group, w_a // 2, w_b // 2)
        cnt = (jnp.minimum(pos + half, seq) - jnp.maximum(pos - half, 0)).astype(F32)
        wsum = jnp.where(first_group, centred[w_a], centred[w_b])
        pooled.append(wsum / cnt - zx_ref[pad + hl:pad + hl + tm, cols])
    return jnp.concatenate(pooled, axis=-1)


def _mix_in_kernel(xp_ref, x_ref, xn_ref, g_ref, w_ref, b_ref, vg_ref, ws_ref, bs_ref, ones_ref,
                   wp_ref, bp_ref, ps_ref,
                   ya_ref, yb_ref, g0_ref, g1_ref, g2_ref, qkv_ref, wb_ref, zx_ref, ts_ref, *, tm, seq):
    _cast_once(w_ref, wb_ref)
    tiles = seq // tm
    t = pl.program_id(0) % tiles
    g = g_ref[...]
    hb = _rms(x_ref[...], g).astype(BF16)

    def seg(a, b):
        return _dot(hb, wb_ref[:, a:b]) + b_ref[:, a:b]

    off_b = 2 * A_WIDTH
    off_c = off_b + B_WIDTH
    uv = seg(0, off_b)
    hl = POOL_HALO
    h_halo = jnp.concatenate([_rms(xp_ref[...], g), _rms(xn_ref[...], g)], axis=0).astype(BF16)
    z_halo = _dot(h_halo, wb_ref[:, off_b:off_c]) + b_ref[:, off_b:off_c]
    z_main = seg(off_b, off_c)
    qk = seg(off_c, off_c + 2 * C_WIDTH)
    parts = (qk[:, :C_WIDTH] * (HEAD_DIM ** -0.5), qk[:, C_WIDTH:], seg(off_c + 2 * C_WIDTH, off_c + 3 * C_WIDTH))

    pooled = _pool_minus_identity(z_halo[0:hl], z_main, z_halo[hl:], zx_ref, ts_ref, t, tiles, tm, seq).astype(BF16)
    yb_ref[...] = ((_dot(pooled, wp_ref[...]) + bp_ref[...]) * ps_ref[...]).astype(BF16)

    for grp, (out_ref, (_, dil)) in enumerate(zip((g0_ref, g1_ref, g2_ref), DILATED_CONFIGS)):
        for part in range(3):
            tile = parts[part][:, grp * C_GROUP_WIDTH:(grp + 1) * C_GROUP_WIDTH]
            if dil == 1:
                out_ref[part] = tile.astype(BF16)
                continue
            qkv_ref[part, grp] = tile
            for r in range(dil):
                piece = qkv_ref[part, grp, pl.ds(r, tm // dil, stride=dil), :]
                out_ref[part, :, r * C_GROUP_WIDTH:(r + 1) * C_GROUP_WIDTH] = piece.astype(BF16)

    uv = _gelu_tanh(uv)
    u, v = uv[:, :A_WIDTH], uv[:, A_WIDTH:]
    vsq = (v * v).astype(BF16)
    cut = 2 * LANES
    ssum = jnp.concatenate([_dot(vsq[:, :cut], ones_ref[:cut, :cut]), _dot(vsq[:, cut:], ones_ref[cut:, cut:])],
                           axis=1)
    vn = v * lax.rsqrt(ssum * (1.0 / HEAD_DIM) + EPS) * vg_ref[...]
    first_head = lax.broadcasted_iota(jnp.int32, (GMLP_CHUNK, LANES), 1) < HEAD_DIM
    chunk_rows = [slice(c * GMLP_CHUNK, (c + 1) * GMLP_CHUNK) for c in range(tm // GMLP_CHUNK)]
    for j in range(A_WIDTH // LANES):
        cols = slice(j * LANES, (j + 1) * LANES)
        rhs = jnp.concatenate(
            [jnp.concatenate([jnp.where(first_head, vn[rows, cols], 0.0).astype(BF16),
                              jnp.where(first_head, 0.0, vn[rows, cols]).astype(BF16)], axis=0)
             for rows in chunk_rows], axis=1)
        s = _dot(ws_ref[j], rhs)
        for c, rows in enumerate(chunk_rows):
            s_c = s[:, c * LANES:(c + 1) * LANES] + bs_ref[:, cols]
            ya_ref[rows, cols] = (u[rows, cols] * s_c).astype(BF16)


def _mix_in(x, g, w_in, layer, b_in, v_gain, ws_cat, bs_full, ones_bd, wp_bd, bp, ps, *, tm, seq):
    t, d = x.shape
    in_width = w_in.shape[2]
    hl = POOL_HALO
    blk_per_tile = tm // hl
    n_hl = t // hl
    qkv_shapes = [jax.ShapeDtypeStruct((3, t // dil, dil * C_GROUP_WIDTH), BF16) for _, dil in DILATED_CONFIGS]
    qkv_specs = [pl.BlockSpec((3, tm // dil, dil * C_GROUP_WIDTH), lambda i: (0, i, 0))
                 for _, dil in DILATED_CONFIGS]
    return pl.pallas_call(
        functools.partial(_mix_in_kernel, tm=tm, seq=seq),
        grid=(t // tm,),
        in_specs=[pl.BlockSpec((hl, d), lambda i: (jnp.maximum(i * blk_per_tile - 1, 0), 0)),
                  pl.BlockSpec((tm, d), lambda i: (i, 0)),
                  pl.BlockSpec((hl, d), lambda i: (jnp.minimum((i + 1) * blk_per_tile, n_hl - 1), 0)),
                  _layer_block(g, layer), _layer_block(w_in, layer), _layer_block(b_in, layer),
                  _layer_block(v_gain, layer), _layer_block(ws_cat, layer), _layer_block(bs_full, layer),
                  _full(ones_bd.shape), _layer_block(wp_bd, layer), _layer_block(bp, layer),
                  _layer_block(ps, layer)],
        out_specs=[pl.BlockSpec((tm, A_WIDTH), lambda i: (i, 0)),
                   pl.BlockSpec((tm, B_WIDTH), lambda i: (i, 0))] + qkv_specs,
        out_shape=[jax.ShapeDtypeStruct((t, A_WIDTH), BF16),
                   jax.ShapeDtypeStruct((t, B_WIDTH), BF16)] + qkv_shapes,
        scratch_shapes=[pltpu.VMEM((3, C_GROUPS, tm, C_GROUP_WIDTH), F32), pltpu.VMEM((d, in_width), BF16),
                        pltpu.VMEM((tm + 3 * hl, B_WIDTH), F32),
                        pltpu.VMEM((len(POOL_WINDOWS), tm + 3 * hl, B_WIDTH), F32)],
        compiler_params=_cparams(1, "arbitrary"),
        name="mix_in",
    )(x, x, x, g, w_in, b_in, v_gain, ws_cat, bs_full, ones_bd, wp_bd, bp, ps)


def _attn_kernel(g0_ref, g1_ref, g2_ref, bias_ref, yc_ref, kx0_ref, vx0_ref, kx1_ref, vx1_ref, kx2_ref, vx2_ref,
                 nat_ref, *, seq):
    qc = 2 * RADIUS
    kc = qc + 2 * RADIUS
    first_head = lax.broadcasted_iota(jnp.int32, (qc, LANES), 1) < HEAD_DIM

    def per_lane_head(a):
        a = jnp.broadcast_to(a, (2 * qc, LANES))
        return jnp.where(first_head, a[:qc], a[qc:])

    groups = zip((g0_ref, g1_ref, g2_ref), (kx0_ref, kx1_ref, kx2_ref), (vx0_ref, vx1_ref, vx2_ref), DILATED_CONFIGS)
    for grp, (qkv_ref, kx_ref, vx_ref, (_, dil)) in enumerate(groups):
        sub_len = seq // dil
        n_chunks = sub_len // qc
        halo = jnp.zeros((RADIUS, dil * LANES), BF16)
        for x_ref, part in ((kx_ref, 1), (vx_ref, 2)):
            x_ref[0:RADIUS] = halo
            x_ref[RADIUS:RADIUS + sub_len] = qkv_ref[part]
            x_ref[RADIUS + sub_len:] = halo
        for c in range(n_chunks):
            rows = slice(c * qc, (c + 1) * qc)
            variant = 0 if c == 0 else (2 if c == n_chunks - 1 else 1)
            for r in range(dil):
                lanes = slice(r * LANES, (r + 1) * LANES)
                q = qkv_ref[0, rows, lanes]
                keys = kx_ref[c * qc:c * qc + kc, lanes]
                vals = vx_ref[c * qc:c * qc + kc, lanes]
                zero = jnp.zeros_like(q)
                q2 = jnp.concatenate([jnp.where(first_head, q, zero), jnp.where(first_head, zero, q)], axis=0)
                s = _dot_nt(q2, keys) + bias_ref[grp, variant]
                m = jnp.max(s, axis=-1, keepdims=True)
                p = jnp.exp(s - m).astype(BF16)
                o2 = _dot(p, jnp.concatenate([vals, jnp.ones_like(vals)], axis=1))
                l_both = per_lane_head(o2[:, LANES:])
                natural = slice(c * qc, (c + 1) * qc) if dil == 1 else pl.ds(c * qc * dil + r, qc, stride=dil)
                nat_ref[2 * grp, natural, :] = per_lane_head(o2[:, :LANES]) / l_both
                nat_ref[2 * grp + 1, natural, :] = per_lane_head(m) + jnp.log(l_both)

    l0, l1, l2 = nat_ref[1], nat_ref[3], nat_ref[5]
    m = jnp.maximum(l0, jnp.maximum(l1, l2))
    e0, e1, e2 = jnp.exp(l0 - m), jnp.exp(l1 - m), jnp.exp(l2 - m)
    yc_ref[...] = ((e0 * nat_ref[0] + e1 * nat_ref[2] + e2 * nat_ref[4]) / (e0 + e1 + e2)).astype(BF16)


def _attn(qkv, biases, *, batch, seq):
    assert C_GROUPS == 3 and all(seq // dil >= 4 * RADIUS for _, dil in DILATED_CONFIGS)
    views, specs, scratch = [], [], []
    for arr, (_, dil) in zip(qkv, DILATED_CONFIGS):
        sub_len, width = seq // dil, dil * C_GROUP_WIDTH
        views.append(arr.reshape(3, batch, sub_len, width))
        specs.append(pl.BlockSpec((3, None, sub_len, width), lambda b: (0, b, 0, 0)))
        scratch += [pltpu.VMEM((sub_len + 2 * RADIUS, width), BF16)] * 2
    yc = pl.pallas_call(
        functools.partial(_attn_kernel, seq=seq),
        grid=(batch,),
        in_specs=specs + [_full(biases.shape)],
        out_specs=pl.BlockSpec((None, seq, C_GROUP_WIDTH), lambda b: (b, 0, 0)),
        out_shape=jax.ShapeDtypeStruct((batch, seq, C_GROUP_WIDTH), BF16),
        scratch_shapes=scratch + [pltpu.VMEM((2 * C_GROUPS, seq, C_GROUP_WIDTH), F32)],
        compiler_params=_cparams(1),
        name="dilated_attn",
    )(*views, biases)
    return yc.reshape(batch * seq, C_GROUP_WIDTH)


def _t5_bucket(rel):
    nb = N_BUCKETS // 2
    ret = (rel > 0).astype(np.int32) * nb
    n = np.abs(rel)
    max_exact = nb // 2
    large = max_exact + (np.log(np.maximum(n, 1) / max_exact)
                         / math.log(MAX_DISTANCE / max_exact) * (nb - max_exact)).astype(np.int32)
    large = np.minimum(large, nb - 1)
    return ret + np.where(n < max_exact, n, large)


def _attn_biases(rel_table):
    qc, kc = 2 * RADIUS, 4 * RADIUS
    n = qc + kc - 1
    delta = (np.arange(n) + qc - 1) % n - (qc + RADIUS - 1)
    dils = np.array([dil for _, dil in DILATED_CONFIGS])
    buckets = _t5_bucket(delta[None, :] * dils[:, None])
    onehot = (buckets[..., None] == np.arange(N_BUCKETS)).astype(np.float32)
    table = rel_table.reshape(N_BUCKETS, C_GROUPS, 2)
    per_delta = jnp.einsum('gnb,bgh->ghn', onehot, table, precision=lax.Precision.HIGHEST)
    per_delta = jnp.where(np.abs(delta) <= RADIUS, per_delta, NEG_INF)
    reps = -(-(qc * (n - 1)) // n)
    band = jnp.tile(per_delta, (1, 1, reps))[..., :qc * (n - 1)].reshape(C_GROUPS, 2, qc, n - 1)[..., :kc]
    key = np.arange(kc)
    variants = jnp.stack([jnp.where(key < RADIUS, NEG_INF, band), band,
                          jnp.where(key >= kc - RADIUS, NEG_INF, band)], axis=1)
    return variants.reshape(C_GROUPS, 3, 2 * qc, kc)


def _mix_out_kernel(x_ref, ya_ref, yb_ref, yc_ref, wo_ref, bo_ref, xg_ref, qk_ref, vo_ref, xbo_ref, wu_ref,
                    out_ref, wub_ref, xp_ref, wob_ref, *, tm, sub):
    _cast_once(wo_ref, wob_ref)
    wub_ref[...] = wu_ref[...].astype(BF16)

    for s in range(tm // sub):
        rows = slice(s * sub, (s + 1) * sub)
        y = (_dot(ya_ref[rows, :], wob_ref[0:A_WIDTH, :])
             + _dot(yb_ref[rows, :], wob_ref[A_WIDTH:A_WIDTH + B_WIDTH, :])
             + _dot(yc_ref[rows, :], wob_ref[A_WIDTH + B_WIDTH:, :]))
        out_ref[rows, :] = x_ref[rows, :] + y + bo_ref[...]
    out_ref[...] = _cross_attend(out_ref[...], xg_ref, qk_ref, vo_ref, xbo_ref, xp_ref)


def _mix_out(x, ya, yb, yc, w_out, b_out, xg, qk, vo, layer, xb_o, w_up, *, tm, sub, seq):
    t, d = x.shape
    tiles = seq // tm
    hm = qk.shape[-1]
    steps = t // tm
    up_width = w_up.shape[2]
    panel = d // steps
    assert d % steps == 0 and panel % BF16_SUBLANES == 0
    row = lambda w: pl.BlockSpec((tm, w), lambda i: (i, 0))
    return pl.pallas_call(
        functools.partial(_mix_out_kernel, tm=tm, sub=sub),
        grid=(steps,),
        in_specs=[row(d), row(A_WIDTH), row(B_WIDTH), row(C_GROUP_WIDTH),
                  _layer_block(w_out, layer), _layer_block(b_out, layer), _layer_block(xg, layer),
                  pl.BlockSpec((None, None, d, hm), lambda i: (layer, i // tiles, 0, 0)),
                  pl.BlockSpec((None, None, hm, d), lambda i: (layer, i // tiles, 0, 0)),
                  _layer_block(xb_o, layer),
                  pl.BlockSpec((None, panel, up_width), lambda i: (layer, i, 0))],
        out_specs=[row(d), pl.BlockSpec((panel, up_width), lambda i: (i, 0))],
        out_shape=[jax.ShapeDtypeStruct((t, d), F32), jax.ShapeDtypeStruct((d, up_width), BF16)],
        scratch_shapes=[pltpu.VMEM((tm, hm), BF16),
                        pltpu.VMEM(w_out.shape[1:], BF16)],
        compiler_params=_cparams(1, "arbitrary"),
        name="mix_out_xattn",
    )(x, ya, yb, yc, w_out, b_out, xg, qk, vo, xb_o, w_up)


def _mem_fold_kernel(mem_ref, g_ref, wkv_ref, wq_ref, wo_ref, qk_ref, vo_ref):
    d = mem_ref.shape[1]
    m = mem_ref.shape[0]
    hd = d // X_HEADS
    mem_n = _rms(mem_ref[...], g_ref[...]).astype(BF16)
    kv = _dot(mem_n, wkv_ref[...].astype(BF16)).astype(BF16)
    for h in range(X_HEADS):
        cols = slice(h * hd, (h + 1) * hd)
        wq_h = wq_ref[:, cols].astype(BF16)
        qk_ref[:, h * m:(h + 1) * m] = (_dot_nt(wq_h, kv[:, cols]) * (hd ** -0.5)).astype(BF16)
        wo_h = wo_ref[cols, :].astype(BF16)
        vo_ref[h * m:(h + 1) * m, :] = _dot(kv[:, d + h * hd:d + (h + 1) * hd], wo_h).astype(BF16)


def _mem_fold(mem, g, w_kv, w_q, w_o):
    batch, m, d = mem.shape
    depth = w_kv.shape[0]

    def per_layer(stack):
        return pl.BlockSpec((None,) + stack.shape[1:], lambda l, b: (l, 0, 0))

    return pl.pallas_call(
        _mem_fold_kernel,
        grid=(depth, batch),
        in_specs=[pl.BlockSpec((None, m, d), lambda l, b: (b, 0, 0)), pl.BlockSpec((1, d), lambda l, b: (0, 0)),
                  per_layer(w_kv), per_layer(w_q), per_layer(w_o)],
        out_specs=[pl.BlockSpec((None, None, d, X_HEADS * m), lambda l, b: (l, b, 0, 0)),
                   pl.BlockSpec((None, None, X_HEADS * m, d), lambda l, b: (l, b, 0, 0))],
        out_shape=[jax.ShapeDtypeStruct((depth, batch, d, X_HEADS * m), BF16),
                   jax.ShapeDtypeStruct((depth, batch, X_HEADS * m, d), BF16)],
        compiler_params=_cparams(2),
        name="mem_fold",
    )(mem, g, w_kv, w_q, w_o)


def _cross_attend(x, g_ref, qk_ref, vo_ref, bo_ref, p_ref):
    hb = _rms(x, g_ref[...]).astype(BF16)
    s_all = _dot(hb, qk_ref[...])
    m_len = qk_ref.shape[1] // X_HEADS
    for h in range(X_HEADS):
        cols = slice(h * m_len, (h + 1) * m_len)
        s = s_all[:, cols]
        m = jnp.max(s, axis=-1, keepdims=True)
        p = jnp.exp(s - m)
        l = jnp.sum(p, axis=-1, keepdims=True)
        p_ref[:, cols] = (p / l).astype(BF16)
    return x + _dot(p_ref[...], vo_ref[...]) + bo_ref[...]


def _ffn_kernel(xp_ref, x_ref, xn_ref, g_ref, wu_ref, bu_ref, cw_ref, cb_ref, wd_ref, bd_ref, fg_ref,
                out_ref, hb_ref, u0_ref, u1_ref, u2_ref, u3_ref, acc_ref, wdb_ref, *, tm, sub, seq, d_ff, cn,
                final_norm):
    _cast_once(wd_ref, wdb_ref)
    u_bufs = ((u0_ref, u1_ref), (u2_ref, u3_ref))
    tiles = seq // tm
    t = pl.program_id(0) % tiles
    hl = FFN_HALO
    g = g_ref[...]
    hb_ref[...] = jnp.concatenate([_rms(xp_ref[...], g), _rms(x_ref[...], g), _rms(xn_ref[...], g)],
                                  axis=0).astype(BF16)
    n_sub = tm // sub
    rows = sub + 2 * hl
    n_chunks = d_ff // cn

    def conv(u_ref, ue, cols, a):
        bias = bu_ref[:, cols]
        c0, c1, c2 = cw_ref[0:1, cols], cw_ref[1:2, cols], cw_ref[2:3, cols]
        u_ref[0, 1:1 + rows, :] = ue
        u_ref[1, hl - 1:hl - 1 + rows, :] = ue
        if a == 0:
            u_ref[0, hl:hl + 1, :] = jnp.where(t > 0, ue[hl - 1:hl, :], -bias)
        if a == n_sub - 1:
            u_ref[1, 2 * hl + sub - 1:2 * hl + sub, :] = jnp.where(t < tiles - 1, ue[hl + sub:hl + sub + 1, :], -bias)
        return (c0 * u_ref[0, hl:hl + sub, :] + c1 * ue[hl:hl + sub, :] + c2 * u_ref[1, 2 * hl:2 * hl + sub, :]
                + (bias * (c0 + c1 + c2) + cb_ref[:, cols]))

    def up(a, j):
        hb = hb_ref[a * sub:a * sub + rows, :]
        return (_dot(hb, wu_ref[:, j * cn:(j + 1) * cn]), _dot(hb, wu_ref[:, d_ff + j * cn:d_ff + (j + 1) * cn]))

    def down(a, j, act):
        part = _dot(act, wdb_ref[j * cn:(j + 1) * cn, :])
        if j == 0:
            acc_ref[...] = part
        else:
            acc_ref[...] += part
        if j == n_chunks - 1:
            out_rows = slice(a * sub, (a + 1) * sub)
            y = x_ref[out_rows, :] + acc_ref[...] + bd_ref[...]
            if final_norm:
                y = _rms(y, fg_ref[...])
            out_ref[out_rows, :] = y

    items = [(a, j) for a in range(n_sub) for j in range(n_chunks)]
    ue = up(*items[0])
    act_prev = None
    for idx, (a, j) in enumerate(items):
        ue_next = up(*items[idx + 1]) if idx + 1 < len(items) else None
        if act_prev is not None:
            down(*items[idx - 1], act_prev)
        gate = conv(u_bufs[idx % 2][0], ue[0], slice(j * cn, (j + 1) * cn), a)
        val = conv(u_bufs[idx % 2][1], ue[1], slice(d_ff + j * cn, d_ff + (j + 1) * cn), a)
        act_prev = (gate * jax.nn.sigmoid(gate) * val).astype(BF16)
        ue = ue_next
    down(*items[-1], act_prev)


def _ffn(x, g, w_up, layer, b_up, conv_w, conv_b, w_down, b_down, final_g, *, tm, sub, seq, cn, final_norm):
    t, d = x.shape
    d_ff = w_down.shape[1]
    hl = FFN_HALO
    blk_per_tile = tm // hl
    n_hl = t // hl
    row = pl.BlockSpec((tm, d), lambda i: (i, 0))
    return pl.pallas_call(
        functools.partial(_ffn_kernel, tm=tm, sub=sub, seq=seq, d_ff=d_ff, cn=cn, final_norm=final_norm),
        grid=(t // tm,),
        in_specs=[pl.BlockSpec((hl, d), lambda i: (jnp.maximum(i * blk_per_tile - 1, 0), 0)),
                  row,
                  pl.BlockSpec((hl, d), lambda i: (jnp.minimum((i + 1) * blk_per_tile, n_hl - 1), 0)),
                  _layer_block(g, layer), _full(w_up.shape), _layer_block(b_up, layer),
                  _layer_block(conv_w, layer), _layer_block(conv_b, layer), _layer_block(w_down, layer),
                  _layer_block(b_down, layer), _full((1, d))],
        out_specs=row,
        out_shape=jax.ShapeDtypeStruct((t, d), F32),
        scratch_shapes=([pltpu.VMEM((tm + 2 * hl, d), BF16)] + [pltpu.VMEM((2, sub + 3 * hl, cn), F32)] * 4
                        + [pltpu.VMEM((sub, d), F32), pltpu.VMEM((d_ff, d), BF16)]),
        compiler_params=_cparams(1, "arbitrary"),
        name="conv_ffn",
    )(x, x, x, g, w_up, b_up, conv_w, conv_b, w_down, b_down, final_g)


def _block_diag(blocks):
    *lead, n, a, b = blocks.shape
    eye = jnp.eye(n, dtype=blocks.dtype)
    return (eye[:, None, :, None] * blocks[..., :, :, None, :]).reshape(*lead, n * a, n * b)


def kernel(x, mem, rel_table, mem_norm_g, norm_mix_g, w_in, b_in, gmlp_v_g, gmlp_w_s, gmlp_b_s, pool_w, pool_b, pool_scale, w_out, b_out, norm_mem_g, xattn_w_q, xattn_w_kv, xattn_w_o, xattn_b_o, norm_ffn_g, ffn_w_up, ffn_b_up, ffn_conv_w, ffn_conv_b, ffn_w_down, ffn_b_down, final_norm_g):
    batch, seq, d = x.shape
    mem_len = mem.shape[1]
    depth = w_in.shape[0]
    t = batch * seq
    assert all(w // (2 * dil) == RADIUS for w, dil in DILATED_CONFIGS)
    assert x.dtype == F32 and all(seq % rows == 0 for rows in (MIX_IN_ROWS, MIX_OUT_ROWS, FFN_ROWS))
    assert FFN_ROWS % FFN_PASS_ROWS == 0 and ffn_w_down.shape[1] % FFN_CHUNK == 0

    rows = lambda a: a.reshape(depth, 1, -1)
    ones_bd = jnp.asarray(np.kron(np.eye(A_HEADS), np.ones((HEAD_DIM, HEAD_DIM))), BF16)
    biases = _attn_biases(rel_table)
    qk, vo = _mem_fold(mem, mem_norm_g.reshape(1, d), xattn_w_kv, xattn_w_q, xattn_w_o)
    ws_cat = (gmlp_w_s.reshape(depth, A_HEADS // 2, 2, GMLP_CHUNK, GMLP_CHUNK).transpose(0, 1, 3, 2, 4)
              .reshape(depth, A_HEADS // 2, GMLP_CHUNK, 2 * GMLP_CHUNK).astype(BF16))
    bs_full = jnp.repeat(jnp.swapaxes(gmlp_b_s, 1, 2), HEAD_DIM, axis=2)
    wp_bd = _block_diag(pool_w).astype(BF16)

    xf = x.reshape(t, d)
    for l in range(depth):
        ya, yb, *qkv = _mix_in(xf, rows(norm_mix_g), w_in, l, rows(b_in), rows(gmlp_v_g), ws_cat, bs_full, ones_bd,
                               wp_bd, rows(pool_b), rows(pool_scale), tm=MIX_IN_ROWS, seq=seq)
        yc = _attn(qkv, biases, batch=batch, seq=seq)
        xf, w_up = _mix_out(xf, ya, yb, yc, w_out, rows(b_out), rows(norm_mem_g), qk, vo, l, rows(xattn_b_o),
                            ffn_w_up, tm=MIX_OUT_ROWS, sub=MIX_OUT_PASS_ROWS, seq=seq)
        xf = _ffn(xf, rows(norm_ffn_g), w_up, l, rows(ffn_b_up), ffn_conv_w, rows(ffn_conv_b), ffn_w_down,
                  rows(ffn_b_down), final_norm_g.reshape(1, d),
                  tm=FFN_ROWS, sub=FFN_PASS_ROWS, seq=seq, cn=FFN_CHUNK, final_norm=(l == depth - 1))
    return xf.reshape(batch, seq, d)
```

```python
import functools
import math

import numpy as np
import jax
import jax.numpy as jnp
from jax import lax
from jax.experimental import pallas as pl
from jax.experimental.pallas import tpu as pltpu

F32 = jnp.float32
BF16 = jnp.bfloat16

HEAD_DIM = 64
A_HEADS = 6
A_WIDTH = A_HEADS * HEAD_DIM
GMLP_CHUNK = 128
POOL_WINDOWS = (2, 4, 8, 16)
B_GROUP_DIM = 64
B_WIDTH = len(POOL_WINDOWS) * B_GROUP_DIM
DILATED_CONFIGS = ((128, 1), (512, 4), (2048, 16))
C_GROUPS = len(DILATED_CONFIGS)
C_GROUP_WIDTH = 2 * HEAD_DIM
C_WIDTH = C_GROUPS * C_GROUP_WIDTH
RADIUS = 64
N_BUCKETS = 32
MAX_DISTANCE = 1024
X_HEADS = 4
EPS = 1e-6
NEG_INF = -1e30

LANES = 128
BF16_SUBLANES = 16
POOL_HALO = 8
FFN_HALO = 8
VMEM_LIMIT = 56 * 1024 * 1024

MIX_IN_ROWS = 1024
MIX_OUT_ROWS = 1024
MIX_OUT_PASS_ROWS = 256
FFN_ROWS = 512
FFN_PASS_ROWS = 256
FFN_CHUNK = 256


def _cparams(n_axes, semantics="parallel"):
    return pltpu.CompilerParams(dimension_semantics=(semantics,) * n_axes, vmem_limit_bytes=VMEM_LIMIT)


def _layer_block(stacked, layer):
    return pl.BlockSpec((None,) + stacked.shape[1:], lambda *_: (layer,) + (0,) * (stacked.ndim - 1))


def _layer_rows(layer, *refs):
    return [ref.at[pl.ds(layer, 1)] for ref in refs]


def _rms(x, g):
    ms = jnp.mean(x * x, axis=-1, keepdims=True)
    return x * lax.rsqrt(ms + EPS) * g


def _gelu_tanh(x):
    return x * (0.5 * (1.0 + jnp.tanh(math.sqrt(2.0 / math.pi) * (x + 0.044715 * (x * x * x)))))


def _dot(a, b):
    return jnp.dot(a, b, preferred_element_type=F32)


def _cast_once(src_ref, dst_ref):
    @pl.when(pl.program_id(0) == 0)
    def _():
        dst_ref[...] = src_ref[...].astype(BF16)


def _dot_nt(a, b):
    return lax.dot_general(a, b, (((1,), (1,)), ((), ())), preferred_element_type=F32)


def _full(shape):
    return pl.BlockSpec(shape, lambda *_: (0,) * len(shape))


def _pool_minus_identity(z_prev, z_main, z_next, zx_ref, ts_ref, t, tiles, tm, seq):
    hl = POOL_HALO
    pad = hl
    ext = tm + 2 * hl
    assert POOL_WINDOWS == tuple(2 ** (k + 1) for k in range(len(POOL_WINDOWS))) and POOL_WINDOWS[-1] // 2 <= pad
    zx_ref[0:pad] = jnp.zeros((pad, B_WIDTH), F32)
    zx_ref[pad:pad + hl] = jnp.where(t > 0, z_prev, 0.0)
    zx_ref[pad + hl:pad + hl + tm] = z_main
    zx_ref[pad + hl + tm:] = jnp.where(t < tiles - 1, z_next, 0.0)
    ts_ref[:, 0:pad] = jnp.zeros((len(POOL_WINDOWS), pad, B_WIDTH), F32)

    pos = t * tm + lax.broadcasted_iota(jnp.int32, (tm, LANES), 0)
    first_group = lax.broadcasted_iota(jnp.int32, (tm, LANES), 1) < B_GROUP_DIM
    pooled = []
    for cb in range(B_WIDTH // LANES):
        cols = slice(cb * LANES, (cb + 1) * LANES)
        w_a, w_b = POOL_WINDOWS[2 * cb], POOL_WINDOWS[2 * cb + 1]
        centred = {}
        for k, w in enumerate(POOL_WINDOWS):
            if w > w_b:
                break
            src = zx_ref if k == 0 else ts_ref.at[k - 1]
            shift = w // 2
            ts_ref[k, pad:pad + ext, cols] = src[pad:pad + ext, cols] + src[pad - shift:pad - shift + ext, cols]
            if w in (w_a, w_b):
                start = pad + hl + w // 2 - 1
                centred[w] = ts_ref[k, start:start + tm, cols]
        half = jnp.where(first_group, w_a // 2, w_b // 2)
        cnt = (jnp.minimum(pos + half, seq) - jnp.maximum(pos - half, 0)).astype(F32)
        wsum = jnp.where(first_group, centred[w_a], centred[w_b])
        pooled.append(wsum / cnt - zx_ref[pad + hl:pad + hl + tm, cols])
    return jnp.concatenate(pooled, axis=-1)


def _mix_in_kernel(xp_ref, x_ref, xn_ref, g_ref, w_ref, b_ref, vg_ref, ws_ref, bs_ref, ones_ref,
                   wp_ref, bp_ref, ps_ref,
                   ya_ref, yb_ref, g0_ref, g1_ref, g2_ref, qkv_ref, wb_ref, zx_ref, ts_ref, *, layer, tm, seq):
    g_ref, b_ref, vg_ref, bp_ref, ps_ref = _layer_rows(layer, g_ref, b_ref, vg_ref, bp_ref, ps_ref)
    _cast_once(w_ref, wb_ref)
    tiles = seq // tm
    t = pl.program_id(0) % tiles
    g = g_ref[...]
    hb = _rms(x_ref[...], g).astype(BF16)

    def seg(a, b):
        return _dot(hb, wb_ref[:, a:b]) + b_ref[:, a:b]

    off_b = 2 * A_WIDTH
    off_c = off_b + B_WIDTH
    uv = seg(0, off_b)
    hl = POOL_HALO
    h_halo = jnp.concatenate([_rms(xp_ref[...], g), _rms(xn_ref[...], g)], axis=0).astype(BF16)
    z_halo = _dot(h_halo, wb_ref[:, off_b:off_c]) + b_ref[:, off_b:off_c]
    z_main = seg(off_b, off_c)
    qk = seg(off_c, off_c + 2 * C_WIDTH)
    parts = (qk[:, :C_WIDTH] * (HEAD_DIM ** -0.5), qk[:, C_WIDTH:], seg(off_c + 2 * C_WIDTH, off_c + 3 * C_WIDTH))

    pooled = _pool_minus_identity(z_halo[0:hl], z_main, z_halo[hl:], zx_ref, ts_ref, t, tiles, tm, seq).astype(BF16)
    yb_ref[...] = ((_dot(pooled, wp_ref[...]) + bp_ref[...]) * ps_ref[...]).astype(BF16)

    for grp, (out_ref, (_, dil)) in enumerate(zip((g0_ref, g1_ref, g2_ref), DILATED_CONFIGS)):
        for part in range(3):
            tile = parts[part][:, grp * C_GROUP_WIDTH:(grp + 1) * C_GROUP_WIDTH]
            if dil == 1:
                out_ref[part] = tile.astype(BF16)
                continue
            qkv_ref[part, grp] = tile
            for r in range(dil):
                piece = qkv_ref[part, grp, pl.ds(r, tm // dil, stride=dil), :]
                out_ref[part, :, r * C_GROUP_WIDTH:(r + 1) * C_GROUP_WIDTH] = piece.astype(BF16)

    uv = _gelu_tanh(uv)
    u, v = uv[:, :A_WIDTH], uv[:, A_WIDTH:]
    vsq = (v * v).astype(BF16)
    cut = 2 * LANES
    ssum = jnp.concatenate([_dot(vsq[:, :cut], ones_ref[:cut, :cut]), _dot(vsq[:, cut:], ones_ref[cut:, cut:])],
                           axis=1)
    vn = v * lax.rsqrt(ssum * (1.0 / HEAD_DIM) + EPS) * vg_ref[...]
    first_head = lax.broadcasted_iota(jnp.int32, (GMLP_CHUNK, LANES), 1) < HEAD_DIM
    chunk_rows = [slice(c * GMLP_CHUNK, (c + 1) * GMLP_CHUNK) for c in range(tm // GMLP_CHUNK)]
    for j in range(A_WIDTH // LANES):
        cols = slice(j * LANES, (j + 1) * LANES)
        rhs = jnp.concatenate(
            [jnp.concatenate([jnp.where(first_head, vn[rows, cols], 0.0).astype(BF16),
                              jnp.where(first_head, 0.0, vn[rows, cols]).astype(BF16)], axis=0)
             for rows in chunk_rows], axis=1)
        s = _dot(ws_ref[j], rhs)
        for c, rows in enumerate(chunk_rows):
            s_c = s[:, c * LANES:(c + 1) * LANES] + bs_ref[:, cols]
            ya_ref[rows, cols] = (u[rows, cols] * s_c).astype(BF16)


def _mix_in(x, g, w_in, layer, b_in, v_gain, ws_cat, bs_full, ones_bd, wp_bd, bp, ps, *, tm, seq):
    t, d = x.shape
    in_width = w_in.shape[2]
    hl = POOL_HALO
    blk_per_tile = tm // hl
    n_hl = t // hl
    qkv_shapes = [jax.ShapeDtypeStruct((3, t // dil, dil * C_GROUP_WIDTH), BF16) for _, dil in DILATED_CONFIGS]
    qkv_specs = [pl.BlockSpec((3, tm // dil, dil * C_GROUP_WIDTH), lambda i: (0, i, 0))
                 for _, dil in DILATED_CONFIGS]
    return pl.pallas_call(
        functools.partial(_mix_in_kernel, layer=layer, tm=tm, seq=seq),
        grid=(t // tm,),
        in_specs=[pl.BlockSpec((hl, d), lambda i: (jnp.maximum(i * blk_per_tile - 1, 0), 0)),
                  pl.BlockSpec((tm, d), lambda i: (i, 0)),
                  pl.BlockSpec((hl, d), lambda i: (jnp.minimum((i + 1) * blk_per_tile, n_hl - 1), 0)),
                  _full(g.shape), _layer_block(w_in, layer), _full(b_in.shape),
                  _full(v_gain.shape), _layer_block(ws_cat, layer), _layer_block(bs_full, layer),
                  _full(ones_bd.shape), _layer_block(wp_bd, layer), _full(bp.shape),
                  _full(ps.shape)],
        out_specs=[pl.BlockSpec((tm, A_WIDTH), lambda i: (i, 0)),
                   pl.BlockSpec((tm, B_WIDTH), lambda i: (i, 0))] + qkv_specs,
        out_shape=[jax.ShapeDtypeStruct((t, A_WIDTH), BF16),
                   jax.ShapeDtypeStruct((t, B_WIDTH), BF16)] + qkv_shapes,
        scratch_shapes=[pltpu.VMEM((3, C_GROUPS, tm, C_GROUP_WIDTH), F32), pltpu.VMEM((d, in_width), BF16),
                        pltpu.VMEM((tm + 3 * hl, B_WIDTH), F32),
                        pltpu.VMEM((len(POOL_WINDOWS), tm + 3 * hl, B_WIDTH), F32)],
        compiler_params=_cparams(1, "arbitrary"),
        name="mix_in",
    )(x, x, x, g, w_in, b_in, v_gain, ws_cat, bs_full, ones_bd, wp_bd, bp, ps)


def _attn_kernel(g0_ref, g1_ref, g2_ref, bias_ref, yc_ref, kx0_ref, vx0_ref, kx1_ref, vx1_ref, kx2_ref, vx2_ref,
                 nat_ref, *, seq):
    qc = 2 * RADIUS
    kc = qc + 2 * RADIUS
    first_head = lax.broadcasted_iota(jnp.int32, (qc, LANES), 1) < HEAD_DIM

    def per_lane_head(a):
        a = jnp.broadcast_to(a, (2 * qc, LANES))
        return jnp.where(first_head, a[:qc], a[qc:])

    groups = zip((g0_ref, g1_ref, g2_ref), (kx0_ref, kx1_ref, kx2_ref), (vx0_ref, vx1_ref, vx2_ref), DILATED_CONFIGS)
    for grp, (qkv_ref, kx_ref, vx_ref, (_, dil)) in enumerate(groups):
        sub_len = seq // dil
        n_chunks = sub_len // qc
        halo = jnp.zeros((RADIUS, dil * LANES), BF16)
        for x_ref, part in ((kx_ref, 1), (vx_ref, 2)):
            x_ref[0:RADIUS] = halo
            x_ref[RADIUS:RADIUS + sub_len] = qkv_ref[part]
            x_ref[RADIUS + sub_len:] = halo
        for c in range(n_chunks):
            rows = slice(c * qc, (c + 1) * qc)
            variant = 0 if c == 0 else (2 if c == n_chunks - 1 else 1)
            for r in range(dil):
                lanes = slice(r * LANES, (r + 1) * LANES)
                q = qkv_ref[0, rows, lanes]
                keys = kx_ref[c * qc:c * qc + kc, lanes]
                vals = vx_ref[c * qc:c * qc + kc, lanes]
                zero = jnp.zeros_like(q)
                q2 = jnp.concatenate([jnp.where(first_head, q, zero), jnp.where(first_head, zero, q)], axis=0)
                s = _dot_nt(q2, keys) + bias_ref[grp, variant]
                m = jnp.max(s, axis=-1, keepdims=True)
                p = jnp.exp(s - m).astype(BF16)
                o2 = _dot(p, jnp.concatenate([vals, jnp.ones_like(vals)], axis=1))
                l_both = per_lane_head(o2[:, LANES:])
                natural = slice(c * qc, (c + 1) * qc) if dil == 1 else pl.ds(c * qc * dil + r, qc, stride=dil)
                nat_ref[2 * grp, natural, :] = per_lane_head(o2[:, :LANES]) / l_both
                nat_ref[2 * grp + 1, natural, :] = per_lane_head(m) + jnp.log(l_both)

    l0, l1, l2 = nat_ref[1], nat_ref[3], nat_ref[5]
    m = jnp.maximum(l0, jnp.maximum(l1, l2))
    e0, e1, e2 = jnp.exp(l0 - m), jnp.exp(l1 - m), jnp.exp(l2 - m)
    yc_ref[...] = ((e0 * nat_ref[0] + e1 * nat_ref[2] + e2 * nat_ref[4]) / (e0 + e1 + e2)).astype(BF16)


def _attn(qkv, biases, *, batch, seq):
    assert C_GROUPS == 3 and all(seq // dil >= 4 * RADIUS for _, dil in DILATED_CONFIGS)
    views, specs, scratch = [], [], []
    for arr, (_, dil) in zip(qkv, DILATED_CONFIGS):
        sub_len, width = seq // dil, dil * C_GROUP_WIDTH
        views.append(arr.reshape(3, batch, sub_len, width))
        specs.append(pl.BlockSpec((3, None, sub_len, width), lambda b: (0, b, 0, 0)))
        scratch += [pltpu.VMEM((sub_len + 2 * RADIUS, width), BF16)] * 2
    yc = pl.pallas_call(
        functools.partial(_attn_kernel, seq=seq),
        grid=(batch,),
        in_specs=specs + [_full(biases.shape)],
        out_specs=pl.BlockSpec((None, seq, C_GROUP_WIDTH), lambda b: (b, 0, 0)),
        out_shape=jax.ShapeDtypeStruct((batch, seq, C_GROUP_WIDTH), BF16),
        scratch_shapes=scratch + [pltpu.VMEM((2 * C_GROUPS, seq, C_GROUP_WIDTH), F32)],
        compiler_params=_cparams(1),
        name="dilated_attn",
    )(*views, biases)
    return yc.reshape(batch * seq, C_GROUP_WIDTH)


def _t5_bucket(rel):
    nb = N_BUCKETS // 2
    ret = (rel > 0).astype(np.int32) * nb
    n = np.abs(rel)
    max_exact = nb // 2
    large = max_exact + (np.log(np.maximum(n, 1) / max_exact)
                         / math.log(MAX_DISTANCE / max_exact) * (nb - max_exact)).astype(np.int32)
    large = np.minimum(large, nb - 1)
    return ret + np.where(n < max_exact, n, large)


def _attn_biases(rel_table):
    qc, kc = 2 * RADIUS, 4 * RADIUS
    n = qc + kc - 1
    delta = (np.arange(n) + qc - 1) % n - (qc + RADIUS - 1)
    dils = np.array([dil for _, dil in DILATED_CONFIGS])
    buckets = _t5_bucket(delta[None, :] * dils[:, None])
    onehot = (buckets[..., None] == np.arange(N_BUCKETS)).astype(np.float32)
    table = rel_table.reshape(N_BUCKETS, C_GROUPS, 2)
    per_delta = jnp.einsum('gnb,bgh->ghn', onehot, table, precision=lax.Precision.HIGHEST)
    per_delta = jnp.where(np.abs(delta) <= RADIUS, per_delta, NEG_INF)
    reps = -(-(qc * (n - 1)) // n)
    band = jnp.tile(per_delta, (1, 1, reps))[..., :qc * (n - 1)].reshape(C_GROUPS, 2, qc, n - 1)[..., :kc]
    key = np.arange(kc)
    variants = jnp.stack([jnp.where(key < RADIUS, NEG_INF, band), band,
                          jnp.where(key >= kc - RADIUS, NEG_INF, band)], axis=1)
    return variants.reshape(C_GROUPS, 3, 2 * qc, kc)


def _mix_out_kernel(x_ref, ya_ref, yb_ref, yc_ref, wo_ref, bo_ref, xg_ref, qk_ref, vo_ref, xbo_ref, wu_ref,
                    out_ref, wub_ref, xp_ref, wob_ref, *, layer, tm, sub):
    bo_ref, xg_ref, xbo_ref = _layer_rows(layer, bo_ref, xg_ref, xbo_ref)
    _cast_once(wo_ref, wob_ref)
    wub_ref[...] = wu_ref[...].astype(BF16)

    for s in range(tm // sub):
        rows = slice(s * sub, (s + 1) * sub)
        y = (_dot(ya_ref[rows, :], wob_ref[0:A_WIDTH, :])
             + _dot(yb_ref[rows, :], wob_ref[A_WIDTH:A_WIDTH + B_WIDTH, :])
             + _dot(yc_ref[rows, :], wob_ref[A_WIDTH + B_WIDTH:, :]))
        out_ref[rows, :] = x_ref[rows, :] + y + bo_ref[...]
    out_ref[...] = _cross_attend(out_ref[...], xg_ref, qk_ref, vo_ref, xbo_ref, xp_ref)


def _mix_out(x, ya, yb, yc, w_out, b_out, xg, qk, vo, layer, xb_o, w_up, *, tm, sub, seq):
    t, d = x.shape
    tiles = seq // tm
    hm = qk.shape[-1]
    steps = t // tm
    up_width = w_up.shape[2]
    panel = d // steps
    assert d % steps == 0 and panel % BF16_SUBLANES == 0
    row = lambda w: pl.BlockSpec((tm, w), lambda i: (i, 0))
    return pl.pallas_call(
        functools.partial(_mix_out_kernel, layer=layer, tm=tm, sub=sub),
        grid=(steps,),
        in_specs=[row(d), row(A_WIDTH), row(B_WIDTH), row(C_GROUP_WIDTH),
                  _layer_block(w_out, layer), _full(b_out.shape), _full(xg.shape),
                  pl.BlockSpec((None, None, d, hm), lambda i: (layer, i // tiles, 0, 0)),
                  pl.BlockSpec((None, None, hm, d), lambda i: (layer, i // tiles, 0, 0)),
                  _full(xb_o.shape),
                  pl.BlockSpec((None, panel, up_width), lambda i: (layer, i, 0))],
        out_specs=[row(d), pl.BlockSpec((panel, up_width), lambda i: (i, 0))],
        out_shape=[jax.ShapeDtypeStruct((t, d), F32), jax.ShapeDtypeStruct((d, up_width), BF16)],
        scratch_shapes=[pltpu.VMEM((tm, hm), BF16),
                        pltpu.VMEM(w_out.shape[1:], BF16)],
        compiler_params=_cparams(1, "arbitrary"),
        name="mix_out_xattn",
    )(x, ya, yb, yc, w_out, b_out, xg, qk, vo, xb_o, w_up)


def _mem_fold_kernel(mem_ref, g_ref, wkv_ref, wq_ref, wo_ref, qk_ref, vo_ref):
    d = mem_ref.shape[1]
    m = mem_ref.shape[0]
    hd = d // X_HEADS
    mem_n = _rms(mem_ref[...], g_ref[...]).astype(BF16)
    kv = _dot(mem_n, wkv_ref[...].astype(BF16)).astype(BF16)
    for h in range(X_HEADS):
        cols = slice(h * hd, (h + 1) * hd)
        wq_h = wq_ref[:, cols].astype(BF16)
        qk_ref[:, h * m:(h + 1) * m] = (_dot_nt(wq_h, kv[:, cols]) * (hd ** -0.5)).astype(BF16)
        wo_h = wo_ref[cols, :].astype(BF16)
        vo_ref[h * m:(h + 1) * m, :] = _dot(kv[:, d + h * hd:d + (h + 1) * hd], wo_h).astype(BF16)


def _mem_fold(mem, g, w_kv, w_q, w_o):
    batch, m, d = mem.shape
    depth = w_kv.shape[0]

    def per_layer(stack):
        return pl.BlockSpec((None,) + stack.shape[1:], lambda l, b: (l, 0, 0))

    return pl.pallas_call(
        _mem_fold_kernel,
        grid=(depth, batch),
        in_specs=[pl.BlockSpec((None, m, d), lambda l, b: (b, 0, 0)), pl.BlockSpec((1, d), lambda l, b: (0, 0)),
                  per_layer(w_kv), per_layer(w_q), per_layer(w_o)],
        out_specs=[pl.BlockSpec((None, None, d, X_HEADS * m), lambda l, b: (l, b, 0, 0)),
                   pl.BlockSpec((None, None, X_HEADS * m, d), lambda l, b: (l, b, 0, 0))],
        out_shape=[jax.ShapeDtypeStruct((depth, batch, d, X_HEADS * m), BF16),
                   jax.ShapeDtypeStruct((depth, batch, X_HEADS * m, d), BF16)],
        compiler_params=_cparams(2),
        name="mem_fold",
    )(mem, g, w_kv, w_q, w_o)


def _cross_attend(x, g_ref, qk_ref, vo_ref, bo_ref, p_ref):
    hb = _rms(x, g_ref[...]).astype(BF16)
    s_all = _dot(hb, qk_ref[...])
    m_len = qk_ref.shape[1] // X_HEADS
    for h in range(X_HEADS):
        cols = slice(h * m_len, (h + 1) * m_len)
        s = s_all[:, cols]
        m = jnp.max(s, axis=-1, keepdims=True)
        p = jnp.exp(s - m)
        l = jnp.sum(p, axis=-1, keepdims=True)
        p_ref[:, cols] = (p / l).astype(BF16)
    return x + _dot(p_ref[...], vo_ref[...]) + bo_ref[...]


def _ffn_kernel(xp_ref, x_ref, xn_ref, g_ref, wu_ref, bu_ref, cw_ref, cb_ref, wd_ref, bd_ref, fg_ref,
                out_ref, hb_ref, u0_ref, u1_ref, u2_ref, u3_ref, acc_ref, wdb_ref, *, layer, tm, sub, seq, d_ff, cn,
                final_norm):
    g_ref, bu_ref, cb_ref, bd_ref = _layer_rows(layer, g_ref, bu_ref, cb_ref, bd_ref)
    _cast_once(wd_ref, wdb_ref)
    u_bufs = ((u0_ref, u1_ref), (u2_ref, u3_ref))
    tiles = seq // tm
    t = pl.program_id(0) % tiles
    hl = FFN_HALO
    g = g_ref[...]
    hb_ref[...] = jnp.concatenate([_rms(xp_ref[...], g), _rms(x_ref[...], g), _rms(xn_ref[...], g)],
                                  axis=0).astype(BF16)
    n_sub = tm // sub
    rows = sub + 2 * hl
    n_chunks = d_ff // cn

    def conv(u_ref, ue, cols, a):
        bias = bu_ref[:, cols]
        c0, c1, c2 = cw_ref[0:1, cols], cw_ref[1:2, cols], cw_ref[2:3, cols]
        u_ref[0, 1:1 + rows, :] = ue
        u_ref[1, hl - 1:hl - 1 + rows, :] = ue
        if a == 0:
            u_ref[0, hl:hl + 1, :] = jnp.where(t > 0, ue[hl - 1:hl, :], -bias)
        if a == n_sub - 1:
            u_ref[1, 2 * hl + sub - 1:2 * hl + sub, :] = jnp.where(t < tiles - 1, ue[hl + sub:hl + sub + 1, :], -bias)
        return (c0 * u_ref[0, hl:hl + sub, :] + c1 * ue[hl:hl + sub, :] + c2 * u_ref[1, 2 * hl:2 * hl + sub, :]
                + (bias * (c0 + c1 + c2) + cb_ref[:, cols]))

    def up(a, j):
        hb = hb_ref[a * sub:a * sub + rows, :]
        return (_dot(hb, wu_ref[:, j * cn:(j + 1) * cn]), _dot(hb, wu_ref[:, d_ff + j * cn:d_ff + (j + 1) * cn]))

    def down(a, j, act):
        part = _dot(act, wdb_ref[j * cn:(j + 1) * cn, :])
        if j == 0:
            acc_ref[...] = part
        else:
            acc_ref[...] += part
        if j == n_chunks - 1:
            out_rows = slice(a * sub, (a + 1) * sub)
            y = x_ref[out_rows, :] + acc_ref[...] + bd_ref[...]
            if final_norm:
                y = _rms(y, fg_ref[...])
            out_ref[out_rows, :] = y

    items = [(a, j) for a in range(n_sub) for j in range(n_chunks)]
    ue = up(*items[0])
    act_prev = None
    for idx, (a, j) in enumerate(items):
        ue_next = up(*items[idx + 1]) if idx + 1 < len(items) else None
        if act_prev is not None:
            down(*items[idx - 1], act_prev)
        gate = conv(u_bufs[idx % 2][0], ue[0], slice(j * cn, (j + 1) * cn), a)
        val = conv(u_bufs[idx % 2][1], ue[1], slice(d_ff + j * cn, d_ff + (j + 1) * cn), a)
        act_prev = (gate * jax.nn.sigmoid(gate) * val).astype(BF16)
        ue = ue_next
    down(*items[-1], act_prev)


def _ffn(x, g, w_up, layer, b_up, conv_w, conv_b, w_down, b_down, final_g, *, tm, sub, seq, cn, final_norm):
    t, d = x.shape
    d_ff = w_down.shape[1]
    hl = FFN_HALO
    blk_per_tile = tm // hl
    n_hl = t // hl
    row = pl.BlockSpec((tm, d), lambda i: (i, 0))
    return pl.pallas_call(
        functools.partial(_ffn_kernel, layer=layer, tm=tm, sub=sub, seq=seq, d_ff=d_ff, cn=cn,
                          final_norm=final_norm),
        grid=(t // tm,),
        in_specs=[pl.BlockSpec((hl, d), lambda i: (jnp.maximum(i * blk_per_tile - 1, 0), 0)),
                  row,
                  pl.BlockSpec((hl, d), lambda i: (jnp.minimum((i + 1) * blk_per_tile, n_hl - 1), 0)),
                  _full(g.shape), _full(w_up.shape), _full(b_up.shape),
                  _layer_block(conv_w, layer), _full(conv_b.shape), _layer_block(w_down, layer),
                  _full(b_down.shape), _full((1, d))],
        out_specs=row,
        out_shape=jax.ShapeDtypeStruct((t, d), F32),
        scratch_shapes=([pltpu.VMEM((tm + 2 * hl, d), BF16)] + [pltpu.VMEM((2, sub + 3 * hl, cn), F32)] * 4
                        + [pltpu.VMEM((sub, d), F32), pltpu.VMEM((d_ff, d), BF16)]),
        compiler_params=_cparams(1, "arbitrary"),
        name="conv_ffn",
    )(x, x, x, g, w_up, b_up, conv_w, conv_b, w_down, b_down, final_g)


def _block_diag(blocks):
    *lead, n, a, b = blocks.shape
    eye = jnp.eye(n, dtype=blocks.dtype)
    return (eye[:, None, :, None] * blocks[..., :, :, None, :]).reshape(*lead, n * a, n * b)


def kernel(x, mem, rel_table, mem_norm_g, norm_mix_g, w_in, b_in, gmlp_v_g, gmlp_w_s, gmlp_b_s, pool_w, pool_b, pool_scale, w_out, b_out, norm_mem_g, xattn_w_q, xattn_w_kv, xattn_w_o, xattn_b_o, norm_ffn_g, ffn_w_up, ffn_b_up, ffn_conv_w, ffn_conv_b, ffn_w_down, ffn_b_down, final_norm_g):
    batch, seq, d = x.shape
    mem_len = mem.shape[1]
    depth = w_in.shape[0]
    t = batch * seq
    assert all(w // (2 * dil) == RADIUS for w, dil in DILATED_CONFIGS)
    assert x.dtype == F32 and all(seq % rows == 0 for rows in (MIX_IN_ROWS, MIX_OUT_ROWS, FFN_ROWS))
    assert FFN_ROWS % FFN_PASS_ROWS == 0 and ffn_w_down.shape[1] % FFN_CHUNK == 0

    rows = lambda a: a.reshape(depth, -1)
    ones_bd = jnp.asarray(np.kron(np.eye(A_HEADS), np.ones((HEAD_DIM, HEAD_DIM))), BF16)
    biases = _attn_biases(rel_table)
    qk, vo = _mem_fold(mem, mem_norm_g.reshape(1, d), xattn_w_kv, xattn_w_q, xattn_w_o)
    ws_cat = (gmlp_w_s.reshape(depth, A_HEADS // 2, 2, GMLP_CHUNK, GMLP_CHUNK).transpose(0, 1, 3, 2, 4)
              .reshape(depth, A_HEADS // 2, GMLP_CHUNK, 2 * GMLP_CHUNK).astype(BF16))
    bs_full = jnp.repeat(jnp.swapaxes(gmlp_b_s, 1, 2), HEAD_DIM, axis=2)
    wp_bd = _block_diag(pool_w).astype(BF16)

    xf = x.reshape(t, d)
    for l in range(depth):
        ya, yb, *qkv = _mix_in(xf, rows(norm_mix_g), w_in, l, rows(b_in), rows(gmlp_v_g), ws_cat, bs_full, ones_bd,
                               wp_bd, rows(pool_b), rows(pool_scale), tm=MIX_IN_ROWS, seq=seq)
        yc = _attn(qkv, biases, batch=batch, seq=seq)
        xf, w_up = _mix_out(xf, ya, yb, yc, w_out, rows(b_out), rows(norm_mem_g), qk, vo, l, rows(xattn_b_o),
                            ffn_w_up, tm=MIX_OUT_ROWS, sub=MIX_OUT_PASS_ROWS, seq=seq)
        xf = _ffn(xf, rows(norm_ffn_g), w_up, l, rows(ffn_b_up), ffn_conv_w, rows(ffn_conv_b), ffn_w_down,
                  rows(ffn_b_down), final_norm_g.reshape(1, d),
                  tm=FFN_ROWS, sub=FFN_PASS_ROWS, seq=seq, cn=FFN_CHUNK, final_norm=(l == depth - 1))
    return xf.reshape(batch, seq, d)
```

```python
import functools
import math

import numpy as np
import jax
import jax.numpy as jnp
from jax import lax
from jax.experimental import pallas as pl
from jax.experimental.pallas import tpu as pltpu

F32 = jnp.float32
BF16 = jnp.bfloat16

HEAD_DIM = 64
A_HEADS = 6
A_WIDTH = A_HEADS * HEAD_DIM
GMLP_CHUNK = 128
POOL_WINDOWS = (2, 4, 8, 16)
B_GROUP_DIM = 64
B_WIDTH = len(POOL_WINDOWS) * B_GROUP_DIM
DILATED_CONFIGS = ((128, 1), (512, 4), (2048, 16))
C_GROUPS = len(DILATED_CONFIGS)
C_GROUP_WIDTH = 2 * HEAD_DIM
C_WIDTH = C_GROUPS * C_GROUP_WIDTH
RADIUS = 64
N_BUCKETS = 32
MAX_DISTANCE = 1024
X_HEADS = 4
EPS = 1e-6
NEG_INF = -1e30

LANES = 128
BF16_SUBLANES = 16
POOL_HALO = 8
FFN_HALO = 8
VMEM_LIMIT = 56 * 1024 * 1024

MIX_IN_ROWS = 1024
MIX_OUT_ROWS = 1024
MIX_OUT_PASS_ROWS = 256
FFN_ROWS = 512
FFN_PASS_ROWS = 256
FFN_CHUNK = 256


def _cparams(n_axes, semantics="parallel"):
    return pltpu.CompilerParams(dimension_semantics=(semantics,) * n_axes, vmem_limit_bytes=VMEM_LIMIT)


def _layer_block(stacked, layer):
    return pl.BlockSpec((None,) + stacked.shape[1:], lambda *_: (layer,) + (0,) * (stacked.ndim - 1))


def _layer_rows(layer, *refs):
    return [ref.at[pl.ds(layer, 1)] for ref in refs]


def _rms(x, g):
    ms = jnp.mean(x * x, axis=-1, keepdims=True)
    return x * lax.rsqrt(ms + EPS) * g


def _gelu_tanh(x):
    return x * (0.5 * (1.0 + jnp.tanh(math.sqrt(2.0 / math.pi) * (x + 0.044715 * (x * x * x)))))


def _dot(a, b):
    return jnp.dot(a, b, preferred_element_type=F32)


def _cast_once(src_ref, dst_ref):
    @pl.when(pl.program_id(0) == 0)
    def _():
        dst_ref[...] = src_ref[...].astype(BF16)


def _dot_nt(a, b):
    return lax.dot_general(a, b, (((1,), (1,)), ((), ())), preferred_element_type=F32)


def _full(shape):
    return pl.BlockSpec(shape, lambda *_: (0,) * len(shape))


def _pool_minus_identity(z_prev, z_main, z_next, zx_ref, ts_ref, t, tiles, tm, seq):
    hl = POOL_HALO
    pad = hl
    ext = tm + 2 * hl
    assert POOL_WINDOWS == tuple(2 ** (k + 1) for k in range(len(POOL_WINDOWS))) and POOL_WINDOWS[-1] // 2 <= pad
    zx_ref[0:pad] = jnp.zeros((pad, B_WIDTH), F32)
    zx_ref[pad:pad + hl] = jnp.where(t > 0, z_prev, 0.0)
    zx_ref[pad + hl:pad + hl + tm] = z_main
    zx_ref[pad + hl + tm:] = jnp.where(t < tiles - 1, z_next, 0.0)
    ts_ref[:, 0:pad] = jnp.zeros((len(POOL_WINDOWS), pad, B_WIDTH), F32)

    pos = t * tm + lax.broadcasted_iota(jnp.int32, (tm, LANES), 0)
    first_group = lax.broadcasted_iota(jnp.int32, (tm, LANES), 1) < B_GROUP_DIM
    pooled = []
    for cb in range(B_WIDTH // LANES):
        cols = slice(cb * LANES, (cb + 1) * LANES)
        w_a, w_b = POOL_WINDOWS[2 * cb], POOL_WINDOWS[2 * cb + 1]
        centred = {}
        for k, w in enumerate(POOL_WINDOWS):
            if w > w_b:
                break
            src = zx_ref if k == 0 else ts_ref.at[k - 1]
            shift = w // 2
            ts_ref[k, pad:pad + ext, cols] = src[pad:pad + ext, cols] + src[pad - shift:pad - shift + ext, cols]
            if w in (w_a, w_b):
                start = pad + hl + w // 2 - 1
                centred[w] = ts_ref[k, start:start + tm, cols]
        half = jnp.where(first_group, w_a // 2, w_b // 2)
        cnt = (jnp.minimum(pos + half, seq) - jnp.maximum(pos - half, 0)).astype(F32)
        wsum = jnp.where(first_group, centred[w_a], centred[w_b])
        pooled.append(wsum / cnt - zx_ref[pad + hl:pad + hl + tm, cols])
    return jnp.concatenate(pooled, axis=-1)


def _mix_in_kernel(xp_ref, x_ref, xn_ref, g_ref, w_ref, b_ref, vg_ref, ws_ref, bs_ref, ones_ref,
                   wp_ref, bp_ref, ps_ref,
                   ya_ref, yb_ref, g0_ref, g1_ref, g2_ref, qkv_ref, wb_ref, zx_ref, ts_ref, *, layer, tm, seq):
    g_ref, b_ref, vg_ref, bp_ref, ps_ref = _layer_rows(layer, g_ref, b_ref, vg_ref, bp_ref, ps_ref)
    _cast_once(w_ref, wb_ref)
    tiles = seq // tm
    t = pl.program_id(0) % tiles
    g = g_ref[...]
    hb = _rms(x_ref[...], g).astype(BF16)

    def seg(a, b):
        return _dot(hb, wb_ref[:, a:b]) + b_ref[:, a:b]

    off_b = 2 * A_WIDTH
    off_c = off_b + B_WIDTH
    uv = seg(0, off_b)
    hl = POOL_HALO
    h_halo = jnp.concatenate([_rms(xp_ref[...], g), _rms(xn_ref[...], g)], axis=0).astype(BF16)
    z_halo = _dot(h_halo, wb_ref[:, off_b:off_c]) + b_ref[:, off_b:off_c]
    z_main = seg(off_b, off_c)
    qk = seg(off_c, off_c + 2 * C_WIDTH)
    parts = (qk[:, :C_WIDTH] * (HEAD_DIM ** -0.5), qk[:, C_WIDTH:], seg(off_c + 2 * C_WIDTH, off_c + 3 * C_WIDTH))

    pooled = _pool_minus_identity(z_halo[0:hl], z_main, z_halo[hl:], zx_ref, ts_ref, t, tiles, tm, seq).astype(BF16)
    yb_ref[...] = ((_dot(pooled, wp_ref[...]) + bp_ref[...]) * ps_ref[...]).astype(BF16)

    for grp, (out_ref, (_, dil)) in enumerate(zip((g0_ref, g1_ref, g2_ref), DILATED_CONFIGS)):
        for part in range(3):
            tile = parts[part][:, grp * C_GROUP_WIDTH:(grp + 1) * C_GROUP_WIDTH]
            if dil == 1:
                out_ref[part] = tile.astype(BF16)
                continue
            qkv_ref[part, grp] = tile
            for r in range(dil):
                piece = qkv_ref[part, grp, pl.ds(r, tm // dil, stride=dil), :]
                out_ref[part, :, r * C_GROUP_WIDTH:(r + 1) * C_GROUP_WIDTH] = piece.astype(BF16)

    uv = _gelu_tanh(uv)
    u, v = uv[:, :A_WIDTH], uv[:, A_WIDTH:]
    vsq = (v * v).astype(BF16)
    cut = 2 * LANES
    ssum = jnp.concatenate([_dot(vsq[:, :cut], ones_ref[:cut, :cut]), _dot(vsq[:, cut:], ones_ref[cut:, cut:])],
                           axis=1)
    vn = v * lax.rsqrt(ssum * (1.0 / HEAD_DIM) + EPS) * vg_ref[...]
    first_head = lax.broadcasted_iota(jnp.int32, (GMLP_CHUNK, LANES), 1) < HEAD_DIM
    chunk_rows = [slice(c * GMLP_CHUNK, (c + 1) * GMLP_CHUNK) for c in range(tm // GMLP_CHUNK)]
    for j in range(A_WIDTH // LANES):
        cols = slice(j * LANES, (j + 1) * LANES)
        rhs = jnp.concatenate(
            [jnp.concatenate([jnp.where(first_head, vn[rows, cols], 0.0).astype(BF16),
                              jnp.where(first_head, 0.0, vn[rows, cols]).astype(BF16)], axis=0)
             for rows in chunk_rows], axis=1)
        s = _dot(ws_ref[j], rhs)
        for c, rows in enumerate(chunk_rows):
            s_c = s[:, c * LANES:(c + 1) * LANES] + bs_ref[:, cols]
            ya_ref[rows, cols] = (u[rows, cols] * s_c).astype(BF16)


def _mix_in(x, g, w_in, layer, b_in, v_gain, ws_cat, bs_full, ones_bd, wp_bd, bp, ps, *, tm, seq):
    t, d = x.shape
    in_width = w_in.shape[2]
    hl = POOL_HALO
    blk_per_tile = tm // hl
    n_hl = t // hl
    qkv_shapes = [jax.ShapeDtypeStruct((3, t // dil, dil * C_GROUP_WIDTH), BF16) for _, dil in DILATED_CONFIGS]
    qkv_specs = [pl.BlockSpec((3, tm // dil, dil * C_GROUP_WIDTH), lambda i: (0, i, 0))
                 for _, dil in DILATED_CONFIGS]
    return pl.pallas_call(
        functools.partial(_mix_in_kernel, layer=layer, tm=tm, seq=seq),
        grid=(t // tm,),
        in_specs=[pl.BlockSpec((hl, d), lambda i: (jnp.maximum(i * blk_per_tile - 1, 0), 0)),
                  pl.BlockSpec((tm, d), lambda i: (i, 0)),
                  pl.BlockSpec((hl, d), lambda i: (jnp.minimum((i + 1) * blk_per_tile, n_hl - 1), 0)),
                  _full(g.shape), _layer_block(w_in, layer), _full(b_in.shape),
                  _full(v_gain.shape), _layer_block(ws_cat, layer), _layer_block(bs_full, layer),
                  _full(ones_bd.shape), _layer_block(wp_bd, layer), _full(bp.shape),
                  _full(ps.shape)],
        out_specs=[pl.BlockSpec((tm, A_WIDTH), lambda i: (i, 0)),
                   pl.BlockSpec((tm, B_WIDTH), lambda i: (i, 0))] + qkv_specs,
        out_shape=[jax.ShapeDtypeStruct((t, A_WIDTH), BF16),
                   jax.ShapeDtypeStruct((t, B_WIDTH), BF16)] + qkv_shapes,
        scratch_shapes=[pltpu.VMEM((3, C_GROUPS, tm, C_GROUP_WIDTH), F32), pltpu.VMEM((d, in_width), BF16),
                        pltpu.VMEM((tm + 3 * hl, B_WIDTH), F32),
                        pltpu.VMEM((len(POOL_WINDOWS), tm + 3 * hl, B_WIDTH), F32)],
        compiler_params=_cparams(1, "arbitrary"),
        name="mix_in",
    )(x, x, x, g, w_in, b_in, v_gain, ws_cat, bs_full, ones_bd, wp_bd, bp, ps)


def _attn_kernel(g0_ref, g1_ref, g2_ref, bias_ref, yc_ref, kx0_ref, vx0_ref, kx1_ref, vx1_ref, kx2_ref, vx2_ref,
                 nat_ref, *, seq):
    qc = 2 * RADIUS
    kc = qc + 2 * RADIUS
    first_head = lax.broadcasted_iota(jnp.int32, (qc, LANES), 1) < HEAD_DIM

    def per_lane_head(a):
        a = jnp.broadcast_to(a, (2 * qc, LANES))
        return jnp.where(first_head, a[:qc], a[qc:])

    groups = zip((g0_ref, g1_ref, g2_ref), (kx0_ref, kx1_ref, kx2_ref), (vx0_ref, vx1_ref, vx2_ref), DILATED_CONFIGS)
    for grp, (qkv_ref, kx_ref, vx_ref, (_, dil)) in enumerate(groups):
        sub_len = seq // dil
        n_chunks = sub_len // qc
        halo = jnp.zeros((RADIUS, dil * LANES), BF16)
        for x_ref, part in ((kx_ref, 1), (vx_ref, 2)):
            x_ref[0:RADIUS] = halo
            x_ref[RADIUS:RADIUS + sub_len] = qkv_ref[part]
            x_ref[RADIUS + sub_len:] = halo
        for c in range(n_chunks):
            rows = slice(c * qc, (c + 1) * qc)
            variant = 0 if c == 0 else (2 if c == n_chunks - 1 else 1)
            for r in range(dil):
                lanes = slice(r * LANES, (r + 1) * LANES)
                q = qkv_ref[0, rows, lanes]
                keys = kx_ref[c * qc:c * qc + kc, lanes]
                vals = vx_ref[c * qc:c * qc + kc, lanes]
                zero = jnp.zeros_like(q)
                q2 = jnp.concatenate([jnp.where(first_head, q, zero), jnp.where(first_head, zero, q)], axis=0)
                s = _dot_nt(q2, keys) + bias_ref[grp, variant]
                m = jnp.max(s, axis=-1, keepdims=True)
                p = jnp.exp(s - m).astype(BF16)
                o2 = _dot(p, jnp.concatenate([vals, jnp.ones_like(vals)], axis=1))
                l_both = per_lane_head(o2[:, LANES:])
                natural = slice(c * qc, (c + 1) * qc) if dil == 1 else pl.ds(c * qc * dil + r, qc, stride=dil)
                nat_ref[2 * grp, natural, :] = per_lane_head(o2[:, :LANES]) / l_both
                nat_ref[2 * grp + 1, natural, :] = per_lane_head(m) + jnp.log(l_both)

    l0, l1, l2 = nat_ref[1], nat_ref[3], nat_ref[5]
    m = jnp.maximum(l0, jnp.maximum(l1, l2))
    e0, e1, e2 = jnp.exp(l0 - m), jnp.exp(l1 - m), jnp.exp(l2 - m)
    yc_ref[...] = ((e0 * nat_ref[0] + e1 * nat_ref[2] + e2 * nat_ref[4]) / (e0 + e1 + e2)).astype(BF16)


def _attn(qkv, biases, *, batch, seq):
    assert C_GROUPS == 3 and all(seq // dil >= 4 * RADIUS for _, dil in DILATED_CONFIGS)
    views, specs, scratch = [], [], []
    for arr, (_, dil) in zip(qkv, DILATED_CONFIGS):
        sub_len, width = seq // dil, dil * C_GROUP_WIDTH
        views.append(arr.reshape(3, batch, sub_len, width))
        specs.append(pl.BlockSpec((3, None, sub_len, width), lambda b: (0, b, 0, 0)))
        scratch += [pltpu.VMEM((sub_len + 2 * RADIUS, width), BF16)] * 2
    yc = pl.pallas_call(
        functools.partial(_attn_kernel, seq=seq),
        grid=(batch,),
        in_specs=specs + [_full(biases.shape)],
        out_specs=pl.BlockSpec((None, seq, C_GROUP_WIDTH), lambda b: (b, 0, 0)),
        out_shape=jax.ShapeDtypeStruct((batch, seq, C_GROUP_WIDTH), BF16),
        scratch_shapes=scratch + [pltpu.VMEM((2 * C_GROUPS, seq, C_GROUP_WIDTH), F32)],
        compiler_params=_cparams(1),
        name="dilated_attn",
    )(*views, biases)
    return yc.reshape(batch * seq, C_GROUP_WIDTH)


def _t5_bucket(rel):
    nb = N_BUCKETS // 2
    ret = (rel > 0).astype(np.int32) * nb
    n = np.abs(rel)
    max_exact = nb // 2
    large = max_exact + (np.log(np.maximum(n, 1) / max_exact)
                         / math.log(MAX_DISTANCE / max_exact) * (nb - max_exact)).astype(np.int32)
    large = np.minimum(large, nb - 1)
    return ret + np.where(n < max_exact, n, large)


def _attn_biases(rel_table):
    qc, kc = 2 * RADIUS, 4 * RADIUS
    n = qc + kc - 1
    delta = (np.arange(n) + qc - 1) % n - (qc + RADIUS - 1)
    dils = np.array([dil for _, dil in DILATED_CONFIGS])
    buckets = _t5_bucket(delta[None, :] * dils[:, None])
    onehot = (buckets[..., None] == np.arange(N_BUCKETS)).astype(np.float32)
    table = rel_table.reshape(N_BUCKETS, C_GROUPS, 2)
    per_delta = jnp.einsum('gnb,bgh->ghn', onehot, table, precision=lax.Precision.HIGHEST)
    per_delta = jnp.where(np.abs(delta) <= RADIUS, per_delta, NEG_INF)
    reps = -(-(qc * (n - 1)) // n)
    band = jnp.tile(per_delta, (1, 1, reps))[..., :qc * (n - 1)].reshape(C_GROUPS, 2, qc, n - 1)[..., :kc]
    key = np.arange(kc)
    variants = jnp.stack([jnp.where(key < RADIUS, NEG_INF, band), band,
                          jnp.where(key >= kc - RADIUS, NEG_INF, band)], axis=1)
    return variants.reshape(C_GROUPS, 3, 2 * qc, kc)


def _mix_out_kernel(x_ref, ya_ref, yb_ref, yc_ref, wo_ref, bo_ref, xg_ref, qk_ref, vo_ref, xbo_ref, wu_ref,
                    out_ref, wub_ref, xp_ref, wob_ref, *, layer, tm, sub):
    bo_ref, xg_ref, xbo_ref = _layer_rows(layer, bo_ref, xg_ref, xbo_ref)
    _cast_once(wo_ref, wob_ref)
    wub_ref[...] = wu_ref[...].astype(BF16)

    for s in range(tm // sub):
        rows = slice(s * sub, (s + 1) * sub)
        y = _dot(jnp.concatenate([ya_ref[rows, :], yb_ref[rows, :], yc_ref[rows, :]], axis=1), wob_ref[...])
        out_ref[rows, :] = x_ref[rows, :] + y + bo_ref[...]
    out_ref[...] = _cross_attend(out_ref[...], xg_ref, qk_ref, vo_ref, xbo_ref, xp_ref)


def _mix_out(x, ya, yb, yc, w_out, b_out, xg, qk, vo, layer, xb_o, w_up, *, tm, sub, seq):
    t, d = x.shape
    tiles = seq // tm
    hm = qk.shape[-1]
    steps = t // tm
    up_width = w_up.shape[2]
    panel = d // steps
    assert d % steps == 0 and panel % BF16_SUBLANES == 0
    row = lambda w: pl.BlockSpec((tm, w), lambda i: (i, 0))
    return pl.pallas_call(
        functools.partial(_mix_out_kernel, layer=layer, tm=tm, sub=sub),
        grid=(steps,),
        in_specs=[row(d), row(A_WIDTH), row(B_WIDTH), row(C_GROUP_WIDTH),
                  _layer_block(w_out, layer), _full(b_out.shape), _full(xg.shape),
                  pl.BlockSpec((None, None, d, hm), lambda i: (layer, i // tiles, 0, 0)),
                  pl.BlockSpec((None, None, hm, d), lambda i: (layer, i // tiles, 0, 0)),
                  _full(xb_o.shape),
                  pl.BlockSpec((None, panel, up_width), lambda i: (layer, i, 0))],
        out_specs=[row(d), pl.BlockSpec((panel, up_width), lambda i: (i, 0))],
        out_shape=[jax.ShapeDtypeStruct((t, d), F32), jax.ShapeDtypeStruct((d, up_width), BF16)],
        scratch_shapes=[pltpu.VMEM((tm, hm), BF16),
                        pltpu.VMEM(w_out.shape[1:], BF16)],
        compiler_params=_cparams(1, "arbitrary"),
        name="mix_out_xattn",
    )(x, ya, yb, yc, w_out, b_out, xg, qk, vo, xb_o, w_up)


def _mem_fold_kernel(mem_ref, g_ref, wkv_ref, wq_ref, wo_ref, qk_ref, vo_ref):
    d = mem_ref.shape[1]
    m = mem_ref.shape[0]
    hd = d // X_HEADS
    mem_n = _rms(mem_ref[...], g_ref[...]).astype(BF16)
    kv = _dot(mem_n, wkv_ref[...].astype(BF16)).astype(BF16)
    for h in range(X_HEADS):
        cols = slice(h * hd, (h + 1) * hd)
        wq_h = wq_ref[:, cols].astype(BF16)
        qk_ref[:, h * m:(h + 1) * m] = (_dot_nt(wq_h, kv[:, cols]) * (hd ** -0.5)).astype(BF16)
        wo_h = wo_ref[cols, :].astype(BF16)
        vo_ref[h * m:(h + 1) * m, :] = _dot(kv[:, d + h * hd:d + (h + 1) * hd], wo_h).astype(BF16)


def _mem_fold(mem, g, w_kv, w_q, w_o):
    batch, m, d = mem.shape
    depth = w_kv.shape[0]

    def per_layer(stack):
        return pl.BlockSpec((None,) + stack.shape[1:], lambda l, b: (l, 0, 0))

    return pl.pallas_call(
        _mem_fold_kernel,
        grid=(depth, batch),
        in_specs=[pl.BlockSpec((None, m, d), lambda l, b: (b, 0, 0)), pl.BlockSpec((1, d), lambda l, b: (0, 0)),
                  per_layer(w_kv), per_layer(w_q), per_layer(w_o)],
        out_specs=[pl.BlockSpec((None, None, d, X_HEADS * m), lambda l, b: (l, b, 0, 0)),
                   pl.BlockSpec((None, None, X_HEADS * m, d), lambda l, b: (l, b, 0, 0))],
        out_shape=[jax.ShapeDtypeStruct((depth, batch, d, X_HEADS * m), BF16),
                   jax.ShapeDtypeStruct((depth, batch, X_HEADS * m, d), BF16)],
        compiler_params=_cparams(2),
        name="mem_fold",
    )(mem, g, w_kv, w_q, w_o)


def _cross_attend(x, g_ref, qk_ref, vo_ref, bo_ref, p_ref):
    hb = _rms(x, g_ref[...]).astype(BF16)
    s_all = _dot(hb, qk_ref[...])
    m_len = qk_ref.shape[1] // X_HEADS
    for h in range(X_HEADS):
        cols = slice(h * m_len, (h + 1) * m_len)
        s = s_all[:, cols]
        m = jnp.max(s, axis=-1, keepdims=True)
        p = jnp.exp(s - m)
        l = jnp.sum(p, axis=-1, keepdims=True)
        p_ref[:, cols] = (p / l).astype(BF16)
    return x + _dot(p_ref[...], vo_ref[...]) + bo_ref[...]


def _ffn_kernel(xp_ref, x_ref, xn_ref, g_ref, wu_ref, bu_ref, cw_ref, cb_ref, wd_ref, bd_ref, fg_ref,
                out_ref, hb_ref, u0_ref, u1_ref, u2_ref, u3_ref, acc_ref, wdb_ref, *, layer, tm, sub, seq, d_ff, cn,
                final_norm):
    g_ref, bu_ref, cb_ref, bd_ref = _layer_rows(layer, g_ref, bu_ref, cb_ref, bd_ref)
    _cast_once(wd_ref, wdb_ref)
    u_bufs = ((u0_ref, u1_ref), (u2_ref, u3_ref))
    tiles = seq // tm
    t = pl.program_id(0) % tiles
    hl = FFN_HALO
    g = g_ref[...]
    hb_ref[...] = jnp.concatenate([_rms(xp_ref[...], g), _rms(x_ref[...], g), _rms(xn_ref[...], g)],
                                  axis=0).astype(BF16)
    n_sub = tm // sub
    rows = sub + 2 * hl
    n_chunks = d_ff // cn

    def conv(u_ref, ue, cols, a):
        bias = bu_ref[:, cols]
        c0, c1, c2 = cw_ref[0:1, cols], cw_ref[1:2, cols], cw_ref[2:3, cols]
        u_ref[0, 1:1 + rows, :] = ue
        u_ref[1, hl - 1:hl - 1 + rows, :] = ue
        if a == 0:
            u_ref[0, hl:hl + 1, :] = jnp.where(t > 0, ue[hl - 1:hl, :], -bias)
        if a == n_sub - 1:
            u_ref[1, 2 * hl + sub - 1:2 * hl + sub, :] = jnp.where(t < tiles - 1, ue[hl + sub:hl + sub + 1, :], -bias)
        return (c0 * u_ref[0, hl:hl + sub, :] + c1 * ue[hl:hl + sub, :] + c2 * u_ref[1, 2 * hl:2 * hl + sub, :]
                + (bias * (c0 + c1 + c2) + cb_ref[:, cols]))

    def up(a, j):
        hb = hb_ref[a * sub:a * sub + rows, :]
        return (_dot(hb, wu_ref[:, j * cn:(j + 1) * cn]), _dot(hb, wu_ref[:, d_ff + j * cn:d_ff + (j + 1) * cn]))

    def down(a, j, act):
        part = _dot(act, wdb_ref[j * cn:(j + 1) * cn, :])
        if j == 0:
            acc_ref[...] = part
        else:
            acc_ref[...] += part
        if j == n_chunks - 1:
            out_rows = slice(a * sub, (a + 1) * sub)
            y = x_ref[out_rows, :] + acc_ref[...] + bd_ref[...]
            if final_norm:
                y = _rms(y, fg_ref[...])
            out_ref[out_rows, :] = y

    items = [(a, j) for a in range(n_sub) for j in range(n_chunks)]
    ue = up(*items[0])
    act_prev = None
    for idx, (a, j) in enumerate(items):
        ue_next = up(*items[idx + 1]) if idx + 1 < len(items) else None
        if act_prev is not None:
            down(*items[idx - 1], act_prev)
        gate = conv(u_bufs[idx % 2][0], ue[0], slice(j * cn, (j + 1) * cn), a)
        val = conv(u_bufs[idx % 2][1], ue[1], slice(d_ff + j * cn, d_ff + (j + 1) * cn), a)
        act_prev = (gate * jax.nn.sigmoid(gate) * val).astype(BF16)
        ue = ue_next
    down(*items[-1], act_prev)


def _ffn(x, g, w_up, layer, b_up, conv_w, conv_b, w_down, b_down, final_g, *, tm, sub, seq, cn, final_norm):
    t, d = x.shape
    d_ff = w_down.shape[1]
    hl = FFN_HALO
    blk_per_tile = tm // hl
    n_hl = t // hl
    row = pl.BlockSpec((tm, d), lambda i: (i, 0))
    return pl.pallas_call(
        functools.partial(_ffn_kernel, layer=layer, tm=tm, sub=sub, seq=seq, d_ff=d_ff, cn=cn,
                          final_norm=final_norm),
        grid=(t // tm,),
        in_specs=[pl.BlockSpec((hl, d), lambda i: (jnp.maximum(i * blk_per_tile - 1, 0), 0)),
                  row,
                  pl.BlockSpec((hl, d), lambda i: (jnp.minimum((i + 1) * blk_per_tile, n_hl - 1), 0)),
                  _full(g.shape), _full(w_up.shape), _full(b_up.shape),
                  _layer_block(conv_w, layer), _full(conv_b.shape), _layer_block(w_down, layer),
                  _full(b_down.shape), _full((1, d))],
        out_specs=row,
        out_shape=jax.ShapeDtypeStruct((t, d), F32),
        scratch_shapes=([pltpu.VMEM((tm + 2 * hl, d), BF16)] + [pltpu.VMEM((2, sub + 3 * hl, cn), F32)] * 4
                        + [pltpu.VMEM((sub, d), F32), pltpu.VMEM((d_ff, d), BF16)]),
        compiler_params=_cparams(1, "arbitrary"),
        name="conv_ffn",
    )(x, x, x, g, w_up, b_up, conv_w, conv_b, w_down, b_down, final_g)


def _block_diag(blocks):
    *lead, n, a, b = blocks.shape
    eye = jnp.eye(n, dtype=blocks.dtype)
    return (eye[:, None, :, None] * blocks[..., :, :, None, :]).reshape(*lead, n * a, n * b)


def kernel(x, mem, rel_table, mem_norm_g, norm_mix_g, w_in, b_in, gmlp_v_g, gmlp_w_s, gmlp_b_s, pool_w, pool_b, pool_scale, w_out, b_out, norm_mem_g, xattn_w_q, xattn_w_kv, xattn_w_o, xattn_b_o, norm_ffn_g, ffn_w_up, ffn_b_up, ffn_conv_w, ffn_conv_b, ffn_w_down, ffn_b_down, final_norm_g):
    batch, seq, d = x.shape
    mem_len = mem.shape[1]
    depth = w_in.shape[0]
    t = batch * seq
    assert all(w // (2 * dil) == RADIUS for w, dil in DILATED_CONFIGS)
    assert x.dtype == F32 and all(seq % rows == 0 for rows in (MIX_IN_ROWS, MIX_OUT_ROWS, FFN_ROWS))
    assert FFN_ROWS % FFN_PASS_ROWS == 0 and ffn_w_down.shape[1] % FFN_CHUNK == 0

    rows = lambda a: a.reshape(depth, -1)
    ones_bd = jnp.asarray(np.kron(np.eye(A_HEADS), np.ones((HEAD_DIM, HEAD_DIM))), BF16)
    biases = _attn_biases(rel_table)
    qk, vo = _mem_fold(mem, mem_norm_g.reshape(1, d), xattn_w_kv, xattn_w_q, xattn_w_o)
    ws_cat = (gmlp_w_s.reshape(depth, A_HEADS // 2, 2, GMLP_CHUNK, GMLP_CHUNK).transpose(0, 1, 3, 2, 4)
              .reshape(depth, A_HEADS // 2, GMLP_CHUNK, 2 * GMLP_CHUNK).astype(BF16))
    bs_full = jnp.repeat(jnp.swapaxes(gmlp_b_s, 1, 2), HEAD_DIM, axis=2)
    wp_bd = _block_diag(pool_w).astype(BF16)

    xf = x.reshape(t, d)
    for l in range(depth):
        ya, yb, *qkv = _mix_in(xf, rows(norm_mix_g), w_in, l, rows(b_in), rows(gmlp_v_g), ws_cat, bs_full, ones_bd,
                               wp_bd, rows(pool_b), rows(pool_scale), tm=MIX_IN_ROWS, seq=seq)
        yc = _attn(qkv, biases, batch=batch, seq=seq)
        xf, w_up = _mix_out(xf, ya, yb, yc, w_out, rows(b_out), rows(norm_mem_g), qk, vo, l, rows(xattn_b_o),
                            ffn_w_up, tm=MIX_OUT_ROWS, sub=MIX_OUT_PASS_ROWS, seq=seq)
        xf = _ffn(xf, rows(norm_ffn_g), w_up, l, rows(ffn_b_up), ffn_conv_w, rows(ffn_conv_b), ffn_w_down,
                  rows(ffn_b_down), final_norm_g.reshape(1, d),
                  tm=FFN_ROWS, sub=FFN_PASS_ROWS, seq=seq, cn=FFN_CHUNK, final_norm=(l == depth - 1))
    return xf.reshape(batch, seq, d)
```
